```python
import math
import jax, jax.numpy as jnp
from jax import lax
import numpy as np

D_MODEL = 2048
BATCH = 1
SEQ = 16384
DEPTH = 1

HEAD_DIM = 64
SWA_Q_HEADS = 16
SWA_KV_HEADS = 2
SWA_GROUP = SWA_Q_HEADS // SWA_KV_HEADS
SB_HEADS = 16
WINDOW = 128
BLOCK = 128
REL_BUCKETS = 32
REL_MAX_DIST = 128
D_FF = 5632
CONV_WIDTH = 3
EPS = 1e-6
NEG_INF = -1e30

SWA_Q_W = SWA_Q_HEADS * HEAD_DIM
SWA_KV_W = SWA_KV_HEADS * HEAD_DIM
SB_W = SB_HEADS * HEAD_DIM
SWA_WIDTH = SWA_Q_W
D_MIX = SWA_WIDTH + SB_W
D_IN = SWA_Q_W + 2 * SWA_KV_W + 3 * SB_W

kernel_name = "hymba_swa_sink_stickbreak_convglu"


def rmsnorm(x, g):
    xf = x.astype(jnp.float32)
    y = xf * lax.rsqrt(jnp.mean(xf * xf, axis=-1, keepdims=True) + EPS)
    return y.astype(x.dtype) * g


def rel_bucket(dist):
    max_exact = REL_BUCKETS // 2
    d = jnp.maximum(dist, 1).astype(jnp.float32)
    large = max_exact + (jnp.log(d / max_exact) / math.log(REL_MAX_DIST / max_exact)
                         * (REL_BUCKETS - max_exact)).astype(jnp.int32)
    large = jnp.minimum(large, REL_BUCKETS - 1)
    return jnp.where(dist < max_exact, dist, large)


def swa_sink_attention(qa, ka, va, rel_bias, sinks):
    B, S, _ = qa.shape
    N = S // BLOCK
    q = qa.reshape(B, N, BLOCK, SWA_KV_HEADS, SWA_GROUP, HEAD_DIM)
    k = ka.reshape(B, N, BLOCK, SWA_KV_HEADS, HEAD_DIM)
    v = va.reshape(B, N, BLOCK, SWA_KV_HEADS, HEAD_DIM)
    shift = lambda a: jnp.pad(a, ((0, 0), (1, 0), (0, 0), (0, 0), (0, 0)))[:, :-1]
    k2 = jnp.concatenate([shift(k), k], axis=2)
    v2 = jnp.concatenate([shift(v), v], axis=2)
    s = jnp.einsum('bnqhgd,bnkhd->bnhgqk', q, k2).astype(jnp.float32) * (HEAD_DIM ** -0.5)

    qi = jnp.arange(BLOCK, dtype=jnp.int32)[:, None]
    kj = jnp.arange(2 * BLOCK, dtype=jnp.int32)[None, :]
    dist = qi + BLOCK - kj
    in_win = (dist >= 0) & (dist < WINDOW)
    blk = jnp.arange(N, dtype=jnp.int32)[:, None, None]
    key_ok = (blk * BLOCK - BLOCK + kj[None]) >= 0
    valid = in_win[None] & key_ok

    bias = rel_bias[rel_bucket(jnp.clip(dist, 0, None))]
    bias = bias.transpose(2, 0, 1).reshape(SWA_KV_HEADS, SWA_GROUP, BLOCK, 2 * BLOCK)
    logits = jnp.where(valid[None, :, None, None], s + bias.astype(jnp.float32), NEG_INF)

    sink = sinks.astype(jnp.float32).reshape(1, 1, SWA_KV_HEADS, SWA_GROUP, 1, 1)
    m = jnp.maximum(logits.max(axis=-1, keepdims=True), sink)
    p = jnp.exp(logits - m)
    w = p / (p.sum(axis=-1, keepdims=True) + jnp.exp(sink - m))
    o = jnp.einsum('bnhgqk,bnkhd->bnqhgd', w.astype(v2.dtype), v2)
    return o.reshape(B, S, SWA_Q_W)


def stick_breaking_attention(qs, ks, vs):
    B, S, _ = qs.shape
    N = S // BLOCK
    q = qs.reshape(B, N, BLOCK, SB_HEADS, HEAD_DIM).transpose(1, 0, 3, 2, 4)
    k = ks.reshape(B, S, SB_HEADS, HEAD_DIM).transpose(0, 2, 1, 3)
    v = vs.reshape(B, S, SB_HEADS, HEAD_DIM).transpose(0, 2, 1, 3)
    scale = HEAD_DIM ** -0.5
    local = jnp.arange(BLOCK, dtype=jnp.int32)

    def per_query_block(args):
        qblk, n = args
        qpos = n * BLOCK + local

        def body(step, carry):
            o, acc = carry
            j = n - step
            kb = lax.dynamic_slice_in_dim(k, j * BLOCK, BLOCK, axis=2)
            vb = lax.dynamic_slice_in_dim(v, j * BLOCK, BLOCK, axis=2)
            z = jnp.einsum('bhqd,bhkd->bhqk', qblk, kb).astype(jnp.float32) * scale
            kpos = j * BLOCK + local
            causal = kpos[None, :] < qpos[:, None]
            log1m = jnp.where(causal, -jax.nn.softplus(z), 0.0)
            row = log1m.sum(axis=-1, keepdims=True)
            suffix = row - jnp.cumsum(log1m, axis=-1)
            log_a = jax.nn.log_sigmoid(z) + suffix + acc[..., None]
            a = jnp.where(causal, jnp.exp(log_a), 0.0)
            o = o + jnp.einsum('bhqk,bhkd->bhqd', a, vb.astype(jnp.float32))
            return (o, acc + row[..., 0])

        o0 = jnp.zeros(qblk.shape, jnp.float32)
        acc0 = jnp.zeros(qblk.shape[:-1], jnp.float32)
        o, _ = lax.fori_loop(0, n + 1, body, (o0, acc0))
        return o.astype(qblk.dtype)

    out = lax.map(per_query_block, (q, jnp.arange(N, dtype=jnp.int32)))
    return out.transpose(1, 0, 3, 2, 4).reshape(B, S, SB_W)


def conv_glu(h, w_up, w_conv, b_conv, w_down):
    S = h.shape[1]
    u = h @ w_up
    gate, val = u[..., :D_FF], u[..., D_FF:]
    gp = jnp.pad(gate, ((0, 0), (CONV_WIDTH - 1, 0), (0, 0)))
    gc = b_conv
    for tap in range(CONV_WIDTH):
        gc = gc + gp[:, tap:tap + S] * w_conv[tap]
    return (jax.nn.silu(gc) * val) @ w_down


def setup_inputs(seed: int = 0) -> dict:
    key = jax.random.key(seed)
    ks = jax.random.split(key, 16)
    nrm = lambda k, shape, scale: jax.random.normal(k, shape, jnp.float32) * scale
    return {
        "x": nrm(ks[0], (BATCH, SEQ, D_MODEL), 1.0),
        "w_in": nrm(ks[1], (D_MODEL, D_IN), D_MODEL ** -0.5),
        "g_attn_norm": 1.0 + nrm(ks[2], (D_MODEL,), 0.02),
        "rel_bias": nrm(ks[3], (REL_BUCKETS, SWA_Q_HEADS), 0.2),
        "swa_sinks": nrm(ks[4], (SWA_Q_HEADS,), 0.5),
        "g_swa_out": 1.0 + nrm(ks[5], (SWA_WIDTH,), 0.02),
        "g_sb_out": 1.0 + nrm(ks[6], (SB_W,), 0.02),
        "w_out": nrm(ks[7], (D_MIX, D_MODEL), D_MIX ** -0.5),
        "g_mlp_norm": 1.0 + nrm(ks[8], (D_MODEL,), 0.02),
        "w_up": nrm(ks[9], (D_MODEL, 2 * D_FF), D_MODEL ** -0.5),
        "w_conv": nrm(ks[10], (CONV_WIDTH, D_FF), CONV_WIDTH ** -0.5),
        "b_conv": nrm(ks[11], (D_FF,), 0.01),
        "w_down": nrm(ks[12], (D_FF, D_MODEL), D_FF ** -0.5),
        "g_final": 1.0 + nrm(ks[13], (D_MODEL,), 0.02),
    }


def reference(x, w_in, g_attn_norm, rel_bias, swa_sinks, g_swa_out, g_sb_out, w_out,
              g_mlp_norm, w_up, w_conv, b_conv, w_down, g_final):
    h = x
    splits = np.cumsum([SWA_Q_W, SWA_KV_W, SWA_KV_W, SB_W, SB_W]).tolist()
    for _ in range(DEPTH):
        hn = rmsnorm(h, g_attn_norm)
        proj = hn @ w_in
        qa, ka, va, qs, ks_, vs = jnp.split(proj, splits, axis=-1)
        ya = swa_sink_attention(qa, ka, va, rel_bias, swa_sinks)
        yb = stick_breaking_attention(qs, ks_, vs)
        mix = jnp.concatenate([rmsnorm(ya, g_swa_out), rmsnorm(yb, g_sb_out)], axis=-1)
        h = h + mix @ w_out
        h = h + conv_glu(rmsnorm(h, g_mlp_norm), w_up, w_conv, b_conv, w_down)
    return rmsnorm(h, g_final)
```

```python
import functools
import math

import numpy as np
import jax
import jax.numpy as jnp
from jax import lax
from jax.experimental import pallas as pl
from jax.experimental.pallas import tpu as pltpu

D_MODEL = 2048
SEQ = 16384
HEAD_DIM = 64
SWA_Q_HEADS = 16
SWA_KV_HEADS = 2
SWA_GROUP = SWA_Q_HEADS // SWA_KV_HEADS
SB_HEADS = 16
WINDOW = 128
BLOCK = 128
REL_BUCKETS = 32
REL_MAX_DIST = 128
D_FF = 5632
CONV_WIDTH = 3
EPS = 1e-6
NEG_INF = -1e30

SWA_Q_W = SWA_Q_HEADS * HEAD_DIM
SWA_KV_W = SWA_KV_HEADS * HEAD_DIM
SB_W = SB_HEADS * HEAD_DIM
D_MIX = SWA_Q_W + SB_W
D_IN = SWA_Q_W + 2 * SWA_KV_W + 3 * SB_W

LANES = 128
COL_KA = SWA_Q_W // LANES
COL_VA = COL_KA + SWA_KV_W // LANES
COL_QS = COL_VA + SWA_KV_W // LANES
COL_KS = COL_QS + SB_W // LANES
COL_VS = COL_KS + SB_W // LANES

SCALE = HEAD_DIM ** -0.5
HALO = 16
VMEM_LIMIT = 56 * 1024 * 1024

F32 = jnp.float32
BF16 = jnp.bfloat16


def _params(sem, vmem=VMEM_LIMIT):
    return pltpu.CompilerParams(dimension_semantics=sem, vmem_limit_bytes=vmem)


def _inproj_kernel(x_ref, g_ref, w_ref, o_ref, hn_ref):
    @pl.when(pl.program_id(1) == 0)
    def _():
        x = x_ref[...]
        ms = jnp.mean(x * x, axis=-1, keepdims=True)
        hn_ref[...] = (x * lax.rsqrt(ms + EPS) * g_ref[...]).astype(BF16)

    o_ref[...] = jnp.dot(hn_ref[...], w_ref[...], preferred_element_type=F32).astype(BF16)


def _inproj(x, g, w_bf16, tm=512, tn=2176):
    S, D = x.shape
    N = w_bf16.shape[1]
    return pl.pallas_call(
        _inproj_kernel,
        out_shape=jax.ShapeDtypeStruct((S, N), BF16),
        grid=(S // tm, N // tn),
        in_specs=[
            pl.BlockSpec((tm, D), lambda i, j: (i, 0)),
            pl.BlockSpec((1, D), lambda i, j: (0, 0)),
            pl.BlockSpec((D, tn), lambda i, j: (0, j)),
        ],
        out_specs=pl.BlockSpec((tm, tn), lambda i, j: (i, j)),
        scratch_shapes=[pltpu.VMEM((tm, D), BF16)],
        compiler_params=_params(("arbitrary", "arbitrary")),
        name="inproj",
    )(x, g, w_bf16)


def _rel_bucket_table():
    qi = np.arange(BLOCK, dtype=np.int64)[:, None]
    kj = np.arange(2 * BLOCK, dtype=np.int64)[None, :]
    dist = qi + BLOCK - kj
    in_win = (dist >= 0) & (dist < WINDOW)
    dc = np.clip(dist, 0, None)
    max_exact = REL_BUCKETS // 2
    d = np.maximum(dc, 1).astype(np.float32)
    large = max_exact + (np.log(d / np.float32(max_exact)) / np.float32(math.log(REL_MAX_DIST / max_exact))
                         * np.float32(REL_BUCKETS - max_exact)).astype(np.int32)
    large = np.minimum(large, REL_BUCKETS - 1)
    bucket = np.where(dc < max_exact, dc, large).astype(np.int32)
    return np.where(in_win, bucket, -1).astype(np.int32)


def _swa_kernel(q_ref, kp_ref, kc_ref, vp_ref, vc_ref, bucket_ref, relb_ref, sink_ref,
                o_ref, bias_ref):
    n = pl.program_id(0)

    @pl.when(n == 0)
    def _():
        bucket = bucket_ref[...]
        for h in range(SWA_Q_HEADS):
            b = jnp.full(bucket.shape, NEG_INF, F32)
            for r in range(REL_BUCKETS):
                b = jnp.where(bucket == r, relb_ref[r, h], b)
            bias_ref[h] = b

    prev_ok = n > 0
    col = lax.broadcasted_iota(jnp.int32, (BLOCK, 2 * BLOCK), 1)
    key_ok = jnp.logical_or(prev_ok, col >= BLOCK)

    for g in range(SWA_KV_HEADS):
        ks = slice(g * HEAD_DIM, (g + 1) * HEAD_DIM)
        k2 = jnp.concatenate([kp_ref[:, ks], kc_ref[:, ks]], axis=0)
        v2 = jnp.concatenate([vp_ref[:, ks], vc_ref[:, ks]], axis=0)
        for gh in range(SWA_GROUP):
            h = g * SWA_GROUP + gh
            qh = q_ref[:, h * HEAD_DIM:(h + 1) * HEAD_DIM]
            s = lax.dot_general(qh, k2, (((1,), (1,)), ((), ())), preferred_element_type=F32)
            logits = jnp.where(key_ok, s + bias_ref[h], NEG_INF)
            sink = sink_ref[h]
            m = jnp.maximum(jnp.max(logits, axis=-1, keepdims=True), sink)
            p = jnp.exp(logits - m)
            denom = jnp.sum(p, axis=-1, keepdims=True) + jnp.exp(sink - m)
            w = (p / denom).astype(BF16)
            o_ref[:, h * HEAD_DIM:(h + 1) * HEAD_DIM] = jnp.dot(w, v2, preferred_element_type=F32)


def _swa(proj, rel_bias, sinks):
    S = proj.shape[0]
    N = S // BLOCK
    bucket = jnp.asarray(_rel_bucket_table())
    prev = lambda n: jnp.maximum(n - 1, 0)
    return pl.pallas_call(
        _swa_kernel,
        out_shape=jax.ShapeDtypeStruct((S, SWA_Q_W), F32),
        grid=(N,),
        in_specs=[
            pl.BlockSpec((BLOCK, SWA_Q_W), lambda n: (n, 0)),
            pl.BlockSpec((BLOCK, LANES), lambda n: (prev(n), COL_KA)),
            pl.BlockSpec((BLOCK, LANES), lambda n: (n, COL_KA)),
            pl.BlockSpec((BLOCK, LANES), lambda n: (prev(n), COL_VA)),
            pl.BlockSpec((BLOCK, LANES), lambda n: (n, COL_VA)),
            pl.BlockSpec((BLOCK, 2 * BLOCK), lambda n: (0, 0)),
            pl.BlockSpec(memory_space=pltpu.SMEM),
            pl.BlockSpec(memory_space=pltpu.SMEM),
        ],
        out_specs=pl.BlockSpec((BLOCK, SWA_Q_W), lambda n: (n, 0)),
        scratch_shapes=[pltpu.VMEM((SWA_Q_HEADS, BLOCK, 2 * BLOCK), F32)],
        compiler_params=_params(("arbitrary",)),
        name="swa",
    )(proj, proj, proj, proj, proj, bucket, rel_bias, sinks)


def _cumsum_weights():
    kk = np.arange(BLOCK)
    upper = (kk[:, None] > kk[None, :]).astype(np.float32)
    w = np.concatenate([upper, np.ones((BLOCK, BLOCK), np.float32)], axis=1)
    return np.concatenate([w, w], axis=0)


def _sb_kernel(q_ref, k_ref, v_ref, w2_ref, o_ref, acc_ref, oacc_ref):
    n = pl.program_id(1)
    lane = lax.broadcasted_iota(jnp.int32, (BLOCK, LANES), 1)
    first = lane < HEAD_DIM
    q = q_ref[...]
    zq = jnp.zeros_like(q)
    qq = jnp.concatenate([jnp.where(first, q, zq), jnp.where(first, zq, q)], axis=0)
    w2 = w2_ref[...]

    row = lax.broadcasted_iota(jnp.int32, (2 * BLOCK, BLOCK), 0) % BLOCK
    colk = lax.broadcasted_iota(jnp.int32, (2 * BLOCK, BLOCK), 1)
    causal = colk < row

    def tile(j, diag):
        start = pl.multiple_of(j * BLOCK, BLOCK)
        kb = k_ref[pl.ds(start, BLOCK), :]
        vb = v_ref[pl.ds(start, BLOCK), :]
        z = lax.dot_general(qq, kb, (((1,), (1,)), ((), ())), preferred_element_type=F32)
        lg = jnp.log(1.0 + jnp.exp(-jnp.abs(z)))
        logsig = jnp.minimum(z, 0.0) - lg
        log1m = logsig - z
        if diag:
            log1m = jnp.where(causal, log1m, 0.0)
        hi = log1m.astype(BF16)
        lo = (log1m - hi.astype(F32)).astype(BF16)
        r = jnp.dot(jnp.concatenate([hi, lo], axis=1), w2, preferred_element_type=F32)
        log_a = logsig + r[:, :BLOCK]
        if not diag:
            log_a = log_a + acc_ref[...]
        a = jnp.exp(log_a)
        if diag:
            a = jnp.where(causal, a, 0.0)
        pv = jnp.dot(a.astype(BF16), vb, preferred_element_type=F32)
        if diag:
            acc_ref[...] = r[:, BLOCK:]
            oacc_ref[...] = pv
        else:
            acc_ref[...] += r[:, BLOCK:]
            oacc_ref[...] += pv

    tile(n, True)

    def body(s, carry):
        tile(n - 1 - s, False)
        return carry

    lax.fori_loop(0, n, body, 0)
    o_ref[...] = jnp.where(first, oacc_ref[:BLOCK, :], oacc_ref[BLOCK:, :])


def _sb(proj):
    S = proj.shape[0]
    N = S // BLOCK
    pairs = SB_W // LANES
    w2 = jnp.asarray(_cumsum_weights(), dtype=BF16)
    return pl.pallas_call(
        _sb_kernel,
        out_shape=jax.ShapeDtypeStruct((S, SB_W), F32),
        grid=(pairs, N),
        in_specs=[
            pl.BlockSpec((BLOCK, LANES), lambda p, n: (n, COL_QS + p)),
            pl.BlockSpec((S, LANES), lambda p, n: (0, COL_KS + p)),
            pl.BlockSpec((S, LANES), lambda p, n: (0, COL_VS + p)),
            pl.BlockSpec((2 * BLOCK, 2 * BLOCK), lambda p, n: (0, 0)),
        ],
        out_specs=pl.BlockSpec((BLOCK, LANES), lambda p, n: (n, p)),
        scratch_shapes=[pltpu.VMEM((2 * BLOCK, BLOCK), F32), pltpu.VMEM((2 * BLOCK, LANES), F32)],
        compiler_params=_params(("arbitrary", "arbitrary")),
        name="stickbreak",
    )(proj, proj, proj, w2)


def _rms(y):
    return y * lax.rsqrt(jnp.mean(y * y, axis=-1, keepdims=True) + EPS)


def _outproj_kernel(ya_ref, yb_ref, x_ref, ga_ref, gb_ref, gm_ref, w_ref, h_ref, hn_ref):
    ma = (_rms(ya_ref[...]) * ga_ref[...]).astype(BF16)
    mb = (_rms(yb_ref[...]) * gb_ref[...]).astype(BF16)
    mix = jnp.concatenate([ma, mb], axis=-1)
    h = x_ref[...] + jnp.dot(mix, w_ref[...], preferred_element_type=F32)
    h_ref[...] = h
    hn_ref[...] = (_rms(h) * gm_ref[...]).astype(BF16)


def _outproj(ya, yb, x, ga, gb, gm, w_bf16, tm=256):
    S, D = x.shape
    row = lambda i: (i, 0)
    const = lambda i: (0, 0)
    return pl.pallas_call(
        _outproj_kernel,
        out_shape=(jax.ShapeDtypeStruct((S, D), F32), jax.ShapeDtypeStruct((S, D), BF16)),
        grid=(S // tm,),
        in_specs=[
            pl.BlockSpec((tm, SWA_Q_W), row),
            pl.BlockSpec((tm, SB_W), row),
            pl.BlockSpec((tm, D), row),
            pl.BlockSpec((1, SWA_Q_W), const),
            pl.BlockSpec((1, SB_W), const),
            pl.BlockSpec((1, D), const),
            pl.BlockSpec((D_MIX, D), const),
        ],
        out_specs=(pl.BlockSpec((tm, D), row), pl.BlockSpec((tm, D), row)),
        compiler_params=_params(("arbitrary",)),
        name="outproj",
    )(ya, yb, x, ga, gb, gm, w_bf16)


def _convglu_kernel(hn_ref, halo_ref, wg_ref, wv_ref, wc_ref, bc_ref, wd_ref, h_ref, gf_ref,
                    o_ref, lhs_ref, gate_ref, acc_ref, *, tm):
    i = pl.program_id(0)
    f = pl.program_id(1)

    @pl.when(f == 0)
    def _():
        halo = halo_ref[...]
        lhs_ref[:HALO, :] = jnp.where(i > 0, halo, jnp.zeros_like(halo))
        lhs_ref[HALO:, :] = hn_ref[...]
        acc_ref[...] = jnp.zeros_like(acc_ref)

    gate_ref[...] = jnp.dot(lhs_ref[...], wg_ref[...], preferred_element_type=F32)
    val = jnp.dot(hn_ref[...], wv_ref[...], preferred_element_type=F32)
    gc = bc_ref[...]
    for tap in range(CONV_WIDTH):
        off = HALO - (CONV_WIDTH - 1) + tap
        gc = gc + gate_ref[pl.ds(off, tm), :] * wc_ref[tap:tap + 1, :]
    act = (gc * (1.0 / (1.0 + jnp.exp(-gc))) * val).astype(BF16)
    acc_ref[...] += jnp.dot(act, wd_ref[...], preferred_element_type=F32)

    @pl.when(f == pl.num_programs(1) - 1)
    def _():
        h2 = h_ref[...] + acc_ref[...]
        o_ref[...] = _rms(h2) * gf_ref[...]


def _convglu(hn2, h1, w_up_bf16, w_conv, b_conv, w_down_bf16, g_final, tm=512, tf=512):
    S, D = h1.shape
    nf = D_FF // tf
    halo_blocks = tm // HALO
    return pl.pallas_call(
        functools.partial(_convglu_kernel, tm=tm),
        out_shape=jax.ShapeDtypeStruct((S, D), F32),
        grid=(S // tm, nf),
        in_specs=[
            pl.BlockSpec((tm, D), lambda i, f: (i, 0)),
            pl.BlockSpec((HALO, D), lambda i, f: (jnp.maximum(i * halo_blocks - 1, 0), 0)),
            pl.BlockSpec((D, tf), lambda i, f: (0, f)),
            pl.BlockSpec((D, tf), lambda i, f: (0, nf + f)),
            pl.BlockSpec((CONV_WIDTH, tf), lambda i, f: (0, f)),
            pl.BlockSpec((1, tf), lambda i, f: (0, f)),
            pl.BlockSpec((tf, D), lambda i, f: (f, 0)),
            pl.BlockSpec((tm, D), lambda i, f: (i, 0)),
            pl.BlockSpec((1, D), lambda i, f: (0, 0)),
        ],
        out_specs=pl.BlockSpec((tm, D), lambda i, f: (i, 0)),
        scratch_shapes=[
            pltpu.VMEM((tm + HALO, D), BF16),
            pltpu.VMEM((tm + HALO, tf), F32),
            pltpu.VMEM((tm, D), F32),
        ],
        compiler_params=_params(("arbitrary", "arbitrary")),
        name="convglu",
    )(hn2, hn2, w_up_bf16, w_up_bf16, w_conv, b_conv, w_down_bf16, h1, g_final)


def kernel(x, w_in, g_attn_norm, rel_bias, swa_sinks, g_swa_out, g_sb_out, w_out,
           g_mlp_norm, w_up, w_conv, b_conv, w_down, g_final):
    B, S, D = x.shape
    assert (B, S, D) == (1, SEQ, D_MODEL)
    x2 = x.reshape(S, D)

    col = np.ones((1, D_IN), np.float32)
    col[:, :SWA_Q_W] = SCALE
    col[:, COL_QS * LANES:COL_KS * LANES] = SCALE
    w_in_b = (w_in * jnp.asarray(col)).astype(BF16)
    w_out_b = w_out.astype(BF16)
    w_up_b = w_up.astype(BF16)
    w_down_b = w_down.astype(BF16)

    proj = _inproj(x2, g_attn_norm.reshape(1, D), w_in_b)
    ya = _swa(proj, rel_bias, swa_sinks)
    yb = _sb(proj)
    h1, hn2 = _outproj(ya, yb, x2, g_swa_out.reshape(1, -1), g_sb_out.reshape(1, -1),
                       g_mlp_norm.reshape(1, D), w_out_b)
    out = _convglu(hn2, h1, w_up_b, w_conv, b_conv.reshape(1, -1), w_down_b, g_final.reshape(1, D))
    return out.reshape(B, S, D)
```

```python
import functools
import math

import numpy as np
import jax
import jax.numpy as jnp
from jax import lax
from jax.experimental import pallas as pl
from jax.experimental.pallas import tpu as pltpu

D_MODEL = 2048
SEQ = 16384
HEAD_DIM = 64
SWA_Q_HEADS = 16
SWA_KV_HEADS = 2
SWA_GROUP = SWA_Q_HEADS // SWA_KV_HEADS
SB_HEADS = 16
WINDOW = 128
BLOCK = 128
REL_BUCKETS = 32
REL_MAX_DIST = 128
D_FF = 5632
CONV_WIDTH = 3
EPS = 1e-6
NEG_INF = -1e30

SWA_Q_W = SWA_Q_HEADS * HEAD_DIM
SWA_KV_W = SWA_KV_HEADS * HEAD_DIM
SB_W = SB_HEADS * HEAD_DIM
D_MIX = SWA_Q_W + SB_W
D_IN = SWA_Q_W + 2 * SWA_KV_W + 3 * SB_W

LANES = 128
COL_KA = SWA_Q_W // LANES
COL_VA = COL_KA + SWA_KV_W // LANES
COL_QS = COL_VA + SWA_KV_W // LANES
COL_KS = COL_QS + SB_W // LANES
COL_VS = COL_KS + SB_W // LANES

SCALE = HEAD_DIM ** -0.5
HALO = 16
VMEM_LIMIT = 56 * 1024 * 1024

F32 = jnp.float32
BF16 = jnp.bfloat16

PRUNE_LOG = -94.0


def _params(sem, vmem=VMEM_LIMIT):
    return pltpu.CompilerParams(dimension_semantics=sem, vmem_limit_bytes=vmem)


def _inproj_kernel(x_ref, g_ref, w_ref, o_ref, hn_ref):
    @pl.when(pl.program_id(1) == 0)
    def _():
        x = x_ref[...]
        ms = jnp.mean(x * x, axis=-1, keepdims=True)
        hn_ref[...] = (x * lax.rsqrt(ms + EPS) * g_ref[...]).astype(BF16)

    o_ref[...] = jnp.dot(hn_ref[...], w_ref[...], preferred_element_type=F32).astype(BF16)


def _inproj(x, g, w_bf16, tm=512, tn=2176):
    S, D = x.shape
    N = w_bf16.shape[1]
    return pl.pallas_call(
        _inproj_kernel,
        out_shape=jax.ShapeDtypeStruct((S, N), BF16),
        grid=(S // tm, N // tn),
        in_specs=[
            pl.BlockSpec((tm, D), lambda i, j: (i, 0)),
            pl.BlockSpec((1, D), lambda i, j: (0, 0)),
            pl.BlockSpec((D, tn), lambda i, j: (0, j)),
        ],
        out_specs=pl.BlockSpec((tm, tn), lambda i, j: (i, j)),
        scratch_shapes=[pltpu.VMEM((tm, D), BF16)],
        compiler_params=_params(("arbitrary", "arbitrary")),
        name="inproj",
    )(x, g, w_bf16)


def _rel_bucket_table():
    qi = np.arange(BLOCK, dtype=np.int64)[:, None]
    kj = np.arange(2 * BLOCK, dtype=np.int64)[None, :]
    dist = qi + BLOCK - kj
    in_win = (dist >= 0) & (dist < WINDOW)
    dc = np.clip(dist, 0, None)
    max_exact = REL_BUCKETS // 2
    d = np.maximum(dc, 1).astype(np.float32)
    large = max_exact + (np.log(d / np.float32(max_exact)) / np.float32(math.log(REL_MAX_DIST / max_exact))
                         * np.float32(REL_BUCKETS - max_exact)).astype(np.int32)
    large = np.minimum(large, REL_BUCKETS - 1)
    bucket = np.where(dc < max_exact, dc, large).astype(np.int32)
    return np.where(in_win, bucket, -1).astype(np.int32)


def _swa_kernel(q_ref, kp_ref, kc_ref, vp_ref, vc_ref, bucket_ref, relb_ref, sink_ref,
                o_ref, bias_ref):
    n = pl.program_id(0)

    @pl.when(n == 0)
    def _():
        bucket = bucket_ref[...]
        for h in range(SWA_Q_HEADS):
            b = jnp.full(bucket.shape, NEG_INF, F32)
            for r in range(REL_BUCKETS):
                b = jnp.where(bucket == r, relb_ref[r, h], b)
            bias_ref[h] = b

    prev_ok = n > 0
    col = lax.broadcasted_iota(jnp.int32, (BLOCK, 2 * BLOCK), 1)
    key_ok = jnp.logical_or(prev_ok, col >= BLOCK)

    for g in range(SWA_KV_HEADS):
        ks = slice(g * HEAD_DIM, (g + 1) * HEAD_DIM)
        k2 = jnp.concatenate([kp_ref[:, ks], kc_ref[:, ks]], axis=0)
        v2 = jnp.concatenate([vp_ref[:, ks], vc_ref[:, ks]], axis=0)
        for gh in range(SWA_GROUP):
            h = g * SWA_GROUP + gh
            qh = q_ref[:, h * HEAD_DIM:(h + 1) * HEAD_DIM]
            s = lax.dot_general(qh, k2, (((1,), (1,)), ((), ())), preferred_element_type=F32)
            logits = jnp.where(key_ok, s + bias_ref[h], NEG_INF)
            sink = sink_ref[h]
            m = jnp.maximum(jnp.max(logits, axis=-1, keepdims=True), sink)
            p = jnp.exp(logits - m)
            denom = jnp.sum(p, axis=-1, keepdims=True) + jnp.exp(sink - m)
            w = (p / denom).astype(BF16)
            o_ref[:, h * HEAD_DIM:(h + 1) * HEAD_DIM] = jnp.dot(w, v2, preferred_element_type=F32)


def _swa(proj, rel_bias, sinks):
    S = proj.shape[0]
    N = S // BLOCK
    bucket = jnp.asarray(_rel_bucket_table())
    prev = lambda n: jnp.maximum(n - 1, 0)
    return pl.pallas_call(
        _swa_kernel,
        out_shape=jax.ShapeDtypeStruct((S, SWA_Q_W), F32),
        grid=(N,),
        in_specs=[
            pl.BlockSpec((BLOCK, SWA_Q_W), lambda n: (n, 0)),
            pl.BlockSpec((BLOCK, LANES), lambda n: (prev(n), COL_KA)),
            pl.BlockSpec((BLOCK, LANES), lambda n: (n, COL_KA)),
            pl.BlockSpec((BLOCK, LANES), lambda n: (prev(n), COL_VA)),
            pl.BlockSpec((BLOCK, LANES), lambda n: (n, COL_VA)),
            pl.BlockSpec((BLOCK, 2 * BLOCK), lambda n: (0, 0)),
            pl.BlockSpec(memory_space=pltpu.SMEM),
            pl.BlockSpec(memory_space=pltpu.SMEM),
        ],
        out_specs=pl.BlockSpec((BLOCK, SWA_Q_W), lambda n: (n, 0)),
        scratch_shapes=[pltpu.VMEM((SWA_Q_HEADS, BLOCK, 2 * BLOCK), F32)],
        compiler_params=_params(("arbitrary",)),
        name="swa",
    )(proj, proj, proj, proj, proj, bucket, rel_bias, sinks)


def _cumsum_weights():
    kk = np.arange(BLOCK)
    upper = (kk[:, None] > kk[None, :]).astype(np.float32)
    w = np.concatenate([upper, np.ones((BLOCK, BLOCK), np.float32)], axis=1)
    return np.concatenate([w, w], axis=0)


def _sb_kernel(q_ref, k_ref, v_ref, w2_ref, o_ref, acc_ref, oacc_ref):
    n = pl.program_id(1)
    lane = lax.broadcasted_iota(jnp.int32, (BLOCK, LANES), 1)
    first = lane < HEAD_DIM
    q = q_ref[...]
    zq = jnp.zeros_like(q)
    qq = jnp.concatenate([jnp.where(first, q, zq), jnp.where(first, zq, q)], axis=0)
    w2 = w2_ref[...]

    def chunk(start, nblk, head):
        kb = k_ref[pl.ds(start, nblk * BLOCK), :]
        vb = v_ref[pl.ds(start, nblk * BLOCK), :]
        z = lax.dot_general(qq, kb, (((1,), (1,)), ((), ())), preferred_element_type=F32)
        lg = jnp.log(1.0 + jnp.exp(-jnp.abs(z)))
        logsig = jnp.minimum(z, 0.0) - lg
        log1m = logsig - z
        if head:
            qpos = n * BLOCK + lax.broadcasted_iota(jnp.int32, z.shape, 0) % BLOCK
            kpos = start + lax.broadcasted_iota(jnp.int32, z.shape, 1)
            causal = kpos < qpos
            log1m = jnp.where(causal, log1m, 0.0)
        running = None if head else acc_ref[...]
        parts = [None] * nblk
        for c in reversed(range(nblk)):
            cs = slice(c * BLOCK, (c + 1) * BLOCK)
            l1 = log1m[:, cs]
            hi = l1.astype(BF16)
            lo = (l1 - hi.astype(F32)).astype(BF16)
            r = jnp.dot(jnp.concatenate([hi, lo], axis=1), w2, preferred_element_type=F32)
            log_a = logsig[:, cs] + r[:, :BLOCK]
            if running is not None:
                log_a = log_a + running
            a = jnp.exp(log_a)
            if head:
                a = jnp.where(causal[:, cs], a, 0.0)
            parts[c] = a.astype(BF16)
            running = r[:, BLOCK:] if running is None else running + r[:, BLOCK:]
        acc_ref[...] = running
        amat = parts[0] if nblk == 1 else jnp.concatenate(parts, axis=1)
        pv = jnp.dot(amat, vb, preferred_element_type=F32)
        if head:
            oacc_ref[...] = pv
        else:
            oacc_ref[...] += pv

    start0 = pl.multiple_of(jnp.maximum(n - 1, 0) * BLOCK, BLOCK)
    chunk(start0, 2, True)

    def live():
        return (jnp.max(acc_ref[...]) > PRUNE_LOG).astype(jnp.int32)

    def cond(carry):
        j, go = carry
        return jnp.logical_and(j >= 0, go > 0)

    def body(carry):
        j, _ = carry
        chunk(pl.multiple_of(j * BLOCK, BLOCK), 1, False)
        return j - 1, live()

    lax.while_loop(cond, body, (n - 2, live()))
    o_ref[...] = jnp.where(first, oacc_ref[:BLOCK, :], oacc_ref[BLOCK:, :])


def _sb(proj):
    S = proj.shape[0]
    N = S // BLOCK
    pairs = SB_W // LANES
    w2 = jnp.asarray(_cumsum_weights(), dtype=BF16)
    return pl.pallas_call(
        _sb_kernel,
        out_shape=jax.ShapeDtypeStruct((S, SB_W), F32),
        grid=(pairs, N),
        in_specs=[
            pl.BlockSpec((BLOCK, LANES), lambda p, n: (n, COL_QS + p)),
            pl.BlockSpec((S, LANES), lambda p, n: (0, COL_KS + p)),
            pl.BlockSpec((S, LANES), lambda p, n: (0, COL_VS + p)),
            pl.BlockSpec((2 * BLOCK, 2 * BLOCK), lambda p, n: (0, 0)),
        ],
        out_specs=pl.BlockSpec((BLOCK, LANES), lambda p, n: (n, p)),
        scratch_shapes=[pltpu.VMEM((2 * BLOCK, BLOCK), F32), pltpu.VMEM((2 * BLOCK, LANES), F32)],
        compiler_params=_params(("arbitrary", "arbitrary")),
        name="stickbreak",
    )(proj, proj, proj, w2)


def _rms(y):
    return y * lax.rsqrt(jnp.mean(y * y, axis=-1, keepdims=True) + EPS)


def _outproj_kernel(ya_ref, yb_ref, x_ref, ga_ref, gb_ref, gm_ref, w_ref, h_ref, hn_ref):
    ma = (_rms(ya_ref[...]) * ga_ref[...]).astype(BF16)
    mb = (_rms(yb_ref[...]) * gb_ref[...]).astype(BF16)
    mix = jnp.concatenate([ma, mb], axis=-1)
    h = x_ref[...] + jnp.dot(mix, w_ref[...], preferred_element_type=F32)
    h_ref[...] = h
    hn_ref[...] = (_rms(h) * gm_ref[...]).astype(BF16)


def _outproj(ya, yb, x, ga, gb, gm, w_bf16, tm=256):
    S, D = x.shape
    row = lambda i: (i, 0)
    const = lambda i: (0, 0)
    return pl.pallas_call(
        _outproj_kernel,
        out_shape=(jax.ShapeDtypeStruct((S, D), F32), jax.ShapeDtypeStruct((S, D), BF16)),
        grid=(S // tm,),
        in_specs=[
            pl.BlockSpec((tm, SWA_Q_W), row),
            pl.BlockSpec((tm, SB_W), row),
            pl.BlockSpec((tm, D), row),
            pl.BlockSpec((1, SWA_Q_W), const),
            pl.BlockSpec((1, SB_W), const),
            pl.BlockSpec((1, D), const),
            pl.BlockSpec((D_MIX, D), const),
        ],
        out_specs=(pl.BlockSpec((tm, D), row), pl.BlockSpec((tm, D), row)),
        compiler_params=_params(("arbitrary",)),
        name="outproj",
    )(ya, yb, x, ga, gb, gm, w_bf16)


def _convglu_kernel(hn_ref, halo_ref, wg_ref, wv_ref, wc_ref, bc_ref, wd_ref, h_ref, gf_ref,
                    o_ref, lhs_ref, gate_ref, acc_ref, *, tm):
    i = pl.program_id(0)
    f = pl.program_id(1)

    @pl.when(f == 0)
    def _():
        halo = halo_ref[...]
        lhs_ref[:HALO, :] = jnp.where(i > 0, halo, jnp.zeros_like(halo))
        lhs_ref[HALO:, :] = hn_ref[...]
        acc_ref[...] = jnp.zeros_like(acc_ref)

    gate_ref[...] = jnp.dot(lhs_ref[...], wg_ref[...], preferred_element_type=F32)
    val = jnp.dot(hn_ref[...], wv_ref[...], preferred_element_type=F32)
    gc = bc_ref[...]
    for tap in range(CONV_WIDTH):
        off = HALO - (CONV_WIDTH - 1) + tap
        gc = gc + gate_ref[pl.ds(off, tm), :] * wc_ref[tap:tap + 1, :]
    act = (gc * (1.0 / (1.0 + jnp.exp(-gc))) * val).astype(BF16)
    acc_ref[...] += jnp.dot(act, wd_ref[...], preferred_element_type=F32)

    @pl.when(f == pl.num_programs(1) - 1)
    def _():
        h2 = h_ref[...] + acc_ref[...]
        o_ref[...] = _rms(h2) * gf_ref[...]


def _convglu(hn2, h1, w_up_bf16, w_conv, b_conv, w_down_bf16, g_final, tm=512, tf=512):
    S, D = h1.shape
    nf = D_FF // tf
    halo_blocks = tm // HALO
    return pl.pallas_call(
        functools.partial(_convglu_kernel, tm=tm),
        out_shape=jax.ShapeDtypeStruct((S, D), F32),
        grid=(S // tm, nf),
        in_specs=[
            pl.BlockSpec((tm, D), lambda i, f: (i, 0)),
            pl.BlockSpec((HALO, D), lambda i, f: (jnp.maximum(i * halo_blocks - 1, 0), 0)),
            pl.BlockSpec((D, tf), lambda i, f: (0, f)),
            pl.BlockSpec((D, tf), lambda i, f: (0, nf + f)),
            pl.BlockSpec((CONV_WIDTH, tf), lambda i, f: (0, f)),
            pl.BlockSpec((1, tf), lambda i, f: (0, f)),
            pl.BlockSpec((tf, D), lambda i, f: (f, 0)),
            pl.BlockSpec((tm, D), lambda i, f: (i, 0)),
            pl.BlockSpec((1, D), lambda i, f: (0, 0)),
        ],
        out_specs=pl.BlockSpec((tm, D), lambda i, f: (i, 0)),
        scratch_shapes=[
            pltpu.VMEM((tm + HALO, D), BF16),
            pltpu.VMEM((tm + HALO, tf), F32),
            pltpu.VMEM((tm, D), F32),
        ],
        compiler_params=_params(("arbitrary", "arbitrary")),
        name="convglu",
    )(hn2, hn2, w_up_bf16, w_up_bf16, w_conv, b_conv, w_down_bf16, h1, g_final)


def kernel(x, w_in, g_attn_norm, rel_bias, swa_sinks, g_swa_out, g_sb_out, w_out,
           g_mlp_norm, w_up, w_conv, b_conv, w_down, g_final):
    B, S, D = x.shape
    assert (B, S, D) == (1, SEQ, D_MODEL)
    x2 = x.reshape(S, D)

    col = np.ones((1, D_IN), np.float32)
    col[:, :SWA_Q_W] = SCALE
    col[:, COL_QS * LANES:COL_KS * LANES] = SCALE
    w_in_b = (w_in * jnp.asarray(col)).astype(BF16)
    w_out_b = w_out.astype(BF16)
    w_up_b = w_up.astype(BF16)
    w_down_b = w_down.astype(BF16)

    proj = _inproj(x2, g_attn_norm.reshape(1, D), w_in_b)
    ya = _swa(proj, rel_bias, swa_sinks)
    yb = _sb(proj)
    h1, hn2 = _outproj(ya, yb, x2, g_swa_out.reshape(1, -1), g_sb_out.reshape(1, -1),
                       g_mlp_norm.reshape(1, D), w_out_b)
    out = _convglu(hn2, h1, w_up_b, w_conv, b_conv.reshape(1, -1), w_down_b, g_final.reshape(1, D))
    return out.reshape(B, S, D)
```

```python
import functools
import math

import numpy as np
import jax
import jax.numpy as jnp
from jax import lax
from jax.experimental import pallas as pl
from jax.experimental.pallas import tpu as pltpu

D_MODEL = 2048
SEQ = 16384
HEAD_DIM = 64
SWA_Q_HEADS = 16
SWA_KV_HEADS = 2
SWA_GROUP = SWA_Q_HEADS // SWA_KV_HEADS
SB_HEADS = 16
WINDOW = 128
BLOCK = 128
REL_BUCKETS = 32
REL_MAX_DIST = 128
D_FF = 5632
CONV_WIDTH = 3
EPS = 1e-6
NEG_INF = -1e30

SWA_Q_W = SWA_Q_HEADS * HEAD_DIM
SWA_KV_W = SWA_KV_HEADS * HEAD_DIM
SB_W = SB_HEADS * HEAD_DIM
D_MIX = SWA_Q_W + SB_W
D_IN = SWA_Q_W + 2 * SWA_KV_W + 3 * SB_W

LANES = 128
REF_SPLITS = np.cumsum([0, SWA_Q_W, SWA_KV_W, SWA_KV_W, SB_W, SB_W, SB_W])
PERM = np.concatenate([np.arange(REF_SPLITS[i], REF_SPLITS[i + 1]) for i in (3, 4, 5, 0, 1, 2)])
WIDE_QS, WIDE_KS, WIDE_VS, WIDE_QA = 0, 1, 2, 3
COL_KS = SB_W
COL_VS = 2 * SB_W
COL_KA = (3 * SB_W + SWA_Q_W) // LANES
COL_VA = COL_KA + SWA_KV_W // LANES
PAIRS = SB_W // LANES

SCALE = HEAD_DIM ** -0.5
HALO = 16
VMEM_LIMIT = 56 * 1024 * 1024

F32 = jnp.float32
BF16 = jnp.bfloat16

PRUNE_LOG = -94.0


def _params(sem, vmem=VMEM_LIMIT):
    return pltpu.CompilerParams(dimension_semantics=sem, vmem_limit_bytes=vmem)


def _rms(y):
    return y * lax.rsqrt(jnp.mean(y * y, axis=-1, keepdims=True) + EPS)


def _inproj_kernel(x_ref, g_ref, w_ref, o_ref, hn_ref):
    @pl.when(pl.program_id(1) == 0)
    def _():
        hn_ref[...] = (_rms(x_ref[...]) * g_ref[...]).astype(BF16)

    o_ref[...] = jnp.dot(hn_ref[...], w_ref[...], preferred_element_type=F32).astype(BF16)


def _inproj(x, g, w_bf16, tm=512, tn=2176):
    S, D = x.shape
    N = w_bf16.shape[1]
    return pl.pallas_call(
        _inproj_kernel,
        out_shape=jax.ShapeDtypeStruct((S, N), BF16),
        grid=(S // tm, N // tn),
        in_specs=[
            pl.BlockSpec((tm, D), lambda i, j: (i, 0)),
            pl.BlockSpec((1, D), lambda i, j: (0, 0)),
            pl.BlockSpec((D, tn), lambda i, j: (0, j)),
        ],
        out_specs=pl.BlockSpec((tm, tn), lambda i, j: (i, j)),
        scratch_shapes=[pltpu.VMEM((tm, D), BF16)],
        compiler_params=_params(("arbitrary", "arbitrary")),
        name="inproj",
    )(x, g, w_bf16)


def _rel_bucket_table():
    qi = np.arange(BLOCK, dtype=np.int64)[:, None]
    kj = np.arange(2 * BLOCK, dtype=np.int64)[None, :]
    dist = qi + BLOCK - kj
    in_win = (dist >= 0) & (dist < WINDOW)
    dc = np.clip(dist, 0, None)
    max_exact = REL_BUCKETS // 2
    d = np.maximum(dc, 1).astype(np.float32)
    large = max_exact + (np.log(d / np.float32(max_exact)) / np.float32(math.log(REL_MAX_DIST / max_exact))
                         * np.float32(REL_BUCKETS - max_exact)).astype(np.int32)
    large = np.minimum(large, REL_BUCKETS - 1)
    bucket = np.where(dc < max_exact, dc, large).astype(np.int32)
    return np.where(in_win, bucket, -1).astype(np.int32)


def _swa_kernel(q_ref, kp_ref, kc_ref, vp_ref, vc_ref, bucket_ref, relb_ref, sink_ref, g_ref,
                o_ref, bias_ref, y_ref):
    n = pl.program_id(0)

    @pl.when(n == 0)
    def _():
        bucket = bucket_ref[...]
        for h in range(SWA_Q_HEADS):
            b = jnp.full(bucket.shape, NEG_INF, F32)
            for r in range(REL_BUCKETS):
                b = jnp.where(bucket == r, relb_ref[r, h], b)
            bias_ref[h] = b

    prev_ok = n > 0
    col = lax.broadcasted_iota(jnp.int32, (BLOCK, 2 * BLOCK), 1)
    key_ok = jnp.logical_or(prev_ok, col >= BLOCK)

    for g in range(SWA_KV_HEADS):
        ks = slice(g * HEAD_DIM, (g + 1) * HEAD_DIM)
        k2 = jnp.concatenate([kp_ref[:, ks], kc_ref[:, ks]], axis=0)
        v2 = jnp.concatenate([vp_ref[:, ks], vc_ref[:, ks]], axis=0)
        for gh in range(SWA_GROUP):
            h = g * SWA_GROUP + gh
            qh = q_ref[:, h * HEAD_DIM:(h + 1) * HEAD_DIM]
            s = lax.dot_general(qh, k2, (((1,), (1,)), ((), ())), preferred_element_type=F32)
            logits = jnp.where(key_ok, s + bias_ref[h], NEG_INF)
            sink = sink_ref[h]
            m = jnp.maximum(jnp.max(logits, axis=-1, keepdims=True), sink)
            p = jnp.exp(logits - m)
            denom = jnp.sum(p, axis=-1, keepdims=True) + jnp.exp(sink - m)
            w = (p / denom).astype(BF16)
            y_ref[:, h * HEAD_DIM:(h + 1) * HEAD_DIM] = jnp.dot(w, v2, preferred_element_type=F32)

    o_ref[...] = (_rms(y_ref[...]) * g_ref[...]).astype(BF16)


def _swa(proj, rel_bias, sinks, g):
    S = proj.shape[0]
    N = S // BLOCK
    bucket = jnp.asarray(_rel_bucket_table())
    prev = lambda n: jnp.maximum(n - 1, 0)
    return pl.pallas_call(
        _swa_kernel,
        out_shape=jax.ShapeDtypeStruct((S, SWA_Q_W), BF16),
        grid=(N,),
        in_specs=[
            pl.BlockSpec((BLOCK, SWA_Q_W), lambda n: (n, WIDE_QA)),
            pl.BlockSpec((BLOCK, LANES), lambda n: (prev(n), COL_KA)),
            pl.BlockSpec((BLOCK, LANES), lambda n: (n, COL_KA)),
            pl.BlockSpec((BLOCK, LANES), lambda n: (prev(n), COL_VA)),
            pl.BlockSpec((BLOCK, LANES), lambda n: (n, COL_VA)),
            pl.BlockSpec((BLOCK, 2 * BLOCK), lambda n: (0, 0)),
            pl.BlockSpec(memory_space=pltpu.SMEM),
            pl.BlockSpec(memory_space=pltpu.SMEM),
            pl.BlockSpec((1, SWA_Q_W), lambda n: (0, 0)),
        ],
        out_specs=pl.BlockSpec((BLOCK, SWA_Q_W), lambda n: (n, 0)),
        scratch_shapes=[pltpu.VMEM((SWA_Q_HEADS, BLOCK, 2 * BLOCK), F32),
                        pltpu.VMEM((BLOCK, SWA_Q_W), F32)],
        compiler_params=_params(("arbitrary",)),
        name="swa",
    )(proj, proj, proj, proj, proj, bucket, rel_bias, sinks, g)


def _cumsum_weights():
    kk = np.arange(BLOCK)
    upper = (kk[:, None] > kk[None, :]).astype(np.float32)
    w = np.concatenate([upper, np.ones((BLOCK, BLOCK), np.float32)], axis=1)
    return np.concatenate([w, w], axis=0)


def _sb_chunk(qq, kb, vb, w2, running, causal):
    nblk = kb.shape[0] // BLOCK
    z = lax.dot_general(qq, kb, (((1,), (1,)), ((), ())), preferred_element_type=F32)
    lg = jnp.log(1.0 + jnp.exp(-jnp.abs(z)))
    logsig = jnp.minimum(z, 0.0) - lg
    log1m = logsig - z
    if causal is not None:
        log1m = jnp.where(causal, log1m, 0.0)
    parts = [None] * nblk
    for c in reversed(range(nblk)):
        cs = slice(c * BLOCK, (c + 1) * BLOCK)
        l1 = log1m[:, cs]
        hi = l1.astype(BF16)
        lo = (l1 - hi.astype(F32)).astype(BF16)
        r = jnp.dot(jnp.concatenate([hi, lo], axis=1), w2, preferred_element_type=F32)
        log_a = logsig[:, cs] + r[:, :BLOCK]
        if running is not None:
            log_a = log_a + running
        a = jnp.exp(log_a)
        if causal is not None:
            a = jnp.where(causal[:, cs], a, 0.0)
        parts[c] = a.astype(BF16)
        running = r[:, BLOCK:] if running is None else running + r[:, BLOCK:]
    amat = parts[0] if nblk == 1 else jnp.concatenate(parts, axis=1)
    return running, jnp.dot(amat, vb, preferred_element_type=F32)


def _sb_kernel(q_ref, kp_ref, kc_ref, vp_ref, vc_ref, kp2_ref, vp2_ref, w2_ref, g_ref, proj_hbm,
               o_ref, qq_ref, kd_ref, vd_ref, acc_ref, oacc_ref, live_ref, sem):
    n = pl.program_id(0)
    lane = lax.broadcasted_iota(jnp.int32, (BLOCK, LANES), 1)
    first = lane < HEAD_DIM
    w2 = w2_ref[...]

    qrow = lax.broadcasted_iota(jnp.int32, (2 * BLOCK, 2 * BLOCK), 0) % BLOCK
    kcol = lax.broadcasted_iota(jnp.int32, (2 * BLOCK, 2 * BLOCK), 1)
    kpos = (n - 1) * BLOCK + kcol
    causal = jnp.logical_and(kpos < n * BLOCK + qrow, kpos >= 0)

    for p in range(PAIRS):
        cols = slice(p * LANES, (p + 1) * LANES)
        q = q_ref[:, cols]
        zq = jnp.zeros_like(q)
        qq = jnp.concatenate([jnp.where(first, q, zq), jnp.where(first, zq, q)], axis=0)
        qq_ref[p] = qq
        kd_ref[p] = kp2_ref[:, cols]
        vd_ref[p] = vp2_ref[:, cols]
        kb = jnp.concatenate([kp_ref[:, cols], kc_ref[:, cols]], axis=0)
        vb = jnp.concatenate([vp_ref[:, cols], vc_ref[:, cols]], axis=0)
        acc, pv = _sb_chunk(qq, kb, vb, w2, None, causal)
        acc_ref[p] = acc
        oacc_ref[p] = pv
        live_ref[p] = (jnp.max(acc) > PRUNE_LOG).astype(jnp.int32)

    def pair_body(p, carry):
        def live():
            return (jnp.max(acc_ref[p]) > PRUNE_LOG).astype(jnp.int32)

        def cond(c):
            j, go = c
            return jnp.logical_and(j >= 0, go > 0)

        def fetch(j, dst, col0, slot):
            src = proj_hbm.at[pl.ds(pl.multiple_of(j * BLOCK, BLOCK), BLOCK),
                              pl.ds(pl.multiple_of(col0 + p * LANES, LANES), LANES)]
            return pltpu.make_async_copy(src, dst.at[p], sem.at[slot])

        def body(c):
            j, _ = c

            @pl.when(j < n - 2)
            def _():
                ck = fetch(j, kd_ref, COL_KS, 0)
                cv = fetch(j, vd_ref, COL_VS, 1)
                ck.start()
                cv.start()
                ck.wait()
                cv.wait()

            acc, pv = _sb_chunk(qq_ref[p], kd_ref[p], vd_ref[p], w2, acc_ref[p], None)
            acc_ref[p] = acc
            oacc_ref[p] += pv
            return j - 1, live()

        lax.while_loop(cond, body, (n - 2, live_ref[p]))
        return carry

    lax.fori_loop(0, PAIRS, pair_body, 0)

    ys = []
    ss = jnp.zeros((BLOCK, 1), F32)
    for p in range(PAIRS):
        y = jnp.where(first, oacc_ref[p, :BLOCK, :], oacc_ref[p, BLOCK:, :])
        ss = ss + jnp.sum(y * y, axis=-1, keepdims=True)
        ys.append(y)
    inv = lax.rsqrt(ss * (1.0 / SB_W) + EPS)
    for p in range(PAIRS):
        cols = slice(p * LANES, (p + 1) * LANES)
        o_ref[:, cols] = (ys[p] * inv * g_ref[:, cols]).astype(BF16)


def _sb(proj, g):
    S = proj.shape[0]
    N = S // BLOCK
    w2 = jnp.asarray(_cumsum_weights(), dtype=BF16)
    back = lambda d: (lambda n: jnp.maximum(n - d, 0))
    wide = lambda rowf, c: pl.BlockSpec((BLOCK, SB_W), lambda n: (rowf(n), c))
    return pl.pallas_call(
        _sb_kernel,
        out_shape=jax.ShapeDtypeStruct((S, SB_W), BF16),
        grid=(N,),
        in_specs=[
            wide(back(0), WIDE_QS),
            wide(back(1), WIDE_KS), wide(back(0), WIDE_KS),
            wide(back(1), WIDE_VS), wide(back(0), WIDE_VS),
            wide(back(2), WIDE_KS), wide(back(2), WIDE_VS),
            pl.BlockSpec((2 * BLOCK, 2 * BLOCK), lambda n: (0, 0)),
            pl.BlockSpec((1, SB_W), lambda n: (0, 0)),
            pl.BlockSpec(memory_space=pl.ANY),
        ],
        out_specs=pl.BlockSpec((BLOCK, SB_W), lambda n: (n, 0)),
        scratch_shapes=[
            pltpu.VMEM((PAIRS, 2 * BLOCK, LANES), BF16),
            pltpu.VMEM((PAIRS, BLOCK, LANES), BF16),
            pltpu.VMEM((PAIRS, BLOCK, LANES), BF16),
            pltpu.VMEM((PAIRS, 2 * BLOCK, BLOCK), F32),
            pltpu.VMEM((PAIRS, 2 * BLOCK, LANES), F32),
            pltpu.SMEM((PAIRS,), jnp.int32),
            pltpu.SemaphoreType.DMA((2,)),
        ],
        compiler_params=_params(("arbitrary",)),
        name="stickbreak",
    )(proj, proj, proj, proj, proj, proj, proj, w2, g, proj)


def _outproj_kernel(ma_ref, mb_ref, x_ref, gm_ref, w_ref, h_ref, hn_ref):
    mix = jnp.concatenate([ma_ref[...], mb_ref[...]], axis=-1)
    h = x_ref[...] + jnp.dot(mix, w_ref[...], preferred_element_type=F32)
    h_ref[...] = h
    hn_ref[...] = (_rms(h) * gm_ref[...]).astype(BF16)


def _outproj(ma, mb, x, gm, w_bf16, tm=256):
    S, D = x.shape
    row = lambda i: (i, 0)
    const = lambda i: (0, 0)
    return pl.pallas_call(
        _outproj_kernel,
        out_shape=(jax.ShapeDtypeStruct((S, D), F32), jax.ShapeDtypeStruct((S, D), BF16)),
        grid=(S // tm,),
        in_specs=[
            pl.BlockSpec((tm, SWA_Q_W), row),
            pl.BlockSpec((tm, SB_W), row),
            pl.BlockSpec((tm, D), row),
            pl.BlockSpec((1, D), const),
            pl.BlockSpec((D_MIX, D), const),
        ],
        out_specs=(pl.BlockSpec((tm, D), row), pl.BlockSpec((tm, D), row)),
        compiler_params=_params(("arbitrary",)),
        name="outproj",
    )(ma, mb, x, gm, w_bf16)


def _convglu_kernel(hn_ref, halo_ref, wg_ref, wv_ref, wc_ref, bc_ref, wd_ref, h_ref, gf_ref,
                    o_ref, lhs_ref, gate_ref, acc_ref, *, tm):
    i = pl.program_id(0)
    f = pl.program_id(1)

    @pl.when(f == 0)
    def _():
        halo = halo_ref[...]
        lhs_ref[:HALO, :] = jnp.where(i > 0, halo, jnp.zeros_like(halo))
        lhs_ref[HALO:, :] = hn_ref[...]
        acc_ref[...] = jnp.zeros_like(acc_ref)

    gate_ref[...] = jnp.dot(lhs_ref[...], wg_ref[...], preferred_element_type=F32)
    val = jnp.dot(hn_ref[...], wv_ref[...], preferred_element_type=F32)
    gc = bc_ref[...]
    for tap in range(CONV_WIDTH):
        off = HALO - (CONV_WIDTH - 1) + tap
        gc = gc + gate_ref[pl.ds(off, tm), :] * wc_ref[tap:tap + 1, :]
    act = (gc * (1.0 / (1.0 + jnp.exp(-gc))) * val).astype(BF16)
    acc_ref[...] += jnp.dot(act, wd_ref[...], preferred_element_type=F32)

    @pl.when(f == pl.num_programs(1) - 1)
    def _():
        h2 = h_ref[...] + acc_ref[...]
        o_ref[...] = _rms(h2) * gf_ref[...]


def _convglu(hn2, h1, w_up_bf16, w_conv, b_conv, w_down_bf16, g_final, tm=512, tf=512):
    S, D = h1.shape
    nf = D_FF // tf
    halo_blocks = tm // HALO
    return pl.pallas_call(
        functools.partial(_convglu_kernel, tm=tm),
        out_shape=jax.ShapeDtypeStruct((S, D), F32),
        grid=(S // tm, nf),
        in_specs=[
            pl.BlockSpec((tm, D), lambda i, f: (i, 0)),
            pl.BlockSpec((HALO, D), lambda i, f: (jnp.maximum(i * halo_blocks - 1, 0), 0)),
            pl.BlockSpec((D, tf), lambda i, f: (0, f)),
            pl.BlockSpec((D, tf), lambda i, f: (0, nf + f)),
            pl.BlockSpec((CONV_WIDTH, tf), lambda i, f: (0, f)),
            pl.BlockSpec((1, tf), lambda i, f: (0, f)),
            pl.BlockSpec((tf, D), lambda i, f: (f, 0)),
            pl.BlockSpec((tm, D), lambda i, f: (i, 0)),
            pl.BlockSpec((1, D), lambda i, f: (0, 0)),
        ],
        out_specs=pl.BlockSpec((tm, D), lambda i, f: (i, 0)),
        scratch_shapes=[
            pltpu.VMEM((tm + HALO, D), BF16),
            pltpu.VMEM((tm + HALO, tf), F32),
            pltpu.VMEM((tm, D), F32),
        ],
        compiler_params=_params(("arbitrary", "arbitrary")),
        name="convglu",
    )(hn2, hn2, w_up_bf16, w_up_bf16, w_conv, b_conv, w_down_bf16, h1, g_final)


def kernel(x, w_in, g_attn_norm, rel_bias, swa_sinks, g_swa_out, g_sb_out, w_out,
           g_mlp_norm, w_up, w_conv, b_conv, w_down, g_final):
    B, S, D = x.shape
    assert (B, S, D) == (1, SEQ, D_MODEL)
    x2 = x.reshape(S, D)

    col = np.ones((1, D_IN), np.float32)
    col[:, REF_SPLITS[0]:REF_SPLITS[1]] = SCALE
    col[:, REF_SPLITS[3]:REF_SPLITS[4]] = SCALE
    w_in_b = (w_in * jnp.asarray(col)).astype(BF16)[:, PERM]
    w_out_b = w_out.astype(BF16)
    w_up_b = w_up.astype(BF16)
    w_down_b = w_down.astype(BF16)

    proj = _inproj(x2, g_attn_norm.reshape(1, D), w_in_b)
    mix_a = _swa(proj, rel_bias, swa_sinks, g_swa_out.reshape(1, -1))
    mix_b = _sb(proj, g_sb_out.reshape(1, -1))
    h1, hn2 = _outproj(mix_a, mix_b, x2, g_mlp_norm.reshape(1, D), w_out_b)
    out = _convglu(hn2, h1, w_up_b, w_conv, b_conv.reshape(1, -1), w_down_b, g_final.reshape(1, D))
    return out.reshape(B, S, D)
```

```python
import functools
import math

import numpy as np
import jax
import jax.numpy as jnp
from jax import lax
from jax.experimental import pallas as pl
from jax.experimental.pallas import tpu as pltpu

D_MODEL = 2048
SEQ = 16384
HEAD_DIM = 64
SWA_Q_HEADS = 16
SWA_KV_HEADS = 2
SWA_GROUP = SWA_Q_HEADS // SWA_KV_HEADS
SB_HEADS = 16
WINDOW = 128
BLOCK = 128
REL_BUCKETS = 32
REL_MAX_DIST = 128
D_FF = 5632
CONV_WIDTH = 3
EPS = 1e-6
NEG_INF = -1e30

SWA_Q_W = SWA_Q_HEADS * HEAD_DIM
SWA_KV_W = SWA_KV_HEADS * HEAD_DIM
SB_W = SB_HEADS * HEAD_DIM
D_MIX = SWA_Q_W + SB_W
D_IN = SWA_Q_W + 2 * SWA_KV_W + 3 * SB_W

LANES = 128
REF_SPLITS = np.cumsum([0, SWA_Q_W, SWA_KV_W, SWA_KV_W, SB_W, SB_W, SB_W])
PERM_GROUPS = (3, 4, 5, 0, 1, 2)
WIDE_QS, WIDE_KS, WIDE_VS, WIDE_QA = 0, 1, 2, 3
COL_KS = SB_W
COL_VS = 2 * SB_W
COL_KA = (3 * SB_W + SWA_Q_W) // LANES
COL_VA = COL_KA + SWA_KV_W // LANES
PAIRS = SB_W // LANES

SCALE = HEAD_DIM ** -0.5
HALO = 16
VMEM_LIMIT = 56 * 1024 * 1024

F32 = jnp.float32
BF16 = jnp.bfloat16

PRUNE_LOG = -88.0


def _params(sem, vmem=VMEM_LIMIT):
    return pltpu.CompilerParams(dimension_semantics=sem, vmem_limit_bytes=vmem)


def _rms(y):
    return y * lax.rsqrt(jnp.mean(y * y, axis=-1, keepdims=True) + EPS)


def _inproj_kernel(x_ref, g_ref, w_ref, o_ref, hn_ref):
    @pl.when(pl.program_id(1) == 0)
    def _():
        hn_ref[...] = (_rms(x_ref[...]) * g_ref[...]).astype(BF16)

    o_ref[...] = jnp.dot(hn_ref[...], w_ref[...], preferred_element_type=F32).astype(BF16)


def _inproj(x, g, w_bf16, tm=512, tn=2176):
    S, D = x.shape
    N = w_bf16.shape[1]
    return pl.pallas_call(
        _inproj_kernel,
        out_shape=jax.ShapeDtypeStruct((S, N), BF16),
        grid=(S // tm, N // tn),
        in_specs=[
            pl.BlockSpec((tm, D), lambda i, j: (i, 0)),
            pl.BlockSpec((1, D), lambda i, j: (0, 0)),
            pl.BlockSpec((D, tn), lambda i, j: (0, j)),
        ],
        out_specs=pl.BlockSpec((tm, tn), lambda i, j: (i, j)),
        scratch_shapes=[pltpu.VMEM((tm, D), BF16)],
        compiler_params=_params(("arbitrary", "arbitrary")),
        name="inproj",
    )(x, g, w_bf16)


def _rel_bucket_table():
    qi = np.arange(BLOCK, dtype=np.int64)[:, None]
    kj = np.arange(2 * BLOCK, dtype=np.int64)[None, :]
    dist = qi + BLOCK - kj
    in_win = (dist >= 0) & (dist < WINDOW)
    dc = np.clip(dist, 0, None)
    max_exact = REL_BUCKETS // 2
    d = np.maximum(dc, 1).astype(np.float32)
    large = max_exact + (np.log(d / np.float32(max_exact)) / np.float32(math.log(REL_MAX_DIST / max_exact))
                         * np.float32(REL_BUCKETS - max_exact)).astype(np.int32)
    large = np.minimum(large, REL_BUCKETS - 1)
    bucket = np.where(dc < max_exact, dc, large).astype(np.int32)
    return np.where(in_win, bucket, -1).astype(np.int32)


def _swa_kernel(q_ref, kp_ref, kc_ref, vp_ref, vc_ref, bucket_ref, relb_ref, sink_ref, g_ref,
                o_ref, bias_ref, y_ref):
    n = pl.program_id(0)

    @pl.when(n == 0)
    def _():
        bucket = bucket_ref[...]
        col = lax.broadcasted_iota(jnp.int32, bucket.shape, 1)
        for h in range(SWA_Q_HEADS):
            b = jnp.full(bucket.shape, NEG_INF, F32)
            for r in range(REL_BUCKETS):
                b = jnp.where(bucket == r, relb_ref[r, h], b)
            bias_ref[0, h] = b
            bias_ref[1, h] = jnp.where(col >= BLOCK, b, NEG_INF)

    first_block = (n == 0).astype(jnp.int32)

    for g in range(SWA_KV_HEADS):
        ks = slice(g * HEAD_DIM, (g + 1) * HEAD_DIM)
        k2 = jnp.concatenate([kp_ref[:, ks], kc_ref[:, ks]], axis=0)
        v2 = jnp.concatenate([vp_ref[:, ks], vc_ref[:, ks]], axis=0)
        for gh in range(SWA_GROUP):
            h = g * SWA_GROUP + gh
            qh = q_ref[:, h * HEAD_DIM:(h + 1) * HEAD_DIM]
            s = lax.dot_general(qh, k2, (((1,), (1,)), ((), ())), preferred_element_type=F32)
            logits = s + bias_ref[first_block, h]
            sink = sink_ref[h]
            m = jnp.maximum(jnp.max(logits, axis=-1, keepdims=True), sink)
            p = jnp.exp(logits - m)
            denom = jnp.sum(p, axis=-1, keepdims=True) + jnp.exp(sink - m)
            w = (p * (1.0 / denom)).astype(BF16)
            y_ref[:, h * HEAD_DIM:(h + 1) * HEAD_DIM] = jnp.dot(w, v2, preferred_element_type=F32)

    o_ref[...] = (_rms(y_ref[...]) * g_ref[...]).astype(BF16)


def _swa(proj, rel_bias, sinks, g):
    S = proj.shape[0]
    N = S // BLOCK
    bucket = jnp.asarray(_rel_bucket_table())
    prev = lambda n: jnp.maximum(n - 1, 0)
    return pl.pallas_call(
        _swa_kernel,
        out_shape=jax.ShapeDtypeStruct((S, SWA_Q_W), BF16),
        grid=(N,),
        in_specs=[
            pl.BlockSpec((BLOCK, SWA_Q_W), lambda n: (n, WIDE_QA)),
            pl.BlockSpec((BLOCK, LANES), lambda n: (prev(n), COL_KA)),
            pl.BlockSpec((BLOCK, LANES), lambda n: (n, COL_KA)),
            pl.BlockSpec((BLOCK, LANES), lambda n: (prev(n), COL_VA)),
            pl.BlockSpec((BLOCK, LANES), lambda n: (n, COL_VA)),
            pl.BlockSpec((BLOCK, 2 * BLOCK), lambda n: (0, 0)),
            pl.BlockSpec(memory_space=pltpu.SMEM),
            pl.BlockSpec(memory_space=pltpu.SMEM),
            pl.BlockSpec((1, SWA_Q_W), lambda n: (0, 0)),
        ],
        out_specs=pl.BlockSpec((BLOCK, SWA_Q_W), lambda n: (n, 0)),
        scratch_shapes=[pltpu.VMEM((2, SWA_Q_HEADS, BLOCK, 2 * BLOCK), F32),
                        pltpu.VMEM((BLOCK, SWA_Q_W), F32)],
        compiler_params=_params(("arbitrary",)),
        name="swa",
    )(proj, proj, proj, proj, proj, bucket, rel_bias, sinks, g)


def _cumsum_weights():
    kk = np.arange(BLOCK)
    upper = (kk[:, None] > kk[None, :]).astype(np.float32)
    w = np.concatenate([upper, np.ones((BLOCK, BLOCK), np.float32)], axis=1)
    return np.concatenate([w, w], axis=0)


def _sb_chunk(qq, kb, vb, w2, running, causal):
    nblk = kb.shape[0] // BLOCK
    z = lax.dot_general(qq, kb, (((1,), (1,)), ((), ())), preferred_element_type=F32)
    lg = jnp.log(1.0 + jnp.exp(-jnp.abs(z)))
    logsig = jnp.minimum(z, 0.0) - lg
    log1m = logsig - z
    if causal is not None:
        log1m = jnp.where(causal, log1m, 0.0)
    parts = [None] * nblk
    for c in reversed(range(nblk)):
        cs = slice(c * BLOCK, (c + 1) * BLOCK)
        l1 = log1m[:, cs]
        hi = l1.astype(BF16)
        lo = (l1 - hi.astype(F32)).astype(BF16)
        r = jnp.dot(jnp.concatenate([hi, lo], axis=1), w2, preferred_element_type=F32)
        log_a = logsig[:, cs] + r[:, :BLOCK]
        if running is not None:
            log_a = log_a + running
        a = jnp.exp(log_a)
        if causal is not None:
            a = jnp.where(causal[:, cs], a, 0.0)
        parts[c] = a.astype(BF16)
        running = r[:, BLOCK:] if running is None else running + r[:, BLOCK:]
    amat = parts[0] if nblk == 1 else jnp.concatenate(parts, axis=1)
    return running, jnp.dot(amat, vb, preferred_element_type=F32)


def _sb_kernel(q_ref, kp_ref, kc_ref, vp_ref, vc_ref, kp2_ref, vp2_ref, w2_ref, g_ref, proj_hbm,
               o_ref, qq_ref, kd_ref, vd_ref, acc_ref, oacc_ref, live_ref, sem):
    n = pl.program_id(0)
    lane = lax.broadcasted_iota(jnp.int32, (BLOCK, LANES), 1)
    first = lane < HEAD_DIM
    w2 = w2_ref[...]

    qrow = lax.broadcasted_iota(jnp.int32, (2 * BLOCK, 2 * BLOCK), 0) % BLOCK
    kcol = lax.broadcasted_iota(jnp.int32, (2 * BLOCK, 2 * BLOCK), 1)
    kpos = (n - 1) * BLOCK + kcol
    causal = jnp.logical_and(kpos < n * BLOCK + qrow, kpos >= 0)

    for p in range(PAIRS):
        cols = slice(p * LANES, (p + 1) * LANES)
        q = q_ref[:, cols]
        zq = jnp.zeros_like(q)
        qq = jnp.concatenate([jnp.where(first, q, zq), jnp.where(first, zq, q)], axis=0)
        qq_ref[p] = qq
        kd_ref[p] = kp2_ref[:, cols]
        vd_ref[p] = vp2_ref[:, cols]
        kb = jnp.concatenate([kp_ref[:, cols], kc_ref[:, cols]], axis=0)
        vb = jnp.concatenate([vp_ref[:, cols], vc_ref[:, cols]], axis=0)
        acc, pv = _sb_chunk(qq, kb, vb, w2, None, causal)
        acc_ref[p] = acc
        oacc_ref[p] = pv
        live_ref[p] = (jnp.max(acc) > PRUNE_LOG).astype(jnp.int32)

    def pair_body(p, carry):
        def live():
            return (jnp.max(acc_ref[p]) > PRUNE_LOG).astype(jnp.int32)

        def cond(c):
            j, go = c
            return jnp.logical_and(j >= 0, go > 0)

        def fetch(j, dst, col0, slot):
            src = proj_hbm.at[pl.ds(pl.multiple_of(j * BLOCK, BLOCK), BLOCK),
                              pl.ds(pl.multiple_of(col0 + p * LANES, LANES), LANES)]
            return pltpu.make_async_copy(src, dst.at[p], sem.at[slot])

        def body(c):
            j, _ = c

            @pl.when(j < n - 2)
            def _():
                ck = fetch(j, kd_ref, COL_KS, 0)
                cv = fetch(j, vd_ref, COL_VS, 1)
                ck.start()
                cv.start()
                ck.wait()
                cv.wait()

            acc, pv = _sb_chunk(qq_ref[p], kd_ref[p], vd_ref[p], w2, acc_ref[p], None)
            acc_ref[p] = acc
            oacc_ref[p] += pv
            return j - 1, live()

        lax.while_loop(cond, body, (n - 2, live_ref[p]))
        return carry

    lax.fori_loop(0, PAIRS, pair_body, 0)

    ys = []
    ss = jnp.zeros((BLOCK, 1), F32)
    for p in range(PAIRS):
        y = jnp.where(first, oacc_ref[p, :BLOCK, :], oacc_ref[p, BLOCK:, :])
        ss = ss + jnp.sum(y * y, axis=-1, keepdims=True)
        ys.append(y)
    inv = lax.rsqrt(ss * (1.0 / SB_W) + EPS)
    for p in range(PAIRS):
        cols = slice(p * LANES, (p + 1) * LANES)
        o_ref[:, cols] = (ys[p] * inv * g_ref[:, cols]).astype(BF16)


def _sb(proj, g):
    S = proj.shape[0]
    N = S // BLOCK
    w2 = jnp.asarray(_cumsum_weights(), dtype=BF16)
    back = lambda d: (lambda n: jnp.maximum(n - d, 0))
    wide = lambda rowf, c: pl.BlockSpec((BLOCK, SB_W), lambda n: (rowf(n), c))
    return pl.pallas_call(
        _sb_kernel,
        out_shape=jax.ShapeDtypeStruct((S, SB_W), BF16),
        grid=(N,),
        in_specs=[
            wide(back(0), WIDE_QS),
            wide(back(1), WIDE_KS), wide(back(0), WIDE_KS),
            wide(back(1), WIDE_VS), wide(back(0), WIDE_VS),
            wide(back(2), WIDE_KS), wide(back(2), WIDE_VS),
            pl.BlockSpec((2 * BLOCK, 2 * BLOCK), lambda n: (0, 0)),
            pl.BlockSpec((1, SB_W), lambda n: (0, 0)),
            pl.BlockSpec(memory_space=pl.ANY),
        ],
        out_specs=pl.BlockSpec((BLOCK, SB_W), lambda n: (n, 0)),
        scratch_shapes=[
            pltpu.VMEM((PAIRS, 2 * BLOCK, LANES), BF16),
            pltpu.VMEM((PAIRS, BLOCK, LANES), BF16),
            pltpu.VMEM((PAIRS, BLOCK, LANES), BF16),
            pltpu.VMEM((PAIRS, 2 * BLOCK, BLOCK), F32),
            pltpu.VMEM((PAIRS, 2 * BLOCK, LANES), F32),
            pltpu.SMEM((PAIRS,), jnp.int32),
            pltpu.SemaphoreType.DMA((2,)),
        ],
        compiler_params=_params(("arbitrary",)),
        name="stickbreak",
    )(proj, proj, proj, proj, proj, proj, proj, w2, g, proj)


def _outproj_kernel(ma_ref, mb_ref, x_ref, gm_ref, w_ref, h_ref, hn_ref):
    mix = jnp.concatenate([ma_ref[...], mb_ref[...]], axis=-1)
    h = x_ref[...] + jnp.dot(mix, w_ref[...], preferred_element_type=F32)
    h_ref[...] = h
    hn_ref[...] = (_rms(h) * gm_ref[...]).astype(BF16)


def _outproj(ma, mb, x, gm, w_bf16, tm=256):
    S, D = x.shape
    row = lambda i: (i, 0)
    const = lambda i: (0, 0)
    return pl.pallas_call(
        _outproj_kernel,
        out_shape=(jax.ShapeDtypeStruct((S, D), F32), jax.ShapeDtypeStruct((S, D), BF16)),
        grid=(S // tm,),
        in_specs=[
            pl.BlockSpec((tm, SWA_Q_W), row),
            pl.BlockSpec((tm, SB_W), row),
            pl.BlockSpec((tm, D), row),
            pl.BlockSpec((1, D), const),
            pl.BlockSpec((D_MIX, D), const),
        ],
        out_specs=(pl.BlockSpec((tm, D), row), pl.BlockSpec((tm, D), row)),
        compiler_params=_params(("arbitrary",)),
        name="outproj",
    )(ma, mb, x, gm, w_bf16)


def _convglu_kernel(hn_ref, halo_ref, wg_ref, wv_ref, wc_ref, bc_ref, wd_ref, h_ref, gf_ref,
                    o_ref, lhs_ref, gate_ref, acc_ref, *, tm):
    i = pl.program_id(0)
    f = pl.program_id(1)

    @pl.when(f == 0)
    def _():
        halo = halo_ref[...]
        lhs_ref[:HALO, :] = jnp.where(i > 0, halo, jnp.zeros_like(halo))
        lhs_ref[HALO:, :] = hn_ref[...]
        acc_ref[...] = jnp.zeros_like(acc_ref)

    gate_ref[...] = jnp.dot(lhs_ref[...], wg_ref[...], preferred_element_type=F32)
    val = jnp.dot(hn_ref[...], wv_ref[...], preferred_element_type=F32)
    gc = bc_ref[...]
    for tap in range(CONV_WIDTH):
        off = HALO - (CONV_WIDTH - 1) + tap
        gc = gc + gate_ref[pl.ds(off, tm), :] * wc_ref[tap:tap + 1, :]
    act = (gc * (1.0 / (1.0 + jnp.exp(-gc))) * val).astype(BF16)
    acc_ref[...] += jnp.dot(act, wd_ref[...], preferred_element_type=F32)

    @pl.when(f == pl.num_programs(1) - 1)
    def _():
        h2 = h_ref[...] + acc_ref[...]
        o_ref[...] = _rms(h2) * gf_ref[...]


def _convglu(hn2, h1, w_up_bf16, w_conv, b_conv, w_down_bf16, g_final, tm=512, tf=512):
    S, D = h1.shape
    nf = D_FF // tf
    halo_blocks = tm // HALO
    return pl.pallas_call(
        functools.partial(_convglu_kernel, tm=tm),
        out_shape=jax.ShapeDtypeStruct((S, D), F32),
        grid=(S // tm, nf),
        in_specs=[
            pl.BlockSpec((tm, D), lambda i, f: (i, 0)),
            pl.BlockSpec((HALO, D), lambda i, f: (jnp.maximum(i * halo_blocks - 1, 0), 0)),
            pl.BlockSpec((D, tf), lambda i, f: (0, f)),
            pl.BlockSpec((D, tf), lambda i, f: (0, nf + f)),
            pl.BlockSpec((CONV_WIDTH, tf), lambda i, f: (0, f)),
            pl.BlockSpec((1, tf), lambda i, f: (0, f)),
            pl.BlockSpec((tf, D), lambda i, f: (f, 0)),
            pl.BlockSpec((tm, D), lambda i, f: (i, 0)),
            pl.BlockSpec((1, D), lambda i, f: (0, 0)),
        ],
        out_specs=pl.BlockSpec((tm, D), lambda i, f: (i, 0)),
        scratch_shapes=[
            pltpu.VMEM((tm + HALO, D), BF16),
            pltpu.VMEM((tm + HALO, tf), F32),
            pltpu.VMEM((tm, D), F32),
        ],
        compiler_params=_params(("arbitrary", "arbitrary")),
        name="convglu",
    )(hn2, hn2, w_up_bf16, w_up_bf16, w_conv, b_conv, w_down_bf16, h1, g_final)


def kernel(x, w_in, g_attn_norm, rel_bias, swa_sinks, g_swa_out, g_sb_out, w_out,
           g_mlp_norm, w_up, w_conv, b_conv, w_down, g_final):
    B, S, D = x.shape
    assert (B, S, D) == (1, SEQ, D_MODEL)
    x2 = x.reshape(S, D)

    col = np.ones((1, D_IN), np.float32)
    col[:, REF_SPLITS[0]:REF_SPLITS[1]] = SCALE
    col[:, REF_SPLITS[3]:REF_SPLITS[4]] = SCALE
    w_in_s = (w_in * jnp.asarray(col)).astype(BF16)
    w_in_b = jnp.concatenate([w_in_s[:, REF_SPLITS[i]:REF_SPLITS[i + 1]] for i in PERM_GROUPS], axis=1)
    w_out_b = w_out.astype(BF16)
    w_up_b = w_up.astype(BF16)
    w_down_b = w_down.astype(BF16)

    proj = _inproj(x2, g_attn_norm.reshape(1, D), w_in_b)
    mix_a = _swa(proj, rel_bias, swa_sinks, g_swa_out.reshape(1, -1))
    mix_b = _sb(proj, g_sb_out.reshape(1, -1))
    h1, hn2 = _outproj(mix_a, mix_b, x2, g_mlp_norm.reshape(1, D), w_out_b)
    out = _convglu(hn2, h1, w_up_b, w_conv, b_conv.reshape(1, -1), w_down_b, g_final.reshape(1, D))
    return out.reshape(B, S, D)
```

```python
import functools
import math

import numpy as np
import jax
import jax.numpy as jnp
from jax import lax
from jax.experimental import pallas as pl
from jax.experimental.pallas import tpu as pltpu

D_MODEL = 2048
SEQ = 16384
HEAD_DIM = 64
SWA_Q_HEADS = 16
SWA_KV_HEADS = 2
SWA_GROUP = SWA_Q_HEADS // SWA_KV_HEADS
SB_HEADS = 16
WINDOW = 128
BLOCK = 128
REL_BUCKETS = 32
REL_MAX_DIST = 128
D_FF = 5632
CONV_WIDTH = 3
EPS = 1e-6
NEG_INF = -1e30

SWA_Q_W = SWA_Q_HEADS * HEAD_DIM
SWA_KV_W = SWA_KV_HEADS * HEAD_DIM
SB_W = SB_HEADS * HEAD_DIM
D_MIX = SWA_Q_W + SB_W
D_IN = SWA_Q_W + 2 * SWA_KV_W + 3 * SB_W

LANES = 128
REF_SPLITS = np.cumsum([0, SWA_Q_W, SWA_KV_W, SWA_KV_W, SB_W, SB_W, SB_W])
PERM_GROUPS = (3, 4, 5, 0, 1, 2)
WIDE_QS, WIDE_KS, WIDE_VS, WIDE_QA = 0, 1, 2, 3
COL_KS = SB_W
COL_VS = 2 * SB_W
COL_KA = (3 * SB_W + SWA_Q_W) // LANES
COL_VA = COL_KA + SWA_KV_W // LANES
PAIRS = SB_W // LANES

SCALE = HEAD_DIM ** -0.5
HALO = 16
VMEM_LIMIT = 56 * 1024 * 1024
CONVGLU_VMEM_LIMIT = 60 * 1024 * 1024

F32 = jnp.float32
BF16 = jnp.bfloat16

PRUNE_LOG = -88.0


def _params(sem, vmem=VMEM_LIMIT):
    return pltpu.CompilerParams(dimension_semantics=sem, vmem_limit_bytes=vmem)


def _rms(y):
    return y * lax.rsqrt(jnp.mean(y * y, axis=-1, keepdims=True) + EPS)


def _inproj_kernel(x_ref, g_ref, w_ref, o_ref):
    hn = (_rms(x_ref[...]) * g_ref[...]).astype(BF16)
    o_ref[...] = jnp.dot(hn, w_ref[...], preferred_element_type=F32).astype(BF16)


def _resident(shape):
    return pl.BlockSpec(shape, lambda *_: (0,) * len(shape), pipeline_mode=pl.Buffered(1))


def _inproj(x, g, w_bf16, tm=512):
    S, D = x.shape
    N = w_bf16.shape[1]
    return pl.pallas_call(
        _inproj_kernel,
        out_shape=jax.ShapeDtypeStruct((S, N), BF16),
        grid=(S // tm,),
        in_specs=[
            pl.BlockSpec((tm, D), lambda i: (i, 0)),
            _resident((1, D)),
            _resident((D, N)),
        ],
        out_specs=pl.BlockSpec((tm, N), lambda i: (i, 0)),
        compiler_params=_params(("arbitrary",)),
        name="inproj",
    )(x, g, w_bf16)


def _rel_bucket_table():
    qi = np.arange(BLOCK, dtype=np.int64)[:, None]
    kj = np.arange(2 * BLOCK, dtype=np.int64)[None, :]
    dist = qi + BLOCK - kj
    in_win = (dist >= 0) & (dist < WINDOW)
    dc = np.clip(dist, 0, None)
    max_exact = REL_BUCKETS // 2
    d = np.maximum(dc, 1).astype(np.float32)
    large = max_exact + (np.log(d / np.float32(max_exact)) / np.float32(math.log(REL_MAX_DIST / max_exact))
                         * np.float32(REL_BUCKETS - max_exact)).astype(np.int32)
    large = np.minimum(large, REL_BUCKETS - 1)
    bucket = np.where(dc < max_exact, dc, large).astype(np.int32)
    return np.where(in_win, bucket, -1).astype(np.int32)


def _swa_kernel(q_ref, kp_ref, kc_ref, vp_ref, vc_ref, bucket_ref, relb_ref, sink_ref, g_ref,
                o_ref, bias_ref, y_ref):
    n = pl.program_id(0)

    @pl.when(n == 0)
    def _():
        bucket = bucket_ref[...]
        col = lax.broadcasted_iota(jnp.int32, bucket.shape, 1)
        for h in range(SWA_Q_HEADS):
            b = jnp.full(bucket.shape, NEG_INF, F32)
            for r in range(REL_BUCKETS):
                b = jnp.where(bucket == r, relb_ref[r, h], b)
            bias_ref[0, h] = b
            bias_ref[1, h] = jnp.where(col >= BLOCK, b, NEG_INF)

    first_block = (n == 0).astype(jnp.int32)

    for g in range(SWA_KV_HEADS):
        ks = slice(g * HEAD_DIM, (g + 1) * HEAD_DIM)
        k2 = jnp.concatenate([kp_ref[:, ks], kc_ref[:, ks]], axis=0)
        v2 = jnp.concatenate([vp_ref[:, ks], vc_ref[:, ks]], axis=0)
        for gh in range(SWA_GROUP):
            h = g * SWA_GROUP + gh
            qh = q_ref[:, h * HEAD_DIM:(h + 1) * HEAD_DIM]
            s = lax.dot_general(qh, k2, (((1,), (1,)), ((), ())), preferred_element_type=F32)
            logits = s + bias_ref[first_block, h]
            sink = sink_ref[h]
            m = jnp.maximum(jnp.max(logits, axis=-1, keepdims=True), sink)
            p = jnp.exp(logits - m)
            denom = jnp.sum(p, axis=-1, keepdims=True) + jnp.exp(sink - m)
            w = (p * (1.0 / denom)).astype(BF16)
            y_ref[:, h * HEAD_DIM:(h + 1) * HEAD_DIM] = jnp.dot(w, v2, preferred_element_type=F32)

    o_ref[...] = (_rms(y_ref[...]) * g_ref[...]).astype(BF16)


def _swa(proj, rel_bias, sinks, g):
    S = proj.shape[0]
    N = S // BLOCK
    bucket = jnp.asarray(_rel_bucket_table())
    prev = lambda n: jnp.maximum(n - 1, 0)
    return pl.pallas_call(
        _swa_kernel,
        out_shape=jax.ShapeDtypeStruct((S, SWA_Q_W), BF16),
        grid=(N,),
        in_specs=[
            pl.BlockSpec((BLOCK, SWA_Q_W), lambda n: (n, WIDE_QA)),
            pl.BlockSpec((BLOCK, LANES), lambda n: (prev(n), COL_KA)),
            pl.BlockSpec((BLOCK, LANES), lambda n: (n, COL_KA)),
            pl.BlockSpec((BLOCK, LANES), lambda n: (prev(n), COL_VA)),
            pl.BlockSpec((BLOCK, LANES), lambda n: (n, COL_VA)),
            pl.BlockSpec((BLOCK, 2 * BLOCK), lambda n: (0, 0)),
            pl.BlockSpec(memory_space=pltpu.SMEM),
            pl.BlockSpec(memory_space=pltpu.SMEM),
            pl.BlockSpec((1, SWA_Q_W), lambda n: (0, 0)),
        ],
        out_specs=pl.BlockSpec((BLOCK, SWA_Q_W), lambda n: (n, 0)),
        scratch_shapes=[pltpu.VMEM((2, SWA_Q_HEADS, BLOCK, 2 * BLOCK), F32),
                        pltpu.VMEM((BLOCK, SWA_Q_W), F32)],
        compiler_params=_params(("arbitrary",)),
        name="swa",
    )(proj, proj, proj, proj, proj, bucket, rel_bias, sinks, g)


def _cumsum_weights():
    kk = np.arange(BLOCK)
    upper = (kk[:, None] > kk[None, :]).astype(np.float32)
    w = np.concatenate([upper, np.ones((BLOCK, BLOCK), np.float32)], axis=1)
    return np.concatenate([w, w], axis=0)


def _sb_chunk(qq, kb, vb, w2, running, causal):
    nblk = kb.shape[0] // BLOCK
    z = lax.dot_general(qq, kb, (((1,), (1,)), ((), ())), preferred_element_type=F32)
    lg = jnp.log(1.0 + jnp.exp(-jnp.abs(z)))
    logsig = jnp.minimum(z, 0.0) - lg
    log1m = logsig - z
    if causal is not None:
        log1m = jnp.where(causal, log1m, 0.0)
    parts = [None] * nblk
    for c in reversed(range(nblk)):
        cs = slice(c * BLOCK, (c + 1) * BLOCK)
        l1 = log1m[:, cs]
        hi = l1.astype(BF16)
        lo = (l1 - hi.astype(F32)).astype(BF16)
        r = jnp.dot(jnp.concatenate([hi, lo], axis=1), w2, preferred_element_type=F32)
        log_a = logsig[:, cs] + r[:, :BLOCK]
        if running is not None:
            log_a = log_a + running
        a = jnp.exp(log_a)
        if causal is not None:
            a = jnp.where(causal[:, cs], a, 0.0)
        parts[c] = a.astype(BF16)
        running = r[:, BLOCK:] if running is None else running + r[:, BLOCK:]
    amat = parts[0] if nblk == 1 else jnp.concatenate(parts, axis=1)
    return running, jnp.dot(amat, vb, preferred_element_type=F32)


def _sb_kernel(q_ref, kp_ref, kc_ref, vp_ref, vc_ref, kp2_ref, vp2_ref, w2_ref, g_ref, proj_hbm,
               o_ref, qq_ref, kd_ref, vd_ref, acc_ref, oacc_ref, live_ref, sem):
    n = pl.program_id(0)
    lane = lax.broadcasted_iota(jnp.int32, (BLOCK, LANES), 1)
    first = lane < HEAD_DIM
    w2 = w2_ref[...]

    qrow = lax.broadcasted_iota(jnp.int32, (2 * BLOCK, 2 * BLOCK), 0) % BLOCK
    kcol = lax.broadcasted_iota(jnp.int32, (2 * BLOCK, 2 * BLOCK), 1)
    kpos = (n - 1) * BLOCK + kcol
    causal = jnp.logical_and(kpos < n * BLOCK + qrow, kpos >= 0)

    for p in range(PAIRS):
        cols = slice(p * LANES, (p + 1) * LANES)
        q = q_ref[:, cols]
        zq = jnp.zeros_like(q)
        qq = jnp.concatenate([jnp.where(first, q, zq), jnp.where(first, zq, q)], axis=0)
        qq_ref[p] = qq
        kd_ref[p] = kp2_ref[:, cols]
        vd_ref[p] = vp2_ref[:, cols]
        kb = jnp.concatenate([kp_ref[:, cols], kc_ref[:, cols]], axis=0)
        vb = jnp.concatenate([vp_ref[:, cols], vc_ref[:, cols]], axis=0)
        acc, pv = _sb_chunk(qq, kb, vb, w2, None, causal)
        acc_ref[p] = acc
        oacc_ref[p] = pv
        live_ref[p] = (jnp.max(acc) > PRUNE_LOG).astype(jnp.int32)

    def pair_body(p, carry):
        def live():
            return (jnp.max(acc_ref[p]) > PRUNE_LOG).astype(jnp.int32)

        def cond(c):
            j, go = c
            return jnp.logical_and(j >= 0, go > 0)

        def fetch(j, dst, col0, slot):
            src = proj_hbm.at[pl.ds(pl.multiple_of(j * BLOCK, BLOCK), BLOCK),
                              pl.ds(pl.multiple_of(col0 + p * LANES, LANES), LANES)]
            return pltpu.make_async_copy(src, dst.at[p], sem.at[slot])

        def body(c):
            j, _ = c

            @pl.when(j < n - 2)
            def _():
                ck = fetch(j, kd_ref, COL_KS, 0)
                cv = fetch(j, vd_ref, COL_VS, 1)
                ck.start()
                cv.start()
                ck.wait()
                cv.wait()

            acc, pv = _sb_chunk(qq_ref[p], kd_ref[p], vd_ref[p], w2, acc_ref[p], None)
            acc_ref[p] = acc
            oacc_ref[p] += pv
            return j - 1, live()

        lax.while_loop(cond, body, (n - 2, live_ref[p]))
        return carry

    lax.fori_loop(0, PAIRS, pair_body, 0)

    ys = []
    ss = jnp.zeros((BLOCK, 1), F32)
    for p in range(PAIRS):
        y = jnp.where(first, oacc_ref[p, :BLOCK, :], oacc_ref[p, BLOCK:, :])
        ss = ss + jnp.sum(y * y, axis=-1, keepdims=True)
        ys.append(y)
    inv = lax.rsqrt(ss * (1.0 / SB_W) + EPS)
    for p in range(PAIRS):
        cols = slice(p * LANES, (p + 1) * LANES)
        o_ref[:, cols] = (ys[p] * inv * g_ref[:, cols]).astype(BF16)


def _sb(proj, g):
    S = proj.shape[0]
    N = S // BLOCK
    w2 = jnp.asarray(_cumsum_weights(), dtype=BF16)
    back = lambda d: (lambda n: jnp.maximum(n - d, 0))
    wide = lambda rowf, c: pl.BlockSpec((BLOCK, SB_W), lambda n: (rowf(n), c))
    return pl.pallas_call(
        _sb_kernel,
        out_shape=jax.ShapeDtypeStruct((S, SB_W), BF16),
        grid=(N,),
        in_specs=[
            wide(back(0), WIDE_QS),
            wide(back(1), WIDE_KS), wide(back(0), WIDE_KS),
            wide(back(1), WIDE_VS), wide(back(0), WIDE_VS),
            wide(back(2), WIDE_KS), wide(back(2), WIDE_VS),
            pl.BlockSpec((2 * BLOCK, 2 * BLOCK), lambda n: (0, 0)),
            pl.BlockSpec((1, SB_W), lambda n: (0, 0)),
            pl.BlockSpec(memory_space=pl.ANY),
        ],
        out_specs=pl.BlockSpec((BLOCK, SB_W), lambda n: (n, 0)),
        scratch_shapes=[
            pltpu.VMEM((PAIRS, 2 * BLOCK, LANES), BF16),
            pltpu.VMEM((PAIRS, BLOCK, LANES), BF16),
            pltpu.VMEM((PAIRS, BLOCK, LANES), BF16),
            pltpu.VMEM((PAIRS, 2 * BLOCK, BLOCK), F32),
            pltpu.VMEM((PAIRS, 2 * BLOCK, LANES), F32),
            pltpu.SMEM((PAIRS,), jnp.int32),
            pltpu.SemaphoreType.DMA((2,)),
        ],
        compiler_params=_params(("arbitrary",)),
        name="stickbreak",
    )(proj, proj, proj, proj, proj, proj, proj, w2, g, proj)


def _outproj_kernel(ma_ref, mb_ref, x_ref, gm_ref, w_ref, h_ref, hn_ref):
    mix = jnp.concatenate([ma_ref[...], mb_ref[...]], axis=-1)
    h = x_ref[...] + jnp.dot(mix, w_ref[...], preferred_element_type=F32)
    h_ref[...] = h
    hn_ref[...] = (_rms(h) * gm_ref[...]).astype(BF16)


def _outproj(ma, mb, x, gm, w_bf16, tm=512):
    S, D = x.shape
    row = lambda i: (i, 0)
    return pl.pallas_call(
        _outproj_kernel,
        out_shape=(jax.ShapeDtypeStruct((S, D), F32), jax.ShapeDtypeStruct((S, D), BF16)),
        grid=(S // tm,),
        in_specs=[
            pl.BlockSpec((tm, SWA_Q_W), row),
            pl.BlockSpec((tm, SB_W), row),
            pl.BlockSpec((tm, D), row),
            _resident((1, D)),
            _resident((D_MIX, D)),
        ],
        out_specs=(pl.BlockSpec((tm, D), row), pl.BlockSpec((tm, D), row)),
        compiler_params=_params(("arbitrary",)),
        name="outproj",
    )(ma, mb, x, gm, w_bf16)


def _convglu_kernel(hn_ref, halo_ref, wg_ref, wv_ref, wc_ref, bc_ref, wd_ref, h_ref, gf_ref,
                    o_ref, lhs_ref, gate_ref, *, tm):
    i = pl.program_id(0)
    f = pl.program_id(1)

    @pl.when(f == 0)
    def _():
        halo = halo_ref[...]
        lhs_ref[:HALO, :] = jnp.where(i > 0, halo, jnp.zeros_like(halo))
        lhs_ref[HALO:, :] = hn_ref[...]
        o_ref[...] = h_ref[...]

    gate_ref[...] = jnp.dot(lhs_ref[...], wg_ref[...], preferred_element_type=F32)
    val = jnp.dot(hn_ref[...], wv_ref[...], preferred_element_type=F32)
    gc = bc_ref[...]
    for tap in range(CONV_WIDTH):
        off = HALO - (CONV_WIDTH - 1) + tap
        gc = gc + gate_ref[pl.ds(off, tm), :] * wc_ref[tap:tap + 1, :]
    act = (gc * (1.0 / (1.0 + jnp.exp(-gc))) * val).astype(BF16)
    o_ref[...] += jnp.dot(act, wd_ref[...], preferred_element_type=F32)

    @pl.when(f == pl.num_programs(1) - 1)
    def _():
        o_ref[...] = _rms(o_ref[...]) * gf_ref[...]


def _convglu(hn2, h1, w_up_bf16, w_conv, b_conv, w_down_bf16, g_final, tm=1024, tf=512):
    S, D = h1.shape
    nf = D_FF // tf
    halo_blocks = tm // HALO
    return pl.pallas_call(
        functools.partial(_convglu_kernel, tm=tm),
        out_shape=jax.ShapeDtypeStruct((S, D), F32),
        grid=(S // tm, nf),
        in_specs=[
            pl.BlockSpec((tm, D), lambda i, f: (i, 0)),
            pl.BlockSpec((HALO, D), lambda i, f: (jnp.maximum(i * halo_blocks - 1, 0), 0)),
            pl.BlockSpec((D, tf), lambda i, f: (0, f)),
            pl.BlockSpec((D, tf), lambda i, f: (0, nf + f)),
            pl.BlockSpec((CONV_WIDTH, tf), lambda i, f: (0, f)),
            pl.BlockSpec((1, tf), lambda i, f: (0, f)),
            pl.BlockSpec((tf, D), lambda i, f: (f, 0)),
            pl.BlockSpec((tm, D), lambda i, f: (i, 0), pipeline_mode=pl.Buffered(1)),
            _resident((1, D)),
        ],
        out_specs=pl.BlockSpec((tm, D), lambda i, f: (i, 0)),
        scratch_shapes=[
            pltpu.VMEM((tm + HALO, D), BF16),
            pltpu.VMEM((tm + HALO, tf), F32),
        ],
        compiler_params=_params(("arbitrary", "arbitrary"), CONVGLU_VMEM_LIMIT),
        name="convglu",
    )(hn2, hn2, w_up_bf16, w_up_bf16, w_conv, b_conv, w_down_bf16, h1, g_final)


def kernel(x, w_in, g_attn_norm, rel_bias, swa_sinks, g_swa_out, g_sb_out, w_out,
           g_mlp_norm, w_up, w_conv, b_conv, w_down, g_final):
    B, S, D = x.shape
    assert (B, S, D) == (1, SEQ, D_MODEL)
    x2 = x.reshape(S, D)

    col = np.ones((1, D_IN), np.float32)
    col[:, REF_SPLITS[0]:REF_SPLITS[1]] = SCALE
    col[:, REF_SPLITS[3]:REF_SPLITS[4]] = SCALE
    w_in_s = (w_in * jnp.asarray(col)).astype(BF16)
    w_in_b = jnp.concatenate([w_in_s[:, REF_SPLITS[i]:REF_SPLITS[i + 1]] for i in PERM_GROUPS], axis=1)
    w_out_b = w_out.astype(BF16)
    w_up_b = w_up.astype(BF16)
    w_down_b = w_down.astype(BF16)

    proj = _inproj(x2, g_attn_norm.reshape(1, D), w_in_b)
    mix_a = _swa(proj, rel_bias, swa_sinks, g_swa_out.reshape(1, -1))
    mix_b = _sb(proj, g_sb_out.reshape(1, -1))
    h1, hn2 = _outproj(mix_a, mix_b, x2, g_mlp_norm.reshape(1, D), w_out_b)
    out = _convglu(hn2, h1, w_up_b, w_conv, b_conv.reshape(1, -1), w_down_b, g_final.reshape(1, D))
    return out.reshape(B, S, D)
```

```python
import functools
import math

import numpy as np
import jax
import jax.numpy as jnp
from jax import lax
from jax.experimental import pallas as pl
from jax.experimental.pallas import tpu as pltpu

D_MODEL = 2048
SEQ = 16384
HEAD_DIM = 64
SWA_Q_HEADS = 16
SWA_KV_HEADS = 2
SWA_GROUP = SWA_Q_HEADS // SWA_KV_HEADS
SB_HEADS = 16
WINDOW = 128
BLOCK = 128
REL_BUCKETS = 32
REL_MAX_DIST = 128
D_FF = 5632
CONV_WIDTH = 3
EPS = 1e-6
NEG_INF = -1e30

SWA_Q_W = SWA_Q_HEADS * HEAD_DIM
SWA_KV_W = SWA_KV_HEADS * HEAD_DIM
SB_W = SB_HEADS * HEAD_DIM
D_MIX = SWA_Q_W + SB_W
D_IN = SWA_Q_W + 2 * SWA_KV_W + 3 * SB_W

LANES = 128
REF_SPLITS = np.cumsum([0, SWA_Q_W, SWA_KV_W, SWA_KV_W, SB_W, SB_W, SB_W])
PERM_GROUPS = (3, 4, 5, 0, 1, 2)
WIDE_QS, WIDE_KS, WIDE_VS, WIDE_QA = 0, 1, 2, 3
COL_KS = SB_W
COL_VS = 2 * SB_W
COL_KA = (3 * SB_W + SWA_Q_W) // LANES
COL_VA = COL_KA + SWA_KV_W // LANES
PAIRS = SB_W // LANES

SCALE = HEAD_DIM ** -0.5
HALO = 16
VMEM_LIMIT = 56 * 1024 * 1024
CONVGLU_VMEM_LIMIT = 62 * 1024 * 1024

F32 = jnp.float32
BF16 = jnp.bfloat16

PRUNE_LOG = -88.0
LOG2E = math.log2(math.e)
PRUNE_LOG2 = PRUNE_LOG * LOG2E


def _params(sem, vmem=VMEM_LIMIT):
    return pltpu.CompilerParams(dimension_semantics=sem, vmem_limit_bytes=vmem)


def _rms(y):
    return y * lax.rsqrt(jnp.mean(y * y, axis=-1, keepdims=True) + EPS)


def _inproj_kernel(x_ref, g_ref, w_ref, o_ref):
    hn = (_rms(x_ref[...]) * g_ref[...]).astype(BF16)
    o_ref[...] = jnp.dot(hn, w_ref[...], preferred_element_type=F32).astype(BF16)


def _resident(shape):
    return pl.BlockSpec(shape, lambda *_: (0,) * len(shape), pipeline_mode=pl.Buffered(1))


def _inproj(x, g, w_bf16, tm=512):
    S, D = x.shape
    N = w_bf16.shape[1]
    return pl.pallas_call(
        _inproj_kernel,
        out_shape=jax.ShapeDtypeStruct((S, N), BF16),
        grid=(S // tm,),
        in_specs=[
            pl.BlockSpec((tm, D), lambda i: (i, 0)),
            _resident((1, D)),
            _resident((D, N)),
        ],
        out_specs=pl.BlockSpec((tm, N), lambda i: (i, 0)),
        compiler_params=_params(("arbitrary",)),
        name="inproj",
    )(x, g, w_bf16)


def _rel_bucket_table():
    qi = np.arange(BLOCK, dtype=np.int64)[:, None]
    kj = np.arange(2 * BLOCK, dtype=np.int64)[None, :]
    dist = qi + BLOCK - kj
    in_win = (dist >= 0) & (dist < WINDOW)
    dc = np.clip(dist, 0, None)
    max_exact = REL_BUCKETS // 2
    d = np.maximum(dc, 1).astype(np.float32)
    large = max_exact + (np.log(d / np.float32(max_exact)) / np.float32(math.log(REL_MAX_DIST / max_exact))
                         * np.float32(REL_BUCKETS - max_exact)).astype(np.int32)
    large = np.minimum(large, REL_BUCKETS - 1)
    bucket = np.where(dc < max_exact, dc, large).astype(np.int32)
    return np.where(in_win, bucket, -1).astype(np.int32)


def _swa_kernel(q_ref, kp_ref, kc_ref, vp_ref, vc_ref, bucket_ref, relb_ref, sink_ref, g_ref,
                o_ref, bias_ref, y_ref):
    n = pl.program_id(0)

    @pl.when(n == 0)
    def _():
        bucket = bucket_ref[...]
        col = lax.broadcasted_iota(jnp.int32, bucket.shape, 1)
        for h in range(SWA_Q_HEADS):
            b = jnp.full(bucket.shape, NEG_INF, F32)
            for r in range(REL_BUCKETS):
                b = jnp.where(bucket == r, relb_ref[r, h], b)
            bias_ref[0, h] = b
            bias_ref[1, h] = jnp.where(col >= BLOCK, b, NEG_INF)

    first_block = (n == 0).astype(jnp.int32)

    for g in range(SWA_KV_HEADS):
        ks = slice(g * HEAD_DIM, (g + 1) * HEAD_DIM)
        k2 = jnp.concatenate([kp_ref[:, ks], kc_ref[:, ks]], axis=0)
        v2 = jnp.concatenate([vp_ref[:, ks], vc_ref[:, ks]], axis=0)
        for gh in range(SWA_GROUP):
            h = g * SWA_GROUP + gh
            qh = q_ref[:, h * HEAD_DIM:(h + 1) * HEAD_DIM]
            s = lax.dot_general(qh, k2, (((1,), (1,)), ((), ())), preferred_element_type=F32)
            logits = s + bias_ref[first_block, h]
            sink = sink_ref[h]
            m = jnp.maximum(jnp.max(logits, axis=-1, keepdims=True), sink)
            p = jnp.exp(logits - m)
            denom = jnp.sum(p, axis=-1, keepdims=True) + jnp.exp(sink - m)
            w = (p * (1.0 / denom)).astype(BF16)
            y_ref[:, h * HEAD_DIM:(h + 1) * HEAD_DIM] = jnp.dot(w, v2, preferred_element_type=F32)

    o_ref[...] = (_rms(y_ref[...]) * g_ref[...]).astype(BF16)


def _swa(proj, rel_bias, sinks, g):
    S = proj.shape[0]
    N = S // BLOCK
    bucket = jnp.asarray(_rel_bucket_table())
    prev = lambda n: jnp.maximum(n - 1, 0)
    return pl.pallas_call(
        _swa_kernel,
        out_shape=jax.ShapeDtypeStruct((S, SWA_Q_W), BF16),
        grid=(N,),
        in_specs=[
            pl.BlockSpec((BLOCK, SWA_Q_W), lambda n: (n, WIDE_QA)),
            pl.BlockSpec((BLOCK, LANES), lambda n: (prev(n), COL_KA)),
            pl.BlockSpec((BLOCK, LANES), lambda n: (n, COL_KA)),
            pl.BlockSpec((BLOCK, LANES), lambda n: (prev(n), COL_VA)),
            pl.BlockSpec((BLOCK, LANES), lambda n: (n, COL_VA)),
            pl.BlockSpec((BLOCK, 2 * BLOCK), lambda n: (0, 0)),
            pl.BlockSpec(memory_space=pltpu.SMEM),
            pl.BlockSpec(memory_space=pltpu.SMEM),
            pl.BlockSpec((1, SWA_Q_W), lambda n: (0, 0)),
        ],
        out_specs=pl.BlockSpec((BLOCK, SWA_Q_W), lambda n: (n, 0)),
        scratch_shapes=[pltpu.VMEM((2, SWA_Q_HEADS, BLOCK, 2 * BLOCK), F32),
                        pltpu.VMEM((BLOCK, SWA_Q_W), F32)],
        compiler_params=_params(("arbitrary",)),
        name="swa",
    )(proj, proj, proj, proj, proj, bucket, rel_bias, sinks, g)


def _cumsum_weights():
    kk = np.arange(BLOCK)
    upper = (kk[:, None] > kk[None, :]).astype(np.float32)
    w = np.concatenate([upper, np.ones((BLOCK, BLOCK), np.float32)], axis=1)
    return np.concatenate([w, w], axis=0)


def _sb_chunk(qq, kb, vb, w2, running, masks):
    nblk = kb.shape[0] // BLOCK
    z = lax.dot_general(qq, kb, (((1,), (1,)), ((), ())), preferred_element_type=F32) * LOG2E
    lg = jnp.log2(1.0 + jnp.exp2(-jnp.abs(z)))
    logsig = jnp.minimum(z, 0.0) - lg
    log1m = logsig - z
    parts = [None] * nblk
    for c in reversed(range(nblk)):
        cs = slice(c * BLOCK, (c + 1) * BLOCK)
        l1 = log1m[:, cs]
        if masks[c] is not None:
            l1 = jnp.where(masks[c], l1, 0.0)
        hi = l1.astype(BF16)
        lo = (l1 - hi.astype(F32)).astype(BF16)
        r = jnp.dot(jnp.concatenate([hi, lo], axis=1), w2, preferred_element_type=F32)
        log_a = logsig[:, cs] + r[:, :BLOCK]
        if running is not None:
            log_a = log_a + running
        a = jnp.exp2(log_a)
        if masks[c] is not None:
            a = jnp.where(masks[c], a, 0.0)
        parts[c] = a.astype(BF16)
        running = r[:, BLOCK:] if running is None else running + r[:, BLOCK:]
    amat = parts[0] if nblk == 1 else jnp.concatenate(parts, axis=1)
    return running, jnp.dot(amat, vb, preferred_element_type=F32)


def _sb_kernel(q_ref, kp_ref, kc_ref, vp_ref, vc_ref, kp2_ref, vp2_ref, w2_ref, g_ref, proj_hbm,
               o_ref, qq_ref, kd_ref, vd_ref, acc_ref, oacc_ref, live_ref, sem):
    n = pl.program_id(0)
    lane = lax.broadcasted_iota(jnp.int32, (BLOCK, LANES), 1)
    first = lane < HEAD_DIM
    w2 = w2_ref[...]

    qrow = lax.broadcasted_iota(jnp.int32, (2 * BLOCK, BLOCK), 0) % BLOCK
    kcol = lax.broadcasted_iota(jnp.int32, (2 * BLOCK, BLOCK), 1)
    diag = kcol < qrow

    def phase1(with_prev):
        for p in range(PAIRS):
            cols = slice(p * LANES, (p + 1) * LANES)
            q = q_ref[:, cols]
            zq = jnp.zeros_like(q)
            qq = jnp.concatenate([jnp.where(first, q, zq), jnp.where(first, zq, q)], axis=0)
            qq_ref[p] = qq
            kd_ref[p] = kp2_ref[:, cols]
            vd_ref[p] = vp2_ref[:, cols]
            if with_prev:
                kb = jnp.concatenate([kp_ref[:, cols], kc_ref[:, cols]], axis=0)
                vb = jnp.concatenate([vp_ref[:, cols], vc_ref[:, cols]], axis=0)
                masks = [None, diag]
            else:
                kb, vb, masks = kc_ref[:, cols], vc_ref[:, cols], [diag]
            acc, pv = _sb_chunk(qq, kb, vb, w2, None, masks)
            acc_ref[p] = acc
            oacc_ref[p] = pv
            live_ref[p] = (jnp.max(acc) > PRUNE_LOG2).astype(jnp.int32)

    pl.when(n > 0)(functools.partial(phase1, True))
    pl.when(n == 0)(functools.partial(phase1, False))

    def pair_body(p, carry):
        def live():
            return (jnp.max(acc_ref[p]) > PRUNE_LOG2).astype(jnp.int32)

        def cond(c):
            j, go = c
            return jnp.logical_and(j >= 0, go > 0)

        def fetch(j, dst, col0, slot):
            src = proj_hbm.at[pl.ds(pl.multiple_of(j * BLOCK, BLOCK), BLOCK),
                              pl.ds(pl.multiple_of(col0 + p * LANES, LANES), LANES)]
            return pltpu.make_async_copy(src, dst.at[p], sem.at[slot])

        def body(c):
            j, _ = c

            @pl.when(j < n - 2)
            def _():
                ck = fetch(j, kd_ref, COL_KS, 0)
                cv = fetch(j, vd_ref, COL_VS, 1)
                ck.start()
                cv.start()
                ck.wait()
                cv.wait()

            acc, pv = _sb_chunk(qq_ref[p], kd_ref[p], vd_ref[p], w2, acc_ref[p], [None])
            acc_ref[p] = acc
            oacc_ref[p] += pv
            return j - 1, live()

        lax.while_loop(cond, body, (n - 2, live_ref[p]))
        return carry

    lax.fori_loop(0, PAIRS, pair_body, 0)

    ys = []
    ss = jnp.zeros((BLOCK, 1), F32)
    for p in range(PAIRS):
        y = jnp.where(first, oacc_ref[p, :BLOCK, :], oacc_ref[p, BLOCK:, :])
        ss = ss + jnp.sum(y * y, axis=-1, keepdims=True)
        ys.append(y)
    inv = lax.rsqrt(ss * (1.0 / SB_W) + EPS)
    for p in range(PAIRS):
        cols = slice(p * LANES, (p + 1) * LANES)
        o_ref[:, cols] = (ys[p] * inv * g_ref[:, cols]).astype(BF16)


def _sb(proj, g):
    S = proj.shape[0]
    N = S // BLOCK
    w2 = jnp.asarray(_cumsum_weights(), dtype=BF16)
    back = lambda d: (lambda n: jnp.maximum(n - d, 0))
    wide = lambda rowf, c: pl.BlockSpec((BLOCK, SB_W), lambda n: (rowf(n), c))
    return pl.pallas_call(
        _sb_kernel,
        out_shape=jax.ShapeDtypeStruct((S, SB_W), BF16),
        grid=(N,),
        in_specs=[
            wide(back(0), WIDE_QS),
            wide(back(1), WIDE_KS), wide(back(0), WIDE_KS),
            wide(back(1), WIDE_VS), wide(back(0), WIDE_VS),
            wide(back(2), WIDE_KS), wide(back(2), WIDE_VS),
            pl.BlockSpec((2 * BLOCK, 2 * BLOCK), lambda n: (0, 0)),
            pl.BlockSpec((1, SB_W), lambda n: (0, 0)),
            pl.BlockSpec(memory_space=pl.ANY),
        ],
        out_specs=pl.BlockSpec((BLOCK, SB_W), lambda n: (n, 0)),
        scratch_shapes=[
            pltpu.VMEM((PAIRS, 2 * BLOCK, LANES), BF16),
            pltpu.VMEM((PAIRS, BLOCK, LANES), BF16),
            pltpu.VMEM((PAIRS, BLOCK, LANES), BF16),
            pltpu.VMEM((PAIRS, 2 * BLOCK, BLOCK), F32),
            pltpu.VMEM((PAIRS, 2 * BLOCK, LANES), F32),
            pltpu.SMEM((PAIRS,), jnp.int32),
            pltpu.SemaphoreType.DMA((2,)),
        ],
        compiler_params=_params(("arbitrary",)),
        name="stickbreak",
    )(proj, proj, proj, proj, proj, proj, proj, w2, g, proj)


def _outproj_kernel(ma_ref, mb_ref, x_ref, gm_ref, w_ref, h_ref, hn_ref):
    mix = jnp.concatenate([ma_ref[...], mb_ref[...]], axis=-1)
    h = x_ref[...] + jnp.dot(mix, w_ref[...], preferred_element_type=F32)
    h_ref[...] = h
    hn_ref[...] = (_rms(h) * gm_ref[...]).astype(BF16)


def _outproj(ma, mb, x, gm, w_bf16, tm=512):
    S, D = x.shape
    row = lambda i: (i, 0)
    return pl.pallas_call(
        _outproj_kernel,
        out_shape=(jax.ShapeDtypeStruct((S, D), F32), jax.ShapeDtypeStruct((S, D), BF16)),
        grid=(S // tm,),
        in_specs=[
            pl.BlockSpec((tm, SWA_Q_W), row),
            pl.BlockSpec((tm, SB_W), row),
            pl.BlockSpec((tm, D), row),
            _resident((1, D)),
            _resident((D_MIX, D)),
        ],
        out_specs=(pl.BlockSpec((tm, D), row), pl.BlockSpec((tm, D), row)),
        compiler_params=_params(("arbitrary",)),
        name="outproj",
    )(ma, mb, x, gm, w_bf16)


def _convglu_kernel(hn_ref, halo_ref, wg_ref, wv_ref, wc_ref, bc_ref, wd_ref, h_ref, gf_ref,
                    o_ref, lhs_ref, gate_ref, *, tm):
    i = pl.program_id(0)
    f = pl.program_id(1)

    @pl.when(f == 0)
    def _():
        halo = halo_ref[...]
        lhs_ref[:HALO, :] = jnp.where(i > 0, halo, jnp.zeros_like(halo))
        lhs_ref[HALO:, :] = hn_ref[...]
        o_ref[...] = h_ref[...]

    gate_ref[...] = jnp.dot(lhs_ref[...], wg_ref[...], preferred_element_type=F32)
    val = jnp.dot(hn_ref[...], wv_ref[...], preferred_element_type=F32)
    gc = bc_ref[...]
    for tap in range(CONV_WIDTH):
        off = HALO - (CONV_WIDTH - 1) + tap
        gc = gc + gate_ref[pl.ds(off, tm), :] * wc_ref[tap:tap + 1, :]
    act = (gc * (1.0 / (1.0 + jnp.exp(-gc))) * val).astype(BF16)
    o_ref[...] += jnp.dot(act, wd_ref[...], preferred_element_type=F32)

    @pl.when(f == pl.num_programs(1) - 1)
    def _():
        o_ref[...] = _rms(o_ref[...]) * gf_ref[...]


def _convglu(hn2, h1, w_up_bf16, w_conv, b_conv, w_down_bf16, g_final, tm=1024, tf=512):
    S, D = h1.shape
    nf = D_FF // tf
    halo_blocks = tm // HALO
    return pl.pallas_call(
        functools.partial(_convglu_kernel, tm=tm),
        out_shape=jax.ShapeDtypeStruct((S, D), F32),
        grid=(S // tm, nf),
        in_specs=[
            pl.BlockSpec((tm, D), lambda i, f: (i, 0)),
            pl.BlockSpec((HALO, D), lambda i, f: (jnp.maximum(i * halo_blocks - 1, 0), 0)),
            pl.BlockSpec((D, tf), lambda i, f: (0, f)),
            pl.BlockSpec((D, tf), lambda i, f: (0, nf + f)),
            pl.BlockSpec((CONV_WIDTH, tf), lambda i, f: (0, f)),
            pl.BlockSpec((1, tf), lambda i, f: (0, f)),
            pl.BlockSpec((tf, D), lambda i, f: (f, 0)),
            pl.BlockSpec((tm, D), lambda i, f: (i, 0)),
            _resident((1, D)),
        ],
        out_specs=pl.BlockSpec((tm, D), lambda i, f: (i, 0)),
        scratch_shapes=[
            pltpu.VMEM((tm + HALO, D), BF16),
            pltpu.VMEM((tm + HALO, tf), F32),
        ],
        compiler_params=_params(("arbitrary", "arbitrary"), CONVGLU_VMEM_LIMIT),
        name="convglu",
    )(hn2, hn2, w_up_bf16, w_up_bf16, w_conv, b_conv, w_down_bf16, h1, g_final)


def kernel(x, w_in, g_attn_norm, rel_bias, swa_sinks, g_swa_out, g_sb_out, w_out,
           g_mlp_norm, w_up, w_conv, b_conv, w_down, g_final):
    B, S, D = x.shape
    assert (B, S, D) == (1, SEQ, D_MODEL)
    x2 = x.reshape(S, D)

    col = np.ones((1, D_IN), np.float32)
    col[:, REF_SPLITS[0]:REF_SPLITS[1]] = SCALE
    col[:, REF_SPLITS[3]:REF_SPLITS[4]] = SCALE
    w_in_s = (w_in * jnp.asarray(col)).astype(BF16)
    w_in_b = jnp.concatenate([w_in_s[:, REF_SPLITS[i]:REF_SPLITS[i + 1]] for i in PERM_GROUPS], axis=1)
    w_out_b = w_out.astype(BF16)
    w_up_b = w_up.astype(BF16)
    w_down_b = w_down.astype(BF16)

    proj = _inproj(x2, g_attn_norm.reshape(1, D), w_in_b)
    mix_a = _swa(proj, rel_bias, swa_sinks, g_swa_out.reshape(1, -1))
    mix_b = _sb(proj, g_sb_out.reshape(1, -1))
    h1, hn2 = _outproj(mix_a, mix_b, x2, g_mlp_norm.reshape(1, D), w_out_b)
    out = _convglu(hn2, h1, w_up_b, w_conv, b_conv.reshape(1, -1), w_down_b, g_final.reshape(1, D))
    return out.reshape(B, S, D)
```

```python
import functools
import math

import numpy as np
import jax
import jax.numpy as jnp
from jax import lax
from jax.experimental import pallas as pl
from jax.experimental.pallas import tpu as pltpu

D_MODEL = 2048
SEQ = 16384
HEAD_DIM = 64
SWA_Q_HEADS = 16
SWA_KV_HEADS = 2
SWA_GROUP = SWA_Q_HEADS // SWA_KV_HEADS
SB_HEADS = 16
WINDOW = 128
BLOCK = 128
REL_BUCKETS = 32
REL_MAX_DIST = 128
D_FF = 5632
CONV_WIDTH = 3
EPS = 1e-6
NEG_INF = -1e30

SWA_Q_W = SWA_Q_HEADS * HEAD_DIM
SWA_KV_W = SWA_KV_HEADS * HEAD_DIM
SB_W = SB_HEADS * HEAD_DIM
D_MIX = SWA_Q_W + SB_W
D_IN = SWA_Q_W + 2 * SWA_KV_W + 3 * SB_W

LANES = 128
REF_SPLITS = np.cumsum([0, SWA_Q_W, SWA_KV_W, SWA_KV_W, SB_W, SB_W, SB_W])
SWA_HEAD_ORDER = tuple(h for b in range(SWA_Q_HEADS // 2) for h in (b, SWA_Q_HEADS // 2 + b))
WIDE_QS, WIDE_KS, WIDE_VS, WIDE_QA = 0, 1, 2, 3
COL_KS = SB_W
COL_VS = 2 * SB_W
COL_KA = (3 * SB_W + SWA_Q_W) // LANES
COL_VA = COL_KA + SWA_KV_W // LANES
PAIRS = SB_W // LANES

SCALE = HEAD_DIM ** -0.5
HALO = 16
VMEM_LIMIT = 56 * 1024 * 1024
CONVGLU_VMEM_LIMIT = 62 * 1024 * 1024

F32 = jnp.float32
BF16 = jnp.bfloat16

PRUNE_LOG = -88.0
LOG2E = math.log2(math.e)
PRUNE_LOG2 = PRUNE_LOG * LOG2E


def _params(sem, vmem=VMEM_LIMIT):
    return pltpu.CompilerParams(dimension_semantics=sem, vmem_limit_bytes=vmem)


def _rms(y):
    return y * lax.rsqrt(jnp.mean(y * y, axis=-1, keepdims=True) + EPS)


def _inproj_kernel(x_ref, g_ref, w_ref, o_ref):
    hn = (_rms(x_ref[...]) * g_ref[...]).astype(BF16)
    o_ref[...] = jnp.dot(hn, w_ref[...], preferred_element_type=F32).astype(BF16)


def _resident(shape):
    return pl.BlockSpec(shape, lambda *_: (0,) * len(shape), pipeline_mode=pl.Buffered(1))


def _inproj(x, g, w_bf16, tm=512):
    S, D = x.shape
    N = w_bf16.shape[1]
    return pl.pallas_call(
        _inproj_kernel,
        out_shape=jax.ShapeDtypeStruct((S, N), BF16),
        grid=(S // tm,),
        in_specs=[
            pl.BlockSpec((tm, D), lambda i: (i, 0)),
            _resident((1, D)),
            _resident((D, N)),
        ],
        out_specs=pl.BlockSpec((tm, N), lambda i: (i, 0)),
        compiler_params=_params(("arbitrary",)),
        name="inproj",
    )(x, g, w_bf16)


def _rel_bucket_table():
    qi = np.arange(BLOCK, dtype=np.int64)[None, :]
    kj = np.arange(2 * BLOCK, dtype=np.int64)[:, None]
    dist = qi + BLOCK - kj
    in_win = (dist >= 0) & (dist < WINDOW)
    dc = np.clip(dist, 0, None)
    max_exact = REL_BUCKETS // 2
    d = np.maximum(dc, 1).astype(np.float32)
    large = max_exact + (np.log(d / np.float32(max_exact)) / np.float32(math.log(REL_MAX_DIST / max_exact))
                         * np.float32(REL_BUCKETS - max_exact)).astype(np.int32)
    large = np.minimum(large, REL_BUCKETS - 1)
    bucket = np.where(dc < max_exact, dc, large).astype(np.int32)
    return np.where(in_win, bucket, -1).astype(np.int32)


def _swa_kernel(q_ref, kp_ref, kc_ref, vp_ref, vc_ref, bucket_ref, relb_ref, sink_ref, g_ref,
                o_ref, bias_ref, yt_ref):
    n = pl.program_id(0)
    half = SWA_Q_HEADS // 2

    @pl.when(n == 0)
    def _():
        bucket = bucket_ref[...]
        krow = lax.broadcasted_iota(jnp.int32, bucket.shape, 0)
        for h in range(SWA_Q_HEADS):
            t = jnp.full(bucket.shape, NEG_INF, F32)
            for r in range(REL_BUCKETS):
                t = jnp.where(bucket == r, relb_ref[r, h], t)
            cols = slice((h // half) * BLOCK, (h // half + 1) * BLOCK)
            bias_ref[0, h % half, :, cols] = t
            bias_ref[1, h % half, :, cols] = jnp.where(krow >= BLOCK, t, NEG_INF)

    first_block = (n == 0).astype(jnp.int32)
    lane = lax.broadcasted_iota(jnp.int32, (BLOCK, LANES), 1)
    lo = lane < HEAD_DIM
    second = lax.broadcasted_iota(jnp.int32, (1, 2 * BLOCK), 1) >= BLOCK

    k2 = jnp.concatenate([kp_ref[...], kc_ref[...]], axis=0)
    v2t = jnp.concatenate([vp_ref[...], vc_ref[...]], axis=0).T

    blocks = range(half)
    logits, sinks = [], []
    for b in blocks:
        q = q_ref[:, b * LANES:(b + 1) * LANES]
        zq = jnp.zeros_like(q)
        qw = jnp.concatenate([jnp.where(lo, q, zq), jnp.where(lo, zq, q)], axis=0)
        st = lax.dot_general(k2, qw, (((1,), (1,)), ((), ())), preferred_element_type=F32)
        logits.append(st + bias_ref[first_block, b])
        sinks.append(jnp.where(second, sink_ref[half + b], sink_ref[b]))
    ms = [jnp.maximum(jnp.max(logits[b], axis=0, keepdims=True), sinks[b]) for b in blocks]
    ps = [jnp.exp(logits[b] - ms[b]) for b in blocks]
    invs = [1.0 / (jnp.sum(ps[b], axis=0, keepdims=True) + jnp.exp(sinks[b] - ms[b])) for b in blocks]
    for b in blocks:
        w = (ps[b] * invs[b]).astype(BF16)
        out = jnp.dot(v2t, w, preferred_element_type=F32)
        yt_ref[b * LANES:b * LANES + HEAD_DIM, :] = out[:HEAD_DIM, :BLOCK]
        yt_ref[b * LANES + HEAD_DIM:(b + 1) * LANES, :] = out[HEAD_DIM:, BLOCK:]

    yt = yt_ref[...]
    inv = lax.rsqrt(jnp.mean(yt * yt, axis=0, keepdims=True) + EPS)
    o_ref[...] = ((yt * inv).T * g_ref[...]).astype(BF16)


def _swa(proj, rel_bias, sinks, g):
    S = proj.shape[0]
    N = S // BLOCK
    bucket = jnp.asarray(_rel_bucket_table())
    prev = lambda n: jnp.maximum(n - 1, 0)
    return pl.pallas_call(
        _swa_kernel,
        out_shape=jax.ShapeDtypeStruct((S, SWA_Q_W), BF16),
        grid=(N,),
        in_specs=[
            pl.BlockSpec((BLOCK, SWA_Q_W), lambda n: (n, WIDE_QA)),
            pl.BlockSpec((BLOCK, LANES), lambda n: (prev(n), COL_KA)),
            pl.BlockSpec((BLOCK, LANES), lambda n: (n, COL_KA)),
            pl.BlockSpec((BLOCK, LANES), lambda n: (prev(n), COL_VA)),
            pl.BlockSpec((BLOCK, LANES), lambda n: (n, COL_VA)),
            _resident((2 * BLOCK, BLOCK)),
            pl.BlockSpec(memory_space=pltpu.SMEM),
            pl.BlockSpec(memory_space=pltpu.SMEM),
            _resident((1, SWA_Q_W)),
        ],
        out_specs=pl.BlockSpec((BLOCK, SWA_Q_W), lambda n: (n, 0)),
        scratch_shapes=[pltpu.VMEM((2, SWA_Q_HEADS // 2, 2 * BLOCK, 2 * BLOCK), F32),
                        pltpu.VMEM((SWA_Q_W, BLOCK), F32)],
        compiler_params=_params(("arbitrary",)),
        name="swa",
    )(proj, proj, proj, proj, proj, bucket, rel_bias, sinks, g)


def _cumsum_weights():
    kk = np.arange(BLOCK)
    upper = (kk[:, None] > kk[None, :]).astype(np.float32)
    w = np.concatenate([upper, np.ones((BLOCK, BLOCK), np.float32)], axis=1)
    return np.concatenate([w, w], axis=0)


def _sb_chunks(items, w2):
    idx = range(len(items))
    nblk = items[0][1].shape[0] // BLOCK
    zs = [lax.dot_general(qq, kb, (((1,), (1,)), ((), ())), preferred_element_type=F32) * LOG2E
          for qq, kb, _, _, _ in items]
    lgs = [jnp.log2(1.0 + jnp.exp2(-jnp.abs(z))) for z in zs]
    logsigs = [jnp.minimum(zs[i], 0.0) - lgs[i] for i in idx]
    log1ms = [logsigs[i] - zs[i] for i in idx]
    runnings = [it[3] for it in items]
    parts = [[None] * nblk for _ in idx]
    for c in reversed(range(nblk)):
        cs = slice(c * BLOCK, (c + 1) * BLOCK)
        rs = []
        for i in idx:
            mask = items[i][4][c]
            l1 = log1ms[i][:, cs]
            if mask is not None:
                l1 = jnp.where(mask, l1, 0.0)
            hi = l1.astype(BF16)
            lo = (l1 - hi.astype(F32)).astype(BF16)
            rs.append(jnp.dot(jnp.concatenate([hi, lo], axis=1), w2, preferred_element_type=F32))
        for i in idx:
            mask = items[i][4][c]
            log_a = logsigs[i][:, cs] + rs[i][:, :BLOCK]
            if runnings[i] is not None:
                log_a = log_a + runnings[i]
            a = jnp.exp2(log_a)
            if mask is not None:
                a = jnp.where(mask, a, 0.0)
            parts[i][c] = a.astype(BF16)
            runnings[i] = rs[i][:, BLOCK:] if runnings[i] is None else runnings[i] + rs[i][:, BLOCK:]
    out = []
    for i in idx:
        amat = parts[i][0] if nblk == 1 else jnp.concatenate(parts[i], axis=1)
        out.append((runnings[i], jnp.dot(amat, items[i][2], preferred_element_type=F32)))
    return out


def _sb_kernel(q_ref, kp_ref, kc_ref, vp_ref, vc_ref, kp2_ref, vp2_ref, w2_ref, g_ref, proj_hbm,
               o_ref, qq_ref, kd_ref, vd_ref, acc_ref, oacc_ref, live_ref, sem):
    n = pl.program_id(0)
    lane = lax.broadcasted_iota(jnp.int32, (BLOCK, LANES), 1)
    first = lane < HEAD_DIM
    w2 = w2_ref[...]

    qrow = lax.broadcasted_iota(jnp.int32, (2 * BLOCK, BLOCK), 0) % BLOCK
    kcol = lax.broadcasted_iota(jnp.int32, (2 * BLOCK, BLOCK), 1)
    diag = kcol < qrow

    def phase1(with_prev):
        items = []
        for p in range(PAIRS):
            cols = slice(p * LANES, (p + 1) * LANES)
            q = q_ref[:, cols]
            zq = jnp.zeros_like(q)
            qq = jnp.concatenate([jnp.where(first, q, zq), jnp.where(first, zq, q)], axis=0)
            qq_ref[p] = qq
            kd_ref[p] = kp2_ref[:, cols]
            vd_ref[p] = vp2_ref[:, cols]
            if with_prev:
                kb = jnp.concatenate([kp_ref[:, cols], kc_ref[:, cols]], axis=0)
                vb = jnp.concatenate([vp_ref[:, cols], vc_ref[:, cols]], axis=0)
                masks = [None, diag]
            else:
                kb, vb, masks = kc_ref[:, cols], vc_ref[:, cols], [diag]
            items.append((qq, kb, vb, None, masks))
        for p, (acc, pv) in enumerate(_sb_chunks(items, w2)):
            acc_ref[p] = acc
            oacc_ref[p] = pv
            live_ref[p] = (jnp.max(acc) > PRUNE_LOG2).astype(jnp.int32)

    pl.when(n > 0)(functools.partial(phase1, True))
    pl.when(n == 0)(functools.partial(phase1, False))

    def pair_body(p, carry):
        def live():
            return (jnp.max(acc_ref[p]) > PRUNE_LOG2).astype(jnp.int32)

        def cond(c):
            j, go = c
            return jnp.logical_and(j >= 0, go > 0)

        def fetch(j, dst, col0, slot):
            src = proj_hbm.at[pl.ds(pl.multiple_of(j * BLOCK, BLOCK), BLOCK),
                              pl.ds(pl.multiple_of(col0 + p * LANES, LANES), LANES)]
            return pltpu.make_async_copy(src, dst.at[p], sem.at[slot])

        def body(c):
            j, _ = c

            @pl.when(j < n - 2)
            def _():
                ck = fetch(j, kd_ref, COL_KS, 0)
                cv = fetch(j, vd_ref, COL_VS, 1)
                ck.start()
                cv.start()
                ck.wait()
                cv.wait()

            (acc, pv), = _sb_chunks([(qq_ref[p], kd_ref[p], vd_ref[p], acc_ref[p], [None])], w2)
            acc_ref[p] = acc
            oacc_ref[p] += pv
            return j - 1, live()

        lax.while_loop(cond, body, (n - 2, live_ref[p]))
        return carry

    lax.fori_loop(0, PAIRS, pair_body, 0)

    ys = []
    ss = jnp.zeros((BLOCK, 1), F32)
    for p in range(PAIRS):
        y = jnp.where(first, oacc_ref[p, :BLOCK, :], oacc_ref[p, BLOCK:, :])
        ss = ss + jnp.sum(y * y, axis=-1, keepdims=True)
        ys.append(y)
    inv = lax.rsqrt(ss * (1.0 / SB_W) + EPS)
    for p in range(PAIRS):
        cols = slice(p * LANES, (p + 1) * LANES)
        o_ref[:, cols] = (ys[p] * inv * g_ref[:, cols]).astype(BF16)


def _sb(proj, g):
    S = proj.shape[0]
    N = S // BLOCK
    w2 = jnp.asarray(_cumsum_weights(), dtype=BF16)
    back = lambda d: (lambda n: jnp.maximum(n - d, 0))
    wide = lambda rowf, c: pl.BlockSpec((BLOCK, SB_W), lambda n: (rowf(n), c))
    return pl.pallas_call(
        _sb_kernel,
        out_shape=jax.ShapeDtypeStruct((S, SB_W), BF16),
        grid=(N,),
        in_specs=[
            wide(back(0), WIDE_QS),
            wide(back(1), WIDE_KS), wide(back(0), WIDE_KS),
            wide(back(1), WIDE_VS), wide(back(0), WIDE_VS),
            wide(back(2), WIDE_KS), wide(back(2), WIDE_VS),
            pl.BlockSpec((2 * BLOCK, 2 * BLOCK), lambda n: (0, 0)),
            pl.BlockSpec((1, SB_W), lambda n: (0, 0)),
            pl.BlockSpec(memory_space=pl.ANY),
        ],
        out_specs=pl.BlockSpec((BLOCK, SB_W), lambda n: (n, 0)),
        scratch_shapes=[
            pltpu.VMEM((PAIRS, 2 * BLOCK, LANES), BF16),
            pltpu.VMEM((PAIRS, BLOCK, LANES), BF16),
            pltpu.VMEM((PAIRS, BLOCK, LANES), BF16),
            pltpu.VMEM((PAIRS, 2 * BLOCK, BLOCK), F32),
            pltpu.VMEM((PAIRS, 2 * BLOCK, LANES), F32),
            pltpu.SMEM((PAIRS,), jnp.int32),
            pltpu.SemaphoreType.DMA((2,)),
        ],
        compiler_params=_params(("arbitrary",)),
        name="stickbreak",
    )(proj, proj, proj, proj, proj, proj, proj, w2, g, proj)


def _outproj_kernel(ma_ref, mb_ref, x_ref, gm_ref, w_ref, h_ref, hn_ref):
    mix = jnp.concatenate([ma_ref[...], mb_ref[...]], axis=-1)
    h = x_ref[...] + jnp.dot(mix, w_ref[...], preferred_element_type=F32)
    h_ref[...] = h
    hn_ref[...] = (_rms(h) * gm_ref[...]).astype(BF16)


def _outproj(ma, mb, x, gm, w_bf16, tm=512):
    S, D = x.shape
    row = lambda i: (i, 0)
    return pl.pallas_call(
        _outproj_kernel,
        out_shape=(jax.ShapeDtypeStruct((S, D), F32), jax.ShapeDtypeStruct((S, D), BF16)),
        grid=(S // tm,),
        in_specs=[
            pl.BlockSpec((tm, SWA_Q_W), row),
            pl.BlockSpec((tm, SB_W), row),
            pl.BlockSpec((tm, D), row),
            _resident((1, D)),
            _resident((D_MIX, D)),
        ],
        out_specs=(pl.BlockSpec((tm, D), row), pl.BlockSpec((tm, D), row)),
        compiler_params=_params(("arbitrary",)),
        name="outproj",
    )(ma, mb, x, gm, w_bf16)


def _convglu_kernel(hn_ref, halo_ref, wg_ref, wv_ref, wc_ref, bc_ref, wd_ref, h_ref, gf_ref,
                    o_ref, lhs_ref, gate_ref, *, tm):
    i = pl.program_id(0)
    f = pl.program_id(1)

    @pl.when(f == 0)
    def _():
        halo = halo_ref[...]
        lhs_ref[:HALO, :] = jnp.where(i > 0, halo, jnp.zeros_like(halo))
        lhs_ref[HALO:, :] = hn_ref[...]
        o_ref[...] = h_ref[...]

    gate_ref[...] = jnp.dot(lhs_ref[...], wg_ref[...], preferred_element_type=F32)
    val = jnp.dot(hn_ref[...], wv_ref[...], preferred_element_type=F32)
    gc = bc_ref[...]
    for tap in range(CONV_WIDTH):
        off = HALO - (CONV_WIDTH - 1) + tap
        gc = gc + gate_ref[pl.ds(off, tm), :] * wc_ref[tap:tap + 1, :]
    act = (gc * (1.0 / (1.0 + jnp.exp(-gc))) * val).astype(BF16)
    o_ref[...] += jnp.dot(act, wd_ref[...], preferred_element_type=F32)

    @pl.when(f == pl.num_programs(1) - 1)
    def _():
        o_ref[...] = _rms(o_ref[...]) * gf_ref[...]


def _convglu(hn2, h1, w_up_bf16, w_conv, b_conv, w_down_bf16, g_final, tm=1024, tf=512):
    S, D = h1.shape
    nf = D_FF // tf
    halo_blocks = tm // HALO
    return pl.pallas_call(
        functools.partial(_convglu_kernel, tm=tm),
        out_shape=jax.ShapeDtypeStruct((S, D), F32),
        grid=(S // tm, nf),
        in_specs=[
            pl.BlockSpec((tm, D), lambda i, f: (i, 0)),
            pl.BlockSpec((HALO, D), lambda i, f: (jnp.maximum(i * halo_blocks - 1, 0), 0)),
            pl.BlockSpec((D, tf), lambda i, f: (0, f)),
            pl.BlockSpec((D, tf), lambda i, f: (0, nf + f)),
            pl.BlockSpec((CONV_WIDTH, tf), lambda i, f: (0, f)),
            pl.BlockSpec((1, tf), lambda i, f: (0, f)),
            pl.BlockSpec((tf, D), lambda i, f: (f, 0)),
            pl.BlockSpec((tm, D), lambda i, f: (i, 0)),
            _resident((1, D)),
        ],
        out_specs=pl.BlockSpec((tm, D), lambda i, f: (i, 0)),
        scratch_shapes=[
            pltpu.VMEM((tm + HALO, D), BF16),
            pltpu.VMEM((tm + HALO, tf), F32),
        ],
        compiler_params=_params(("arbitrary", "arbitrary"), CONVGLU_VMEM_LIMIT),
        name="convglu",
    )(hn2, hn2, w_up_bf16, w_up_bf16, w_conv, b_conv, w_down_bf16, h1, g_final)


def kernel(x, w_in, g_attn_norm, rel_bias, swa_sinks, g_swa_out, g_sb_out, w_out,
           g_mlp_norm, w_up, w_conv, b_conv, w_down, g_final):
    B, S, D = x.shape
    assert (B, S, D) == (1, SEQ, D_MODEL)
    x2 = x.reshape(S, D)

    col = np.ones((1, D_IN), np.float32)
    col[:, REF_SPLITS[0]:REF_SPLITS[1]] = SCALE
    col[:, REF_SPLITS[3]:REF_SPLITS[4]] = SCALE
    w_in_s = (w_in * jnp.asarray(col)).astype(BF16)
    head_cols = lambda a, h: a[..., h * HEAD_DIM:(h + 1) * HEAD_DIM]
    qa_cols = [head_cols(w_in_s, h) for h in SWA_HEAD_ORDER]
    w_in_b = jnp.concatenate([w_in_s[:, REF_SPLITS[i]:REF_SPLITS[i + 1]] for i in (3, 4, 5)]
                             + qa_cols + [w_in_s[:, REF_SPLITS[1]:REF_SPLITS[3]]], axis=1)
    g_a = jnp.concatenate([head_cols(g_swa_out, h) for h in SWA_HEAD_ORDER]).reshape(1, -1)
    w_out_b = jnp.concatenate([w_out[h * HEAD_DIM:(h + 1) * HEAD_DIM] for h in SWA_HEAD_ORDER]
                              + [w_out[SWA_Q_W:]], axis=0).astype(BF16)
    w_up_b = w_up.astype(BF16)
    w_down_b = w_down.astype(BF16)

    proj = _inproj(x2, g_attn_norm.reshape(1, D), w_in_b)
    mix_a = _swa(proj, rel_bias, swa_sinks, g_a)
    mix_b = _sb(proj, g_sb_out.reshape(1, -1))
    h1, hn2 = _outproj(mix_a, mix_b, x2, g_mlp_norm.reshape(1, D), w_out_b)
    out = _convglu(hn2, h1, w_up_b, w_conv, b_conv.reshape(1, -1), w_down_b, g_final.reshape(1, D))
    return out.reshape(B, S, D)
```

```python
import functools
import math

import numpy as np
import jax
import jax.numpy as jnp
from jax import lax
from jax.experimental import pallas as pl
from jax.experimental.pallas import tpu as pltpu

D_MODEL = 2048
SEQ = 16384
HEAD_DIM = 64
SWA_Q_HEADS = 16
SWA_KV_HEADS = 2
SWA_GROUP = SWA_Q_HEADS // SWA_KV_HEADS
SB_HEADS = 16
WINDOW = 128
BLOCK = 128
REL_BUCKETS = 32
REL_MAX_DIST = 128
D_FF = 5632
CONV_WIDTH = 3
EPS = 1e-6
NEG_INF = -1e30

SWA_Q_W = SWA_Q_HEADS * HEAD_DIM
SWA_KV_W = SWA_KV_HEADS * HEAD_DIM
SB_W = SB_HEADS * HEAD_DIM
D_MIX = SWA_Q_W + SB_W
D_IN = SWA_Q_W + 2 * SWA_KV_W + 3 * SB_W

LANES = 128
REF_SPLITS = np.cumsum([0, SWA_Q_W, SWA_KV_W, SWA_KV_W, SB_W, SB_W, SB_W])
SWA_HEAD_ORDER = tuple(h for b in range(SWA_Q_HEADS // 2) for h in (b, SWA_Q_HEADS // 2 + b))
WIDE_QS, WIDE_KS, WIDE_VS, WIDE_QA = 0, 1, 2, 3
COL_KS = SB_W
COL_VS = 2 * SB_W
COL_KA = (3 * SB_W + SWA_Q_W) // LANES
COL_VA = COL_KA + SWA_KV_W // LANES
PAIRS = SB_W // LANES

SCALE = HEAD_DIM ** -0.5
HALO = 16
VMEM_LIMIT = 56 * 1024 * 1024
CONVGLU_VMEM_LIMIT = 62 * 1024 * 1024

F32 = jnp.float32
BF16 = jnp.bfloat16

PRUNE_LOG = -88.0
LOG2E = math.log2(math.e)
PRUNE_LOG2 = PRUNE_LOG * LOG2E


def _params(sem, vmem=VMEM_LIMIT):
    return pltpu.CompilerParams(dimension_semantics=sem, vmem_limit_bytes=vmem)


def _rms(y):
    return y * lax.rsqrt(jnp.mean(y * y, axis=-1, keepdims=True) + EPS)


def _inproj_kernel(x_ref, g_ref, w_ref, o_ref):
    hn = (_rms(x_ref[...]) * g_ref[...]).astype(BF16)
    o_ref[...] = jnp.dot(hn, w_ref[...], preferred_element_type=F32).astype(BF16)


def _resident(shape):
    return pl.BlockSpec(shape, lambda *_: (0,) * len(shape), pipeline_mode=pl.Buffered(1))


def _inproj(x, g, w_bf16, tm=512):
    S, D = x.shape
    N = w_bf16.shape[1]
    return pl.pallas_call(
        _inproj_kernel,
        out_shape=jax.ShapeDtypeStruct((S, N), BF16),
        grid=(S // tm,),
        in_specs=[
            pl.BlockSpec((tm, D), lambda i: (i, 0)),
            _resident((1, D)),
            _resident((D, N)),
        ],
        out_specs=pl.BlockSpec((tm, N), lambda i: (i, 0)),
        compiler_params=_params(("arbitrary",)),
        name="inproj",
    )(x, g, w_bf16)


def _rel_bucket_table():
    qi = np.arange(BLOCK, dtype=np.int64)[None, :]
    kj = np.arange(2 * BLOCK, dtype=np.int64)[:, None]
    dist = qi + BLOCK - kj
    in_win = (dist >= 0) & (dist < WINDOW)
    dc = np.clip(dist, 0, None)
    max_exact = REL_BUCKETS // 2
    d = np.maximum(dc, 1).astype(np.float32)
    large = max_exact + (np.log(d / np.float32(max_exact)) / np.float32(math.log(REL_MAX_DIST / max_exact))
                         * np.float32(REL_BUCKETS - max_exact)).astype(np.int32)
    large = np.minimum(large, REL_BUCKETS - 1)
    bucket = np.where(dc < max_exact, dc, large).astype(np.int32)
    return np.where(in_win, bucket, -1).astype(np.int32)


def _swa_kernel(q_ref, kp_ref, kc_ref, vp_ref, vc_ref, bucket_ref, relb_ref, sink_ref, g_ref,
                o_ref, bias_ref, yt_ref):
    n = pl.program_id(0)
    half = SWA_Q_HEADS // 2

    @pl.when(n == 0)
    def _():
        bucket = bucket_ref[...]
        krow = lax.broadcasted_iota(jnp.int32, bucket.shape, 0)
        for h in range(SWA_Q_HEADS):
            t = jnp.full(bucket.shape, NEG_INF, F32)
            for r in range(REL_BUCKETS):
                t = jnp.where(bucket == r, relb_ref[r, h], t)
            cols = slice((h // half) * BLOCK, (h // half + 1) * BLOCK)
            bias_ref[0, h % half, :, cols] = t
            bias_ref[1, h % half, :, cols] = jnp.where(krow >= BLOCK, t, NEG_INF)

    first_block = (n == 0).astype(jnp.int32)
    lane = lax.broadcasted_iota(jnp.int32, (BLOCK, LANES), 1)
    lo = lane < HEAD_DIM
    second = lax.broadcasted_iota(jnp.int32, (1, 2 * BLOCK), 1) >= BLOCK

    k2 = jnp.concatenate([kp_ref[...], kc_ref[...]], axis=0)
    v2t = jnp.concatenate([vp_ref[...], vc_ref[...]], axis=0).T

    blocks = range(half)
    logits, sinks = [], []
    for b in blocks:
        q = q_ref[:, b * LANES:(b + 1) * LANES]
        zq = jnp.zeros_like(q)
        qw = jnp.concatenate([jnp.where(lo, q, zq), jnp.where(lo, zq, q)], axis=0)
        st = lax.dot_general(k2, qw, (((1,), (1,)), ((), ())), preferred_element_type=F32)
        logits.append(st + bias_ref[first_block, b])
        sinks.append(jnp.where(second, sink_ref[half + b], sink_ref[b]))
    ms = [jnp.maximum(jnp.max(logits[b], axis=0, keepdims=True), sinks[b]) for b in blocks]
    ps = [jnp.exp(logits[b] - ms[b]) for b in blocks]
    invs = [1.0 / (jnp.sum(ps[b], axis=0, keepdims=True) + jnp.exp(sinks[b] - ms[b])) for b in blocks]
    for b in blocks:
        w = (ps[b] * invs[b]).astype(BF16)
        out = jnp.dot(v2t, w, preferred_element_type=F32)
        yt_ref[b * HEAD_DIM:(b + 1) * HEAD_DIM, :] = out[:HEAD_DIM, :BLOCK]
        yt_ref[(half + b) * HEAD_DIM:(half + b + 1) * HEAD_DIM, :] = out[HEAD_DIM:, BLOCK:]

    yt = yt_ref[...]
    inv = lax.rsqrt(jnp.mean(yt * yt, axis=0, keepdims=True) + EPS)
    o_ref[...] = ((yt * inv).T * g_ref[...]).astype(BF16)


def _swa(proj, rel_bias, sinks, g):
    S = proj.shape[0]
    N = S // BLOCK
    bucket = jnp.asarray(_rel_bucket_table())
    prev = lambda n: jnp.maximum(n - 1, 0)
    return pl.pallas_call(
        _swa_kernel,
        out_shape=jax.ShapeDtypeStruct((S, SWA_Q_W), BF16),
        grid=(N,),
        in_specs=[
            pl.BlockSpec((BLOCK, SWA_Q_W), lambda n: (n, WIDE_QA)),
            pl.BlockSpec((BLOCK, LANES), lambda n: (prev(n), COL_KA)),
            pl.BlockSpec((BLOCK, LANES), lambda n: (n, COL_KA)),
            pl.BlockSpec((BLOCK, LANES), lambda n: (prev(n), COL_VA)),
            pl.BlockSpec((BLOCK, LANES), lambda n: (n, COL_VA)),
            _resident((2 * BLOCK, BLOCK)),
            pl.BlockSpec(memory_space=pltpu.SMEM),
            pl.BlockSpec(memory_space=pltpu.SMEM),
            _resident((1, SWA_Q_W)),
        ],
        out_specs=pl.BlockSpec((BLOCK, SWA_Q_W), lambda n: (n, 0)),
        scratch_shapes=[pltpu.VMEM((2, SWA_Q_HEADS // 2, 2 * BLOCK, 2 * BLOCK), F32),
                        pltpu.VMEM((SWA_Q_W, BLOCK), F32)],
        compiler_params=_params(("arbitrary",)),
        name="swa",
    )(proj, proj, proj, proj, proj, bucket, rel_bias, sinks, g)


def _cumsum_weights():
    kk = np.arange(BLOCK)
    upper = (kk[:, None] > kk[None, :]).astype(np.float32)
    w = np.concatenate([upper, np.ones((BLOCK, BLOCK), np.float32)], axis=1)
    return np.concatenate([w, w], axis=0)


def _sb_chunks(items, w2):
    idx = range(len(items))
    nblk = items[0][1].shape[0] // BLOCK
    zs = [lax.dot_general(qq, kb, (((1,), (1,)), ((), ())), preferred_element_type=F32) * LOG2E
          for qq, kb, _, _, _ in items]
    lgs = [jnp.log2(1.0 + jnp.exp2(-jnp.abs(z))) for z in zs]
    logsigs = [jnp.minimum(zs[i], 0.0) - lgs[i] for i in idx]
    log1ms = [logsigs[i] - zs[i] for i in idx]
    runnings = [it[3] for it in items]
    parts = [[None] * nblk for _ in idx]
    for c in reversed(range(nblk)):
        cs = slice(c * BLOCK, (c + 1) * BLOCK)
        rs = []
        for i in idx:
            mask = items[i][4][c]
            l1 = log1ms[i][:, cs]
            if mask is not None:
                l1 = jnp.where(mask, l1, 0.0)
            hi = l1.astype(BF16)
            lo = (l1 - hi.astype(F32)).astype(BF16)
            rs.append(jnp.dot(jnp.concatenate([hi, lo], axis=1), w2, preferred_element_type=F32))
        for i in idx:
            mask = items[i][4][c]
            log_a = logsigs[i][:, cs] + rs[i][:, :BLOCK]
            if runnings[i] is not None:
                log_a = log_a + runnings[i]
            a = jnp.exp2(log_a)
            if mask is not None:
                a = jnp.where(mask, a, 0.0)
            parts[i][c] = a.astype(BF16)
            runnings[i] = rs[i][:, BLOCK:] if runnings[i] is None else runnings[i] + rs[i][:, BLOCK:]
    out = []
    for i in idx:
        amat = parts[i][0] if nblk == 1 else jnp.concatenate(parts[i], axis=1)
        out.append((runnings[i], jnp.dot(amat, items[i][2], preferred_element_type=F32)))
    return out


def _sb_kernel(q_ref, kp_ref, kc_ref, vp_ref, vc_ref, kp2_ref, vp2_ref, w2_ref, g_ref, proj_hbm,
               o_ref, qq_ref, kd_ref, vd_ref, acc_ref, oacc_ref, live_ref, sem):
    n = pl.program_id(0)
    lane = lax.broadcasted_iota(jnp.int32, (BLOCK, LANES), 1)
    first = lane < HEAD_DIM
    w2 = w2_ref[...]

    qrow = lax.broadcasted_iota(jnp.int32, (2 * BLOCK, BLOCK), 0) % BLOCK
    kcol = lax.broadcasted_iota(jnp.int32, (2 * BLOCK, BLOCK), 1)
    diag = kcol < qrow

    def phase1(with_prev):
        items = []
        for p in range(PAIRS):
            cols = slice(p * LANES, (p + 1) * LANES)
            q = q_ref[:, cols]
            zq = jnp.zeros_like(q)
            qq = jnp.concatenate([jnp.where(first, q, zq), jnp.where(first, zq, q)], axis=0)
            qq_ref[p] = qq
            kd_ref[p] = kp2_ref[:, cols]
            vd_ref[p] = vp2_ref[:, cols]
            if with_prev:
                kb = jnp.concatenate([kp_ref[:, cols], kc_ref[:, cols]], axis=0)
                vb = jnp.concatenate([vp_ref[:, cols], vc_ref[:, cols]], axis=0)
                masks = [None, diag]
            else:
                kb, vb, masks = kc_ref[:, cols], vc_ref[:, cols], [diag]
            items.append((qq, kb, vb, None, masks))
        for p, (acc, pv) in enumerate(_sb_chunks(items, w2)):
            acc_ref[p] = acc
            oacc_ref[p] = pv
            live_ref[p] = (jnp.max(acc) > PRUNE_LOG2).astype(jnp.int32)

    pl.when(n > 0)(functools.partial(phase1, True))
    pl.when(n == 0)(functools.partial(phase1, False))

    def pair_body(p, carry):
        def live():
            return (jnp.max(acc_ref[p]) > PRUNE_LOG2).astype(jnp.int32)

        def cond(c):
            j, go = c
            return jnp.logical_and(j >= 0, go > 0)

        def fetch(j, dst, col0, slot):
            src = proj_hbm.at[pl.ds(pl.multiple_of(j * BLOCK, BLOCK), BLOCK),
                              pl.ds(pl.multiple_of(col0 + p * LANES, LANES), LANES)]
            return pltpu.make_async_copy(src, dst.at[p], sem.at[slot])

        def body(c):
            j, _ = c

            @pl.when(j < n - 2)
            def _():
                ck = fetch(j, kd_ref, COL_KS, 0)
                cv = fetch(j, vd_ref, COL_VS, 1)
                ck.start()
                cv.start()
                ck.wait()
                cv.wait()

            (acc, pv), = _sb_chunks([(qq_ref[p], kd_ref[p], vd_ref[p], acc_ref[p], [None])], w2)
            acc_ref[p] = acc
            oacc_ref[p] += pv
            return j - 1, live()

        lax.while_loop(cond, body, (n - 2, live_ref[p]))
        return carry

    lax.fori_loop(0, PAIRS, pair_body, 0)

    ys = []
    ss = jnp.zeros((BLOCK, 1), F32)
    for p in range(PAIRS):
        y = jnp.where(first, oacc_ref[p, :BLOCK, :], oacc_ref[p, BLOCK:, :])
        ss = ss + jnp.sum(y * y, axis=-1, keepdims=True)
        ys.append(y)
    inv = lax.rsqrt(ss * (1.0 / SB_W) + EPS)
    for p in range(PAIRS):
        cols = slice(p * LANES, (p + 1) * LANES)
        o_ref[:, cols] = (ys[p] * inv * g_ref[:, cols]).astype(BF16)


def _sb(proj, g):
    S = proj.shape[0]
    N = S // BLOCK
    w2 = jnp.asarray(_cumsum_weights(), dtype=BF16)
    back = lambda d: (lambda n: jnp.maximum(n - d, 0))
    wide = lambda rowf, c: pl.BlockSpec((BLOCK, SB_W), lambda n: (rowf(n), c))
    return pl.pallas_call(
        _sb_kernel,
        out_shape=jax.ShapeDtypeStruct((S, SB_W), BF16),
        grid=(N,),
        in_specs=[
            wide(back(0), WIDE_QS),
            wide(back(1), WIDE_KS), wide(back(0), WIDE_KS),
            wide(back(1), WIDE_VS), wide(back(0), WIDE_VS),
            wide(back(2), WIDE_KS), wide(back(2), WIDE_VS),
            pl.BlockSpec((2 * BLOCK, 2 * BLOCK), lambda n: (0, 0)),
            pl.BlockSpec((1, SB_W), lambda n: (0, 0)),
            pl.BlockSpec(memory_space=pl.ANY),
        ],
        out_specs=pl.BlockSpec((BLOCK, SB_W), lambda n: (n, 0)),
        scratch_shapes=[
            pltpu.VMEM((PAIRS, 2 * BLOCK, LANES), BF16),
            pltpu.VMEM((PAIRS, BLOCK, LANES), BF16),
            pltpu.VMEM((PAIRS, BLOCK, LANES), BF16),
            pltpu.VMEM((PAIRS, 2 * BLOCK, BLOCK), F32),
            pltpu.VMEM((PAIRS, 2 * BLOCK, LANES), F32),
            pltpu.SMEM((PAIRS,), jnp.int32),
            pltpu.SemaphoreType.DMA((2,)),
        ],
        compiler_params=_params(("arbitrary",)),
        name="stickbreak",
    )(proj, proj, proj, proj, proj, proj, proj, w2, g, proj)


def _outproj_kernel(ma_ref, mb_ref, x_ref, gm_ref, w_ref, h_ref, hn_ref):
    mix = jnp.concatenate([ma_ref[...], mb_ref[...]], axis=-1)
    h = x_ref[...] + jnp.dot(mix, w_ref[...], preferred_element_type=F32)
    h_ref[...] = h
    hn_ref[...] = (_rms(h) * gm_ref[...]).astype(BF16)


def _outproj(ma, mb, x, gm, w_bf16, tm=512):
    S, D = x.shape
    row = lambda i: (i, 0)
    return pl.pallas_call(
        _outproj_kernel,
        out_shape=(jax.ShapeDtypeStruct((S, D), F32), jax.ShapeDtypeStruct((S, D), BF16)),
        grid=(S // tm,),
        in_specs=[
            pl.BlockSpec((tm, SWA_Q_W), row),
            pl.BlockSpec((tm, SB_W), row),
            pl.BlockSpec((tm, D), row),
            _resident((1, D)),
            _resident((D_MIX, D)),
        ],
        out_specs=(pl.BlockSpec((tm, D), row), pl.BlockSpec((tm, D), row)),
        compiler_params=_params(("arbitrary",)),
        name="outproj",
    )(ma, mb, x, gm, w_bf16)


def _convglu_kernel(hn_ref, halo_ref, wg_ref, wv_ref, wc_ref, bc_ref, wd_ref, h_ref, gf_ref,
                    o_ref, lhs_ref, gate_ref, *, tm):
    i = pl.program_id(0)
    f = pl.program_id(1)

    @pl.when(f == 0)
    def _():
        halo = halo_ref[...]
        lhs_ref[:HALO, :] = jnp.where(i > 0, halo, jnp.zeros_like(halo))
        lhs_ref[HALO:, :] = hn_ref[...]
        o_ref[...] = h_ref[...]

    gate_ref[...] = jnp.dot(lhs_ref[...], wg_ref[...], preferred_element_type=F32)
    val = jnp.dot(hn_ref[...], wv_ref[...], preferred_element_type=F32)
    gc = bc_ref[...]
    for tap in range(CONV_WIDTH):
        off = HALO - (CONV_WIDTH - 1) + tap
        gc = gc + gate_ref[pl.ds(off, tm), :] * wc_ref[tap:tap + 1, :]
    act = (gc * (1.0 / (1.0 + jnp.exp(-gc))) * val).astype(BF16)
    o_ref[...] += jnp.dot(act, wd_ref[...], preferred_element_type=F32)

    @pl.when(f == pl.num_programs(1) - 1)
    def _():
        o_ref[...] = _rms(o_ref[...]) * gf_ref[...]


def _convglu(hn2, h1, w_up_bf16, w_conv, b_conv, w_down_bf16, g_final, tm=1024, tf=512):
    S, D = h1.shape
    nf = D_FF // tf
    halo_blocks = tm // HALO
    return pl.pallas_call(
        functools.partial(_convglu_kernel, tm=tm),
        out_shape=jax.ShapeDtypeStruct((S, D), F32),
        grid=(S // tm, nf),
        in_specs=[
            pl.BlockSpec((tm, D), lambda i, f: (i, 0)),
            pl.BlockSpec((HALO, D), lambda i, f: (jnp.maximum(i * halo_blocks - 1, 0), 0)),
            pl.BlockSpec((D, tf), lambda i, f: (0, f)),
            pl.BlockSpec((D, tf), lambda i, f: (0, nf + f)),
            pl.BlockSpec((CONV_WIDTH, tf), lambda i, f: (0, f)),
            pl.BlockSpec((1, tf), lambda i, f: (0, f)),
            pl.BlockSpec((tf, D), lambda i, f: (f, 0)),
            pl.BlockSpec((tm, D), lambda i, f: (i, 0)),
            _resident((1, D)),
        ],
        out_specs=pl.BlockSpec((tm, D), lambda i, f: (i, 0)),
        scratch_shapes=[
            pltpu.VMEM((tm + HALO, D), BF16),
            pltpu.VMEM((tm + HALO, tf), F32),
        ],
        compiler_params=_params(("arbitrary", "arbitrary"), CONVGLU_VMEM_LIMIT),
        name="convglu",
    )(hn2, hn2, w_up_bf16, w_up_bf16, w_conv, b_conv, w_down_bf16, h1, g_final)


def kernel(x, w_in, g_attn_norm, rel_bias, swa_sinks, g_swa_out, g_sb_out, w_out,
           g_mlp_norm, w_up, w_conv, b_conv, w_down, g_final):
    B, S, D = x.shape
    assert (B, S, D) == (1, SEQ, D_MODEL)
    x2 = x.reshape(S, D)

    col = np.ones((1, D_IN), np.float32)
    col[:, REF_SPLITS[0]:REF_SPLITS[1]] = SCALE
    col[:, REF_SPLITS[3]:REF_SPLITS[4]] = SCALE
    w_in_s = (w_in * jnp.asarray(col)).astype(BF16)
    head_cols = lambda a, h: a[..., h * HEAD_DIM:(h + 1) * HEAD_DIM]
    qa_cols = [head_cols(w_in_s, h) for h in SWA_HEAD_ORDER]
    w_in_b = jnp.concatenate([w_in_s[:, REF_SPLITS[i]:REF_SPLITS[i + 1]] for i in (3, 4, 5)]
                             + qa_cols + [w_in_s[:, REF_SPLITS[1]:REF_SPLITS[3]]], axis=1)
    w_out_b = w_out.astype(BF16)
    w_up_b = w_up.astype(BF16)
    w_down_b = w_down.astype(BF16)

    proj = _inproj(x2, g_attn_norm.reshape(1, D), w_in_b)
    mix_a = _swa(proj, rel_bias, swa_sinks, g_swa_out.reshape(1, -1))
    mix_b = _sb(proj, g_sb_out.reshape(1, -1))
    h1, hn2 = _outproj(mix_a, mix_b, x2, g_mlp_norm.reshape(1, D), w_out_b)
    out = _convglu(hn2, h1, w_up_b, w_conv, b_conv.reshape(1, -1), w_down_b, g_final.reshape(1, D))
    return out.reshape(B, S, D)
```

```python
import functools
import math

import numpy as np
import jax
import jax.numpy as jnp
from jax import lax
from jax.experimental import pallas as pl
from jax.experimental.pallas import tpu as pltpu

D_MODEL = 2048
SEQ = 16384
HEAD_DIM = 64
SWA_Q_HEADS = 16
SWA_KV_HEADS = 2
SWA_GROUP = SWA_Q_HEADS // SWA_KV_HEADS
SB_HEADS = 16
WINDOW = 128
BLOCK = 128
REL_BUCKETS = 32
REL_MAX_DIST = 128
D_FF = 5632
CONV_WIDTH = 3
EPS = 1e-6
NEG_INF = -1e30

SWA_Q_W = SWA_Q_HEADS * HEAD_DIM
SWA_KV_W = SWA_KV_HEADS * HEAD_DIM
SB_W = SB_HEADS * HEAD_DIM
D_MIX = SWA_Q_W + SB_W
D_IN = SWA_Q_W + 2 * SWA_KV_W + 3 * SB_W

LANES = 128
REF_SPLITS = np.cumsum([0, SWA_Q_W, SWA_KV_W, SWA_KV_W, SB_W, SB_W, SB_W])
PERM_GROUPS = (3, 4, 5, 0, 1, 2)
WIDE_QS, WIDE_KS, WIDE_VS, WIDE_QA = 0, 1, 2, 3
COL_KS = SB_W
COL_VS = 2 * SB_W
COL_KA = (3 * SB_W + SWA_Q_W) // LANES
COL_VA = COL_KA + SWA_KV_W // LANES
PAIRS = SB_W // LANES
SWA_PAIRS = tuple((h, h + 2) for g in range(SWA_KV_HEADS) for par in (0, 1)
                  for h in range(g * SWA_GROUP + par, (g + 1) * SWA_GROUP, 4))

SCALE = HEAD_DIM ** -0.5
HALO = 16
VMEM_LIMIT = 56 * 1024 * 1024
CONVGLU_VMEM_LIMIT = 62 * 1024 * 1024

F32 = jnp.float32
BF16 = jnp.bfloat16

PRUNE_LOG = -88.0
LOG2E = math.log2(math.e)
PRUNE_LOG2 = PRUNE_LOG * LOG2E


def _params(sem, vmem=VMEM_LIMIT):
    return pltpu.CompilerParams(dimension_semantics=sem, vmem_limit_bytes=vmem)


def _rms(y):
    return y * lax.rsqrt(jnp.mean(y * y, axis=-1, keepdims=True) + EPS)


def _inproj_kernel(x_ref, g_ref, w_ref, o_ref):
    hn = (_rms(x_ref[...]) * g_ref[...]).astype(BF16)
    o_ref[...] = jnp.dot(hn, w_ref[...], preferred_element_type=F32).astype(BF16)


def _resident(shape):
    return pl.BlockSpec(shape, lambda *_: (0,) * len(shape), pipeline_mode=pl.Buffered(1))


def _inproj(x, g, w_bf16, tm=512):
    S, D = x.shape
    N = w_bf16.shape[1]
    return pl.pallas_call(
        _inproj_kernel,
        out_shape=jax.ShapeDtypeStruct((S, N), BF16),
        grid=(S // tm,),
        in_specs=[
            pl.BlockSpec((tm, D), lambda i: (i, 0)),
            _resident((1, D)),
            _resident((D, N)),
        ],
        out_specs=pl.BlockSpec((tm, N), lambda i: (i, 0)),
        compiler_params=_params(("arbitrary",)),
        name="inproj",
    )(x, g, w_bf16)


def _rel_bucket_table():
    qi = np.arange(BLOCK, dtype=np.int64)[None, :]
    kj = np.arange(2 * BLOCK, dtype=np.int64)[:, None]
    dist = qi + BLOCK - kj
    in_win = (dist >= 0) & (dist < WINDOW)
    dc = np.clip(dist, 0, None)
    max_exact = REL_BUCKETS // 2
    d = np.maximum(dc, 1).astype(np.float32)
    large = max_exact + (np.log(d / np.float32(max_exact)) / np.float32(math.log(REL_MAX_DIST / max_exact))
                         * np.float32(REL_BUCKETS - max_exact)).astype(np.int32)
    large = np.minimum(large, REL_BUCKETS - 1)
    bucket = np.where(dc < max_exact, dc, large).astype(np.int32)
    return np.where(in_win, bucket, -1).astype(np.int32)


def _swa_kernel(q_ref, kp_ref, kc_ref, vp_ref, vc_ref, bucket_ref, relb_ref, sink_ref, g_ref,
                o_ref, bias_ref, yt_ref):
    n = pl.program_id(0)

    @pl.when(n == 0)
    def _():
        bucket = bucket_ref[...]
        krow = lax.broadcasted_iota(jnp.int32, bucket.shape, 0)
        for i, pair in enumerate(SWA_PAIRS):
            for side, h in enumerate(pair):
                t = jnp.full(bucket.shape, NEG_INF, F32)
                for r in range(REL_BUCKETS):
                    t = jnp.where(bucket == r, relb_ref[r, h], t)
                cols = slice(side * BLOCK, (side + 1) * BLOCK)
                bias_ref[0, i, :, cols] = t
                bias_ref[1, i, :, cols] = jnp.where(krow >= BLOCK, t, NEG_INF)

    first_block = (n == 0).astype(jnp.int32)
    lane = lax.broadcasted_iota(jnp.int32, (BLOCK, LANES), 1)
    lo = lane < HEAD_DIM
    second = lax.broadcasted_iota(jnp.int32, (1, 2 * BLOCK), 1) >= BLOCK

    k2 = jnp.concatenate([kp_ref[...], kc_ref[...]], axis=0)
    k2s = jnp.concatenate([k2[:, HEAD_DIM:], k2[:, :HEAD_DIM]], axis=1)
    v2t = jnp.concatenate([vp_ref[...], vc_ref[...]], axis=0).T

    def masked_q(h):
        q = q_ref[:, (h // 2) * LANES:(h // 2 + 1) * LANES]
        keep = lo if h % 2 == 0 else jnp.logical_not(lo)
        return jnp.where(keep, q, jnp.zeros_like(q))

    idx = range(len(SWA_PAIRS))
    logits, sinks = [], []
    for i, (ha, hb) in enumerate(SWA_PAIRS):
        group = ha // SWA_GROUP
        keys = k2 if (ha % 2) == group else k2s
        qw = jnp.concatenate([masked_q(ha), masked_q(hb)], axis=0)
        st = lax.dot_general(keys, qw, (((1,), (1,)), ((), ())), preferred_element_type=F32)
        logits.append(st + bias_ref[first_block, i])
        sinks.append(jnp.where(second, sink_ref[hb], sink_ref[ha]))
    ms = [jnp.maximum(jnp.max(logits[i], axis=0, keepdims=True), sinks[i]) for i in idx]
    ps = [jnp.exp(logits[i] - ms[i]) for i in idx]
    invs = [1.0 / (jnp.sum(ps[i], axis=0, keepdims=True) + jnp.exp(sinks[i] - ms[i])) for i in idx]
    for i, (ha, hb) in enumerate(SWA_PAIRS):
        w = (ps[i] * invs[i]).astype(BF16)
        out = jnp.dot(v2t, w, preferred_element_type=F32)
        rows = slice((ha // SWA_GROUP) * HEAD_DIM, (ha // SWA_GROUP + 1) * HEAD_DIM)
        yt_ref[ha * HEAD_DIM:(ha + 1) * HEAD_DIM, :] = out[rows, :BLOCK]
        yt_ref[hb * HEAD_DIM:(hb + 1) * HEAD_DIM, :] = out[rows, BLOCK:]

    yt = yt_ref[...]
    inv = lax.rsqrt(jnp.mean(yt * yt, axis=0, keepdims=True) + EPS)
    o_ref[...] = ((yt * inv).T * g_ref[...]).astype(BF16)


def _swa(proj, rel_bias, sinks, g):
    S = proj.shape[0]
    N = S // BLOCK
    bucket = jnp.asarray(_rel_bucket_table())
    prev = lambda n: jnp.maximum(n - 1, 0)
    return pl.pallas_call(
        _swa_kernel,
        out_shape=jax.ShapeDtypeStruct((S, SWA_Q_W), BF16),
        grid=(N,),
        in_specs=[
            pl.BlockSpec((BLOCK, SWA_Q_W), lambda n: (n, WIDE_QA)),
            pl.BlockSpec((BLOCK, LANES), lambda n: (prev(n), COL_KA)),
            pl.BlockSpec((BLOCK, LANES), lambda n: (n, COL_KA)),
            pl.BlockSpec((BLOCK, LANES), lambda n: (prev(n), COL_VA)),
            pl.BlockSpec((BLOCK, LANES), lambda n: (n, COL_VA)),
            _resident((2 * BLOCK, BLOCK)),
            pl.BlockSpec(memory_space=pltpu.SMEM),
            pl.BlockSpec(memory_space=pltpu.SMEM),
            _resident((1, SWA_Q_W)),
        ],
        out_specs=pl.BlockSpec((BLOCK, SWA_Q_W), lambda n: (n, 0)),
        scratch_shapes=[pltpu.VMEM((2, SWA_Q_HEADS // 2, 2 * BLOCK, 2 * BLOCK), F32),
                        pltpu.VMEM((SWA_Q_W, BLOCK), F32)],
        compiler_params=_params(("arbitrary",)),
        name="swa",
    )(proj, proj, proj, proj, proj, bucket, rel_bias, sinks, g)


def _cumsum_weights():
    kk = np.arange(BLOCK)
    upper = (kk[:, None] > kk[None, :]).astype(np.float32)
    w = np.concatenate([upper, np.ones((BLOCK, BLOCK), np.float32)], axis=1)
    return np.concatenate([w, w], axis=0)


def _sb_chunks(items, w2):
    idx = range(len(items))
    nblk = items[0][1].shape[0] // BLOCK
    zs = [lax.dot_general(qq, kb, (((1,), (1,)), ((), ())), preferred_element_type=F32) * LOG2E
          for qq, kb, _, _, _ in items]
    lgs = [jnp.log2(1.0 + jnp.exp2(-jnp.abs(z))) for z in zs]
    logsigs = [jnp.minimum(zs[i], 0.0) - lgs[i] for i in idx]
    log1ms = [logsigs[i] - zs[i] for i in idx]
    runnings = [it[3] for it in items]
    parts = [[None] * nblk for _ in idx]
    for c in reversed(range(nblk)):
        cs = slice(c * BLOCK, (c + 1) * BLOCK)
        rs = []
        for i in idx:
            mask = items[i][4][c]
            l1 = log1ms[i][:, cs]
            if mask is not None:
                l1 = jnp.where(mask, l1, 0.0)
            hi = l1.astype(BF16)
            lo = (l1 - hi.astype(F32)).astype(BF16)
            rs.append(jnp.dot(jnp.concatenate([hi, lo], axis=1), w2, preferred_element_type=F32))
        for i in idx:
            mask = items[i][4][c]
            log_a = logsigs[i][:, cs] + rs[i][:, :BLOCK]
            if runnings[i] is not None:
                log_a = log_a + runnings[i]
            a = jnp.exp2(log_a)
            if mask is not None:
                a = jnp.where(mask, a, 0.0)
            parts[i][c] = a.astype(BF16)
            runnings[i] = rs[i][:, BLOCK:] if runnings[i] is None else runnings[i] + rs[i][:, BLOCK:]
    out = []
    for i in idx:
        amat = parts[i][0] if nblk == 1 else jnp.concatenate(parts[i], axis=1)
        out.append((runnings[i], jnp.dot(amat, items[i][2], preferred_element_type=F32)))
    return out


def _sb_kernel(q_ref, kp_ref, kc_ref, vp_ref, vc_ref, kp2_ref, vp2_ref, w2_ref, g_ref, proj_hbm,
               o_ref, qq_ref, kd_ref, vd_ref, acc_ref, oacc_ref, live_ref, sem):
    n = pl.program_id(0)
    lane = lax.broadcasted_iota(jnp.int32, (BLOCK, LANES), 1)
    first = lane < HEAD_DIM
    w2 = w2_ref[...]

    qrow = lax.broadcasted_iota(jnp.int32, (2 * BLOCK, BLOCK), 0) % BLOCK
    kcol = lax.broadcasted_iota(jnp.int32, (2 * BLOCK, BLOCK), 1)
    diag = kcol < qrow

    def phase1(with_prev):
        items = []
        for p in range(PAIRS):
            cols = slice(p * LANES, (p + 1) * LANES)
            q = q_ref[:, cols]
            zq = jnp.zeros_like(q)
            qq = jnp.concatenate([jnp.where(first, q, zq), jnp.where(first, zq, q)], axis=0)
            qq_ref[p] = qq
            kd_ref[p] = kp2_ref[:, cols]
            vd_ref[p] = vp2_ref[:, cols]
            if with_prev:
                kb = jnp.concatenate([kp_ref[:, cols], kc_ref[:, cols]], axis=0)
                vb = jnp.concatenate([vp_ref[:, cols], vc_ref[:, cols]], axis=0)
                masks = [None, diag]
            else:
                kb, vb, masks = kc_ref[:, cols], vc_ref[:, cols], [diag]
            items.append((qq, kb, vb, None, masks))
        for p, (acc, pv) in enumerate(_sb_chunks(items, w2)):
            acc_ref[p] = acc
            oacc_ref[p] = pv
            live_ref[p] = (jnp.max(acc) > PRUNE_LOG2).astype(jnp.int32)

    pl.when(n > 0)(functools.partial(phase1, True))
    pl.when(n == 0)(functools.partial(phase1, False))

    def pair_body(p, carry):
        def live():
            return (jnp.max(acc_ref[p]) > PRUNE_LOG2).astype(jnp.int32)

        def cond(c):
            j, go = c
            return jnp.logical_and(j >= 0, go > 0)

        def fetch(j, dst, col0, slot):
            src = proj_hbm.at[pl.ds(pl.multiple_of(j * BLOCK, BLOCK), BLOCK),
                              pl.ds(pl.multiple_of(col0 + p * LANES, LANES), LANES)]
            return pltpu.make_async_copy(src, dst.at[p], sem.at[slot])

        def body(c):
            j, _ = c

            @pl.when(j < n - 2)
            def _():
                ck = fetch(j, kd_ref, COL_KS, 0)
                cv = fetch(j, vd_ref, COL_VS, 1)
                ck.start()
                cv.start()
                ck.wait()
                cv.wait()

            (acc, pv), = _sb_chunks([(qq_ref[p], kd_ref[p], vd_ref[p], acc_ref[p], [None])], w2)
            acc_ref[p] = acc
            oacc_ref[p] += pv
            return j - 1, live()

        lax.while_loop(cond, body, (n - 2, live_ref[p]))
        return carry

    lax.fori_loop(0, PAIRS, pair_body, 0)

    ys = []
    ss = jnp.zeros((BLOCK, 1), F32)
    for p in range(PAIRS):
        y = jnp.where(first, oacc_ref[p, :BLOCK, :], oacc_ref[p, BLOCK:, :])
        ss = ss + jnp.sum(y * y, axis=-1, keepdims=True)
        ys.append(y)
    inv = lax.rsqrt(ss * (1.0 / SB_W) + EPS)
    for p in range(PAIRS):
        cols = slice(p * LANES, (p + 1) * LANES)
        o_ref[:, cols] = (ys[p] * inv * g_ref[:, cols]).astype(BF16)


def _sb(proj, g):
    S = proj.shape[0]
    N = S // BLOCK
    w2 = jnp.asarray(_cumsum_weights(), dtype=BF16)
    back = lambda d: (lambda n: jnp.maximum(n - d, 0))
    wide = lambda rowf, c: pl.BlockSpec((BLOCK, SB_W), lambda n: (rowf(n), c))
    return pl.pallas_call(
        _sb_kernel,
        out_shape=jax.ShapeDtypeStruct((S, SB_W), BF16),
        grid=(N,),
        in_specs=[
            wide(back(0), WIDE_QS),
            wide(back(1), WIDE_KS), wide(back(0), WIDE_KS),
            wide(back(1), WIDE_VS), wide(back(0), WIDE_VS),
            wide(back(2), WIDE_KS), wide(back(2), WIDE_VS),
            pl.BlockSpec((2 * BLOCK, 2 * BLOCK), lambda n: (0, 0)),
            pl.BlockSpec((1, SB_W), lambda n: (0, 0)),
            pl.BlockSpec(memory_space=pl.ANY),
        ],
        out_specs=pl.BlockSpec((BLOCK, SB_W), lambda n: (n, 0)),
        scratch_shapes=[
            pltpu.VMEM((PAIRS, 2 * BLOCK, LANES), BF16),
            pltpu.VMEM((PAIRS, BLOCK, LANES), BF16),
            pltpu.VMEM((PAIRS, BLOCK, LANES), BF16),
            pltpu.VMEM((PAIRS, 2 * BLOCK, BLOCK), F32),
            pltpu.VMEM((PAIRS, 2 * BLOCK, LANES), F32),
            pltpu.SMEM((PAIRS,), jnp.int32),
            pltpu.SemaphoreType.DMA((2,)),
        ],
        compiler_params=_params(("arbitrary",)),
        name="stickbreak",
    )(proj, proj, proj, proj, proj, proj, proj, w2, g, proj)


def _outproj_kernel(ma_ref, mb_ref, x_ref, gm_ref, w_ref, h_ref, hn_ref):
    mix = jnp.concatenate([ma_ref[...], mb_ref[...]], axis=-1)
    h = x_ref[...] + jnp.dot(mix, w_ref[...], preferred_element_type=F32)
    h_ref[...] = h
    hn_ref[...] = (_rms(h) * gm_ref[...]).astype(BF16)


def _outproj(ma, mb, x, gm, w_bf16, tm=512):
    S, D = x.shape
    row = lambda i: (i, 0)
    return pl.pallas_call(
        _outproj_kernel,
        out_shape=(jax.ShapeDtypeStruct((S, D), F32), jax.ShapeDtypeStruct((S, D), BF16)),
        grid=(S // tm,),
        in_specs=[
            pl.BlockSpec((tm, SWA_Q_W), row),
            pl.BlockSpec((tm, SB_W), row),
            pl.BlockSpec((tm, D), row),
            _resident((1, D)),
            _resident((D_MIX, D)),
        ],
        out_specs=(pl.BlockSpec((tm, D), row), pl.BlockSpec((tm, D), row)),
        compiler_params=_params(("arbitrary",)),
        name="outproj",
    )(ma, mb, x, gm, w_bf16)


def _convglu_kernel(hn_ref, halo_ref, wg_ref, wv_ref, wc_ref, bc_ref, wd_ref, h_ref, gf_ref,
                    o_ref, lhs_ref, gate_ref, *, tm):
    i = pl.program_id(0)
    f = pl.program_id(1)

    @pl.when(f == 0)
    def _():
        halo = halo_ref[...]
        lhs_ref[:HALO, :] = jnp.where(i > 0, halo, jnp.zeros_like(halo))
        lhs_ref[HALO:, :] = hn_ref[...]
        o_ref[...] = h_ref[...]

    gate_ref[...] = jnp.dot(lhs_ref[...], wg_ref[...], preferred_element_type=F32)
    val = jnp.dot(hn_ref[...], wv_ref[...], preferred_element_type=F32)
    gc = bc_ref[...]
    for tap in range(CONV_WIDTH):
        off = HALO - (CONV_WIDTH - 1) + tap
        gc = gc + gate_ref[pl.ds(off, tm), :] * wc_ref[tap:tap + 1, :]
    act = (gc * (1.0 / (1.0 + jnp.exp(-gc))) * val).astype(BF16)
    o_ref[...] += jnp.dot(act, wd_ref[...], preferred_element_type=F32)

    @pl.when(f == pl.num_programs(1) - 1)
    def _():
        o_ref[...] = _rms(o_ref[...]) * gf_ref[...]


def _convglu(hn2, h1, w_up_bf16, w_conv, b_conv, w_down_bf16, g_final, tm=1024, tf=512):
    S, D = h1.shape
    nf = D_FF // tf
    halo_blocks = tm // HALO
    return pl.pallas_call(
        functools.partial(_convglu_kernel, tm=tm),
        out_shape=jax.ShapeDtypeStruct((S, D), F32),
        grid=(S // tm, nf),
        in_specs=[
            pl.BlockSpec((tm, D), lambda i, f: (i, 0)),
            pl.BlockSpec((HALO, D), lambda i, f: (jnp.maximum(i * halo_blocks - 1, 0), 0)),
            pl.BlockSpec((D, tf), lambda i, f: (0, f)),
            pl.BlockSpec((D, tf), lambda i, f: (0, nf + f)),
            pl.BlockSpec((CONV_WIDTH, tf), lambda i, f: (0, f)),
            pl.BlockSpec((1, tf), lambda i, f: (0, f)),
            pl.BlockSpec((tf, D), lambda i, f: (f, 0)),
            pl.BlockSpec((tm, D), lambda i, f: (i, 0)),
            _resident((1, D)),
        ],
        out_specs=pl.BlockSpec((tm, D), lambda i, f: (i, 0)),
        scratch_shapes=[
            pltpu.VMEM((tm + HALO, D), BF16),
            pltpu.VMEM((tm + HALO, tf), F32),
        ],
        compiler_params=_params(("arbitrary", "arbitrary"), CONVGLU_VMEM_LIMIT),
        name="convglu",
    )(hn2, hn2, w_up_bf16, w_up_bf16, w_conv, b_conv, w_down_bf16, h1, g_final)


def kernel(x, w_in, g_attn_norm, rel_bias, swa_sinks, g_swa_out, g_sb_out, w_out,
           g_mlp_norm, w_up, w_conv, b_conv, w_down, g_final):
    B, S, D = x.shape
    assert (B, S, D) == (1, SEQ, D_MODEL)
    x2 = x.reshape(S, D)

    col = np.ones((1, D_IN), np.float32)
    col[:, REF_SPLITS[0]:REF_SPLITS[1]] = SCALE
    col[:, REF_SPLITS[3]:REF_SPLITS[4]] = SCALE
    w_in_s = (w_in * jnp.asarray(col)).astype(BF16)
    w_in_b = jnp.concatenate([w_in_s[:, REF_SPLITS[i]:REF_SPLITS[i + 1]] for i in PERM_GROUPS], axis=1)
    w_out_b = w_out.astype(BF16)
    w_up_b = w_up.astype(BF16)
    w_down_b = w_down.astype(BF16)

    proj = _inproj(x2, g_attn_norm.reshape(1, D), w_in_b)
    mix_a = _swa(proj, rel_bias, swa_sinks, g_swa_out.reshape(1, -1))
    mix_b = _sb(proj, g_sb_out.reshape(1, -1))
    h1, hn2 = _outproj(mix_a, mix_b, x2, g_mlp_norm.reshape(1, D), w_out_b)
    out = _convglu(hn2, h1, w_up_b, w_conv, b_conv.reshape(1, -1), w_down_b, g_final.reshape(1, D))
    return out.reshape(B, S, D)
```

```python
import functools
import math

import numpy as np
import jax
import jax.numpy as jnp
from jax import lax
from jax.experimental import pallas as pl
from jax.experimental.pallas import tpu as pltpu

D_MODEL = 2048
SEQ = 16384
HEAD_DIM = 64
SWA_Q_HEADS = 16
SWA_KV_HEADS = 2
SWA_GROUP = SWA_Q_HEADS // SWA_KV_HEADS
SB_HEADS = 16
WINDOW = 128
BLOCK = 128
REL_BUCKETS = 32
REL_MAX_DIST = 128
D_FF = 5632
CONV_WIDTH = 3
EPS = 1e-6
NEG_INF = -1e30

SWA_Q_W = SWA_Q_HEADS * HEAD_DIM
SWA_KV_W = SWA_KV_HEADS * HEAD_DIM
SB_W = SB_HEADS * HEAD_DIM
D_MIX = SWA_Q_W + SB_W
D_IN = SWA_Q_W + 2 * SWA_KV_W + 3 * SB_W

LANES = 128
REF_SPLITS = np.cumsum([0, SWA_Q_W, SWA_KV_W, SWA_KV_W, SB_W, SB_W, SB_W])
PERM_GROUPS = (3, 4, 5, 0, 1, 2)
WIDE_QS, WIDE_KS, WIDE_VS, WIDE_QA = 0, 1, 2, 3
COL_KS = SB_W
COL_VS = 2 * SB_W
COL_KA = (3 * SB_W + SWA_Q_W) // LANES
COL_VA = COL_KA + SWA_KV_W // LANES
PAIRS = SB_W // LANES
SWA_PAIRS = tuple((h, h + 2) for g in range(SWA_KV_HEADS) for par in (0, 1)
                  for h in range(g * SWA_GROUP + par, (g + 1) * SWA_GROUP, 4))

SCALE = HEAD_DIM ** -0.5
HALO = 16
VMEM_LIMIT = 56 * 1024 * 1024
CONVGLU_VMEM_LIMIT = 62 * 1024 * 1024

F32 = jnp.float32
BF16 = jnp.bfloat16

PRUNE_LOG = -88.0
LOG2E = math.log2(math.e)
PRUNE_LOG2 = PRUNE_LOG * LOG2E


def _params(sem, vmem=VMEM_LIMIT):
    return pltpu.CompilerParams(dimension_semantics=sem, vmem_limit_bytes=vmem)


def _rms(y):
    return y * lax.rsqrt(jnp.mean(y * y, axis=-1, keepdims=True) + EPS)


def _inproj_kernel(x_ref, g_ref, w_ref, o_ref):
    hn = (_rms(x_ref[...]) * g_ref[...]).astype(BF16)
    o_ref[...] = jnp.dot(hn, w_ref[...], preferred_element_type=F32).astype(BF16)


def _resident(shape):
    return pl.BlockSpec(shape, lambda *_: (0,) * len(shape), pipeline_mode=pl.Buffered(1))


def _inproj(x, g, w_bf16, tm=512):
    S, D = x.shape
    N = w_bf16.shape[1]
    return pl.pallas_call(
        _inproj_kernel,
        out_shape=jax.ShapeDtypeStruct((S, N), BF16),
        grid=(S // tm,),
        in_specs=[
            pl.BlockSpec((tm, D), lambda i: (i, 0)),
            _resident((1, D)),
            _resident((D, N)),
        ],
        out_specs=pl.BlockSpec((tm, N), lambda i: (i, 0)),
        compiler_params=_params(("arbitrary",)),
        name="inproj",
    )(x, g, w_bf16)


def _rel_bucket_table():
    qi = np.arange(BLOCK, dtype=np.int64)[None, :]
    kj = np.arange(2 * BLOCK, dtype=np.int64)[:, None]
    dist = qi + BLOCK - kj
    in_win = (dist >= 0) & (dist < WINDOW)
    dc = np.clip(dist, 0, None)
    max_exact = REL_BUCKETS // 2
    d = np.maximum(dc, 1).astype(np.float32)
    large = max_exact + (np.log(d / np.float32(max_exact)) / np.float32(math.log(REL_MAX_DIST / max_exact))
                         * np.float32(REL_BUCKETS - max_exact)).astype(np.int32)
    large = np.minimum(large, REL_BUCKETS - 1)
    bucket = np.where(dc < max_exact, dc, large).astype(np.int32)
    return np.where(in_win, bucket, -1).astype(np.int32)


def _cumsum_weights():
    kk = np.arange(BLOCK)
    upper = (kk[:, None] > kk[None, :]).astype(np.float32)
    w = np.concatenate([upper, np.ones((BLOCK, BLOCK), np.float32)], axis=1)
    return np.concatenate([w, w], axis=0)


def _interleave(*stage_generators):
    pending = list(stage_generators)
    while pending:
        for g in list(pending):
            try:
                next(g)
            except StopIteration:
                pending.remove(g)


def _swa_stages(q_ref, kp_ref, kc_ref, vp_ref, vc_ref, bias_ref, variant, sink_ref, g_ref, yt_ref, o_ref):
    lane = lax.broadcasted_iota(jnp.int32, (BLOCK, LANES), 1)
    lo = lane < HEAD_DIM
    second = lax.broadcasted_iota(jnp.int32, (1, 2 * BLOCK), 1) >= BLOCK

    k2 = jnp.concatenate([kp_ref[...], kc_ref[...]], axis=0)
    k2s = jnp.concatenate([k2[:, HEAD_DIM:], k2[:, :HEAD_DIM]], axis=1)
    v2t = jnp.concatenate([vp_ref[...], vc_ref[...]], axis=0).T

    def masked_q(h):
        q = q_ref[:, (h // 2) * LANES:(h // 2 + 1) * LANES]
        keep = lo if h % 2 == 0 else jnp.logical_not(lo)
        return jnp.where(keep, q, jnp.zeros_like(q))

    idx = range(len(SWA_PAIRS))
    logits, sinks = [], []
    for i, (ha, hb) in enumerate(SWA_PAIRS):
        group = ha // SWA_GROUP
        keys = k2 if (ha % 2) == group else k2s
        qw = jnp.concatenate([masked_q(ha), masked_q(hb)], axis=0)
        st = lax.dot_general(keys, qw, (((1,), (1,)), ((), ())), preferred_element_type=F32)
        logits.append(st + bias_ref[variant, i])
        sinks.append(jnp.where(second, sink_ref[hb], sink_ref[ha]))
    yield
    ms = [jnp.maximum(jnp.max(logits[i], axis=0, keepdims=True), sinks[i]) for i in idx]
    ps = [jnp.exp(logits[i] - ms[i]) for i in idx]
    yield
    invs = [1.0 / (jnp.sum(ps[i], axis=0, keepdims=True) + jnp.exp(sinks[i] - ms[i])) for i in idx]
    yield
    for i, (ha, hb) in enumerate(SWA_PAIRS):
        w = (ps[i] * invs[i]).astype(BF16)
        out = jnp.dot(v2t, w, preferred_element_type=F32)
        rows = slice((ha // SWA_GROUP) * HEAD_DIM, (ha // SWA_GROUP + 1) * HEAD_DIM)
        yt_ref[ha * HEAD_DIM:(ha + 1) * HEAD_DIM, :] = out[rows, :BLOCK]
        yt_ref[hb * HEAD_DIM:(hb + 1) * HEAD_DIM, :] = out[rows, BLOCK:]
    yield
    yt = yt_ref[...]
    inv = lax.rsqrt(jnp.mean(yt * yt, axis=0, keepdims=True) + EPS)
    o_ref[:, :SWA_Q_W] = ((yt * inv).T * g_ref[...]).astype(BF16)


def _sb_stages(items, w2, results):
    idx = range(len(items))
    nblk = items[0][1].shape[0] // BLOCK
    zs = [lax.dot_general(qq, kb, (((1,), (1,)), ((), ())), preferred_element_type=F32) * LOG2E
          for qq, kb, _, _, _ in items]
    yield
    lgs = [jnp.log2(1.0 + jnp.exp2(-jnp.abs(z))) for z in zs]
    logsigs = [jnp.minimum(zs[i], 0.0) - lgs[i] for i in idx]
    log1ms = [logsigs[i] - zs[i] for i in idx]
    yield
    runnings = [it[3] for it in items]
    parts = [[None] * nblk for _ in idx]
    for c in reversed(range(nblk)):
        cs = slice(c * BLOCK, (c + 1) * BLOCK)
        rs = []
        for i in idx:
            mask = items[i][4][c]
            l1 = log1ms[i][:, cs]
            if mask is not None:
                l1 = jnp.where(mask, l1, 0.0)
            hi = l1.astype(BF16)
            lo = (l1 - hi.astype(F32)).astype(BF16)
            rs.append(jnp.dot(jnp.concatenate([hi, lo], axis=1), w2, preferred_element_type=F32))
        yield
        for i in idx:
            mask = items[i][4][c]
            log_a = logsigs[i][:, cs] + rs[i][:, :BLOCK]
            if runnings[i] is not None:
                log_a = log_a + runnings[i]
            a = jnp.exp2(log_a)
            if mask is not None:
                a = jnp.where(mask, a, 0.0)
            parts[i][c] = a.astype(BF16)
            runnings[i] = rs[i][:, BLOCK:] if runnings[i] is None else runnings[i] + rs[i][:, BLOCK:]
        yield
    for i in idx:
        amat = parts[i][0] if nblk == 1 else jnp.concatenate(parts[i], axis=1)
        results.append((runnings[i], jnp.dot(amat, items[i][2], preferred_element_type=F32)))


def _attn_kernel(qa_ref, kap_ref, kac_ref, vap_ref, vac_ref, bucket_ref, relb_ref, sink_ref, ga_ref,
                 q_ref, kp_ref, kc_ref, vp_ref, vc_ref, kp2_ref, vp2_ref, w2_ref, gb_ref, proj_hbm,
                 o_ref, bias_ref, yt_ref, qq_ref, kd_ref, vd_ref, acc_ref, oacc_ref, live_ref, sem):
    n = pl.program_id(0)

    @pl.when(n == 0)
    def _():
        bucket = bucket_ref[...]
        krow = lax.broadcasted_iota(jnp.int32, bucket.shape, 0)
        for i, pair in enumerate(SWA_PAIRS):
            for side, h in enumerate(pair):
                t = jnp.full(bucket.shape, NEG_INF, F32)
                for r in range(REL_BUCKETS):
                    t = jnp.where(bucket == r, relb_ref[r, h], t)
                cols = slice(side * BLOCK, (side + 1) * BLOCK)
                bias_ref[0, i, :, cols] = t
                bias_ref[1, i, :, cols] = jnp.where(krow >= BLOCK, t, NEG_INF)

    lane = lax.broadcasted_iota(jnp.int32, (BLOCK, LANES), 1)
    first = lane < HEAD_DIM
    w2 = w2_ref[...]
    qrow = lax.broadcasted_iota(jnp.int32, (2 * BLOCK, BLOCK), 0) % BLOCK
    kcol = lax.broadcasted_iota(jnp.int32, (2 * BLOCK, BLOCK), 1)
    diag = kcol < qrow

    def phase1(with_prev):
        items = []
        for p in range(PAIRS):
            cols = slice(p * LANES, (p + 1) * LANES)
            q = q_ref[:, cols]
            zq = jnp.zeros_like(q)
            qq = jnp.concatenate([jnp.where(first, q, zq), jnp.where(first, zq, q)], axis=0)
            qq_ref[p] = qq
            kd_ref[p] = kp2_ref[:, cols]
            vd_ref[p] = vp2_ref[:, cols]
            if with_prev:
                kb = jnp.concatenate([kp_ref[:, cols], kc_ref[:, cols]], axis=0)
                vb = jnp.concatenate([vp_ref[:, cols], vc_ref[:, cols]], axis=0)
                masks = [None, diag]
            else:
                kb, vb, masks = kc_ref[:, cols], vc_ref[:, cols], [diag]
            items.append((qq, kb, vb, None, masks))
        results = []
        _interleave(
            _swa_stages(qa_ref, kap_ref, kac_ref, vap_ref, vac_ref, bias_ref, 0 if with_prev else 1,
                        sink_ref, ga_ref, yt_ref, o_ref),
            _sb_stages(items, w2, results))
        for p, (acc, pv) in enumerate(results):
            acc_ref[p] = acc
            oacc_ref[p] = pv
            live_ref[p] = (jnp.max(acc) > PRUNE_LOG2).astype(jnp.int32)

    pl.when(n > 0)(functools.partial(phase1, True))
    pl.when(n == 0)(functools.partial(phase1, False))

    def pair_body(p, carry):
        def live():
            return (jnp.max(acc_ref[p]) > PRUNE_LOG2).astype(jnp.int32)

        def cond(c):
            j, go = c
            return jnp.logical_and(j >= 0, go > 0)

        def fetch(j, dst, col0, slot):
            src = proj_hbm.at[pl.ds(pl.multiple_of(j * BLOCK, BLOCK), BLOCK),
                              pl.ds(pl.multiple_of(col0 + p * LANES, LANES), LANES)]
            return pltpu.make_async_copy(src, dst.at[p], sem.at[slot])

        def body(c):
            j, _ = c

            @pl.when(j < n - 2)
            def _():
                ck = fetch(j, kd_ref, COL_KS, 0)
                cv = fetch(j, vd_ref, COL_VS, 1)
                ck.start()
                cv.start()
                ck.wait()
                cv.wait()

            results = []
            _interleave(_sb_stages([(qq_ref[p], kd_ref[p], vd_ref[p], acc_ref[p], [None])], w2, results))
            (acc, pv), = results
            acc_ref[p] = acc
            oacc_ref[p] += pv
            return j - 1, live()

        lax.while_loop(cond, body, (n - 2, live_ref[p]))
        return carry

    lax.fori_loop(0, PAIRS, pair_body, 0)

    ys = []
    ss = jnp.zeros((BLOCK, 1), F32)
    for p in range(PAIRS):
        y = jnp.where(first, oacc_ref[p, :BLOCK, :], oacc_ref[p, BLOCK:, :])
        ss = ss + jnp.sum(y * y, axis=-1, keepdims=True)
        ys.append(y)
    inv = lax.rsqrt(ss * (1.0 / SB_W) + EPS)
    for p in range(PAIRS):
        cols = slice(p * LANES, (p + 1) * LANES)
        o_ref[:, SWA_Q_W + p * LANES:SWA_Q_W + (p + 1) * LANES] = (ys[p] * inv * gb_ref[:, cols]).astype(BF16)


def _attention(proj, rel_bias, sinks, g_a, g_b):
    S = proj.shape[0]
    N = S // BLOCK
    bucket = jnp.asarray(_rel_bucket_table())
    w2 = jnp.asarray(_cumsum_weights(), dtype=BF16)
    back = lambda d: (lambda n: jnp.maximum(n - d, 0))
    wide = lambda rowf, c: pl.BlockSpec((BLOCK, SB_W), lambda n: (rowf(n), c))
    narrow = lambda rowf, c: pl.BlockSpec((BLOCK, LANES), lambda n: (rowf(n), c))
    smem = pl.BlockSpec(memory_space=pltpu.SMEM)
    return pl.pallas_call(
        _attn_kernel,
        out_shape=jax.ShapeDtypeStruct((S, D_MIX), BF16),
        grid=(N,),
        in_specs=[
            wide(back(0), WIDE_QA),
            narrow(back(1), COL_KA), narrow(back(0), COL_KA),
            narrow(back(1), COL_VA), narrow(back(0), COL_VA),
            _resident((2 * BLOCK, BLOCK)), smem, smem, _resident((1, SWA_Q_W)),
            wide(back(0), WIDE_QS),
            wide(back(1), WIDE_KS), wide(back(0), WIDE_KS),
            wide(back(1), WIDE_VS), wide(back(0), WIDE_VS),
            wide(back(2), WIDE_KS), wide(back(2), WIDE_VS),
            _resident((2 * BLOCK, 2 * BLOCK)), _resident((1, SB_W)),
            pl.BlockSpec(memory_space=pl.ANY),
        ],
        out_specs=pl.BlockSpec((BLOCK, D_MIX), lambda n: (n, 0)),
        scratch_shapes=[
            pltpu.VMEM((2, len(SWA_PAIRS), 2 * BLOCK, 2 * BLOCK), F32),
            pltpu.VMEM((SWA_Q_W, BLOCK), F32),
            pltpu.VMEM((PAIRS, 2 * BLOCK, LANES), BF16),
            pltpu.VMEM((PAIRS, BLOCK, LANES), BF16),
            pltpu.VMEM((PAIRS, BLOCK, LANES), BF16),
            pltpu.VMEM((PAIRS, 2 * BLOCK, BLOCK), F32),
            pltpu.VMEM((PAIRS, 2 * BLOCK, LANES), F32),
            pltpu.SMEM((PAIRS,), jnp.int32),
            pltpu.SemaphoreType.DMA((2,)),
        ],
        compiler_params=_params(("arbitrary",)),
        name="attention",
    )(proj, proj, proj, proj, proj, bucket, rel_bias, sinks, g_a,
      proj, proj, proj, proj, proj, proj, proj, w2, g_b, proj)


def _outproj_kernel(mix_ref, x_ref, gm_ref, w_ref, h_ref, hn_ref):
    h = x_ref[...] + jnp.dot(mix_ref[...], w_ref[...], preferred_element_type=F32)
    h_ref[...] = h
    hn_ref[...] = (_rms(h) * gm_ref[...]).astype(BF16)


def _outproj(mix, x, gm, w_bf16, tm=512):
    S, D = x.shape
    row = lambda i: (i, 0)
    return pl.pallas_call(
        _outproj_kernel,
        out_shape=(jax.ShapeDtypeStruct((S, D), F32), jax.ShapeDtypeStruct((S, D), BF16)),
        grid=(S // tm,),
        in_specs=[
            pl.BlockSpec((tm, D_MIX), row),
            pl.BlockSpec((tm, D), row),
            _resident((1, D)),
            _resident((D_MIX, D)),
        ],
        out_specs=(pl.BlockSpec((tm, D), row), pl.BlockSpec((tm, D), row)),
        compiler_params=_params(("arbitrary",)),
        name="outproj",
    )(mix, x, gm, w_bf16)


def _convglu_kernel(hn_ref, halo_ref, wg_ref, wv_ref, wc_ref, bc_ref, wd_ref, h_ref, gf_ref,
                    o_ref, lhs_ref, gate_ref, *, tm):
    i = pl.program_id(0)
    f = pl.program_id(1)

    @pl.when(f == 0)
    def _():
        halo = halo_ref[...]
        lhs_ref[:HALO, :] = jnp.where(i > 0, halo, jnp.zeros_like(halo))
        lhs_ref[HALO:, :] = hn_ref[...]
        o_ref[...] = h_ref[...]

    gate_ref[...] = jnp.dot(lhs_ref[...], wg_ref[...], preferred_element_type=F32)
    val = jnp.dot(hn_ref[...], wv_ref[...], preferred_element_type=F32)
    gc = bc_ref[...]
    for tap in range(CONV_WIDTH):
        off = HALO - (CONV_WIDTH - 1) + tap
        gc = gc + gate_ref[pl.ds(off, tm), :] * wc_ref[tap:tap + 1, :]
    act = (gc * (1.0 / (1.0 + jnp.exp(-gc))) * val).astype(BF16)
    o_ref[...] += jnp.dot(act, wd_ref[...], preferred_element_type=F32)

    @pl.when(f == pl.num_programs(1) - 1)
    def _():
        o_ref[...] = _rms(o_ref[...]) * gf_ref[...]


def _convglu(hn2, h1, w_up_bf16, w_conv, b_conv, w_down_bf16, g_final, tm=1024, tf=512):
    S, D = h1.shape
    nf = D_FF // tf
    halo_blocks = tm // HALO
    return pl.pallas_call(
        functools.partial(_convglu_kernel, tm=tm),
        out_shape=jax.ShapeDtypeStruct((S, D), F32),
        grid=(S // tm, nf),
        in_specs=[
            pl.BlockSpec((tm, D), lambda i, f: (i, 0)),
            pl.BlockSpec((HALO, D), lambda i, f: (jnp.maximum(i * halo_blocks - 1, 0), 0)),
            pl.BlockSpec((D, tf), lambda i, f: (0, f)),
            pl.BlockSpec((D, tf), lambda i, f: (0, nf + f)),
            pl.BlockSpec((CONV_WIDTH, tf), lambda i, f: (0, f)),
            pl.BlockSpec((1, tf), lambda i, f: (0, f)),
            pl.BlockSpec((tf, D), lambda i, f: (f, 0)),
            pl.BlockSpec((tm, D), lambda i, f: (i, 0)),
            _resident((1, D)),
        ],
        out_specs=pl.BlockSpec((tm, D), lambda i, f: (i, 0)),
        scratch_shapes=[
            pltpu.VMEM((tm + HALO, D), BF16),
            pltpu.VMEM((tm + HALO, tf), F32),
        ],
        compiler_params=_params(("arbitrary", "arbitrary"), CONVGLU_VMEM_LIMIT),
        name="convglu",
    )(hn2, hn2, w_up_bf16, w_up_bf16, w_conv, b_conv, w_down_bf16, h1, g_final)


def kernel(x, w_in, g_attn_norm, rel_bias, swa_sinks, g_swa_out, g_sb_out, w_out,
           g_mlp_norm, w_up, w_conv, b_conv, w_down, g_final):
    B, S, D = x.shape
    assert (B, S, D) == (1, SEQ, D_MODEL)
    x2 = x.reshape(S, D)

    col = np.ones((1, D_IN), np.float32)
    col[:, REF_SPLITS[0]:REF_SPLITS[1]] = SCALE
    col[:, REF_SPLITS[3]:REF_SPLITS[4]] = SCALE
    w_in_s = (w_in * jnp.asarray(col)).astype(BF16)
    w_in_b = jnp.concatenate([w_in_s[:, REF_SPLITS[i]:REF_SPLITS[i + 1]] for i in PERM_GROUPS], axis=1)
    w_out_b = w_out.astype(BF16)
    w_up_b = w_up.astype(BF16)
    w_down_b = w_down.astype(BF16)

    proj = _inproj(x2, g_attn_norm.reshape(1, D), w_in_b)
    mix = _attention(proj, rel_bias, swa_sinks, g_swa_out.reshape(1, -1), g_sb_out.reshape(1, -1))
    h1, hn2 = _outproj(mix, x2, g_mlp_norm.reshape(1, D), w_out_b)
    out = _convglu(hn2, h1, w_up_b, w_conv, b_conv.reshape(1, -1), w_down_b, g_final.reshape(1, D))
    return out.reshape(B, S, D)
```

```python
import functools
import itertools
import math

import numpy as np
import jax
import jax.numpy as jnp
from jax import lax
from jax.experimental import pallas as pl
from jax.experimental.pallas import tpu as pltpu

D_MODEL = 2048
SEQ = 16384
HEAD_DIM = 64
SWA_Q_HEADS = 16
SWA_KV_HEADS = 2
SWA_GROUP = SWA_Q_HEADS // SWA_KV_HEADS
SB_HEADS = 16
WINDOW = 128
BLOCK = 128
REL_BUCKETS = 32
REL_MAX_DIST = 128
D_FF = 5632
CONV_WIDTH = 3
EPS = 1e-6
NEG_INF = -1e30

SWA_Q_W = SWA_Q_HEADS * HEAD_DIM
SWA_KV_W = SWA_KV_HEADS * HEAD_DIM
SB_W = SB_HEADS * HEAD_DIM
D_MIX = SWA_Q_W + SB_W
D_IN = SWA_Q_W + 2 * SWA_KV_W + 3 * SB_W

LANES = 128
REF_SPLITS = np.cumsum([0, SWA_Q_W, SWA_KV_W, SWA_KV_W, SB_W, SB_W, SB_W])
PERM_GROUPS = (3, 4, 5, 0, 1, 2)
WIDE_QS, WIDE_KS, WIDE_VS, WIDE_QA = 0, 1, 2, 3
COL_KS = SB_W
COL_VS = 2 * SB_W
COL_KA = (3 * SB_W + SWA_Q_W) // LANES
COL_VA = COL_KA + SWA_KV_W // LANES
PAIRS = SB_W // LANES
SWA_PAIRS = tuple((h, h + 2) for g in range(SWA_KV_HEADS) for par in (0, 1)
                  for h in range(g * SWA_GROUP + par, (g + 1) * SWA_GROUP, 4))

SCALE = HEAD_DIM ** -0.5
HALO = 16

V7X_VMEM_BYTES = 64 * 1024 * 1024
VMEM_LIMIT = V7X_VMEM_BYTES - 8 * 1024 * 1024
INPROJ_TM = 512
OUTPROJ_TM = 512
CONVGLU_TM = 1024
CONVGLU_TF = 512
CONVGLU_VMEM_LIMIT = V7X_VMEM_BYTES - 2 * 1024 * 1024

F32 = jnp.float32
BF16 = jnp.bfloat16

PRUNE_LOG = -88.0
LOG2E = math.log2(math.e)
PRUNE_LOG2 = PRUNE_LOG * LOG2E


def _params(sem, vmem=VMEM_LIMIT):
    return pltpu.CompilerParams(dimension_semantics=sem, vmem_limit_bytes=vmem)


def _rms(y):
    return y * lax.rsqrt(jnp.mean(y * y, axis=-1, keepdims=True) + EPS)


def _inproj_kernel(n_later, x_ref, g_ref, w_ref, *refs):
    srcs, o_ref, dsts = refs[:n_later], refs[n_later], refs[n_later + 1:]
    hn = (_rms(x_ref[...]) * g_ref[...]).astype(BF16)
    o_ref[...] = jnp.dot(hn, w_ref[...], preferred_element_type=F32).astype(BF16)
    for src, dst in zip(srcs, dsts):
        dst[...] = src[...].astype(BF16)


def _resident(shape):
    return pl.BlockSpec(shape, lambda *_: (0,) * len(shape), pipeline_mode=pl.Buffered(1))


def _inproj(x, g, w_bf16, later_weights, tm=INPROJ_TM):
    S, D = x.shape
    N = w_bf16.shape[1]
    steps = S // tm
    slab = lambda w: pl.BlockSpec((w.shape[0] // steps, w.shape[1]), lambda i: (i, 0))
    outs = pl.pallas_call(
        functools.partial(_inproj_kernel, len(later_weights)),
        out_shape=[jax.ShapeDtypeStruct((S, N), BF16)]
        + [jax.ShapeDtypeStruct(w.shape, BF16) for w in later_weights],
        grid=(steps,),
        in_specs=[
            pl.BlockSpec((tm, D), lambda i: (i, 0)),
            _resident((1, D)),
            _resident((D, N)),
        ] + [slab(w) for w in later_weights],
        out_specs=[pl.BlockSpec((tm, N), lambda i: (i, 0))] + [slab(w) for w in later_weights],
        compiler_params=_params(("arbitrary",)),
        name="inproj",
    )(x, g, w_bf16, *later_weights)
    return outs[0], outs[1:]


def _rel_bucket_table():
    qi = np.arange(BLOCK, dtype=np.int64)[None, :]
    kj = np.arange(2 * BLOCK, dtype=np.int64)[:, None]
    dist = qi + BLOCK - kj
    in_win = (dist >= 0) & (dist < WINDOW)
    dc = np.clip(dist, 0, None)
    max_exact = REL_BUCKETS // 2
    d = np.maximum(dc, 1).astype(np.float32)
    large = max_exact + (np.log(d / np.float32(max_exact)) / np.float32(math.log(REL_MAX_DIST / max_exact))
                         * np.float32(REL_BUCKETS - max_exact)).astype(np.int32)
    large = np.minimum(large, REL_BUCKETS - 1)
    bucket = np.where(dc < max_exact, dc, large).astype(np.int32)
    return np.where(in_win, bucket, -1).astype(np.int32)


def _cumsum_weights():
    kk = np.arange(BLOCK)
    upper = (kk[:, None] > kk[None, :]).astype(np.float32)
    w = np.concatenate([upper, np.ones((BLOCK, BLOCK), np.float32)], axis=1)
    return np.concatenate([w, w], axis=0)


def _interleave(*stage_generators):
    pending = list(stage_generators)
    while pending:
        for g in list(pending):
            try:
                next(g)
            except StopIteration:
                pending.remove(g)


def _swa_stages(q_ref, kp_ref, kc_ref, vp_ref, vc_ref, bias_ref, variant, sink_ref, g_ref, yt_ref, o_ref):
    lane = lax.broadcasted_iota(jnp.int32, (BLOCK, LANES), 1)
    lo = lane < HEAD_DIM
    second = lax.broadcasted_iota(jnp.int32, (1, 2 * BLOCK), 1) >= BLOCK

    k2 = jnp.concatenate([kp_ref[...], kc_ref[...]], axis=0)
    k2s = jnp.concatenate([k2[:, HEAD_DIM:], k2[:, :HEAD_DIM]], axis=1)
    v2t = jnp.concatenate([vp_ref[...], vc_ref[...]], axis=0).T

    def masked_q(h):
        q = q_ref[:, (h // 2) * LANES:(h // 2 + 1) * LANES]
        keep = lo if h % 2 == 0 else jnp.logical_not(lo)
        return jnp.where(keep, q, jnp.zeros_like(q))

    idx = range(len(SWA_PAIRS))
    logits, sinks = [], []
    for i, (ha, hb) in enumerate(SWA_PAIRS):
        group = ha // SWA_GROUP
        keys = k2 if (ha % 2) == group else k2s
        qw = jnp.concatenate([masked_q(ha), masked_q(hb)], axis=0)
        st = lax.dot_general(keys, qw, (((1,), (1,)), ((), ())), preferred_element_type=F32)
        logits.append(st + bias_ref[variant, i])
        sinks.append(jnp.where(second, sink_ref[hb], sink_ref[ha]))
    yield
    ms = [jnp.maximum(jnp.max(logits[i], axis=0, keepdims=True), sinks[i]) for i in idx]
    ps = [jnp.exp(logits[i] - ms[i]) for i in idx]
    yield
    invs = [1.0 / (jnp.sum(ps[i], axis=0, keepdims=True) + jnp.exp(sinks[i] - ms[i])) for i in idx]
    yield
    for i, (ha, hb) in enumerate(SWA_PAIRS):
        w = (ps[i] * invs[i]).astype(BF16)
        out = jnp.dot(v2t, w, preferred_element_type=F32)
        rows = slice((ha // SWA_GROUP) * HEAD_DIM, (ha // SWA_GROUP + 1) * HEAD_DIM)
        yt_ref[ha * HEAD_DIM:(ha + 1) * HEAD_DIM, :] = out[rows, :BLOCK]
        yt_ref[hb * HEAD_DIM:(hb + 1) * HEAD_DIM, :] = out[rows, BLOCK:]
    yield
    yt = yt_ref[...]
    inv = lax.rsqrt(jnp.mean(yt * yt, axis=0, keepdims=True) + EPS)
    o_ref[:, :SWA_Q_W] = ((yt * inv).T * g_ref[...]).astype(BF16)


def _sb_stages(items, w2, results):
    idx = range(len(items))
    nblk = items[0][1].shape[0] // BLOCK
    zs = [lax.dot_general(qq, kb, (((1,), (1,)), ((), ())), preferred_element_type=F32) * LOG2E
          for qq, kb, _, _, _ in items]
    yield
    lgs = [jnp.log2(1.0 + jnp.exp2(-jnp.abs(z))) for z in zs]
    logsigs = [jnp.minimum(zs[i], 0.0) - lgs[i] for i in idx]
    log1ms = [logsigs[i] - zs[i] for i in idx]
    yield
    runnings = [it[3] for it in items]
    parts = [[None] * nblk for _ in idx]
    for c in reversed(range(nblk)):
        cs = slice(c * BLOCK, (c + 1) * BLOCK)
        rs = []
        for i in idx:
            mask = items[i][4][c]
            l1 = log1ms[i][:, cs]
            if mask is not None:
                l1 = jnp.where(mask, l1, 0.0)
            hi = l1.astype(BF16)
            lo = (l1 - hi.astype(F32)).astype(BF16)
            rs.append(jnp.dot(jnp.concatenate([hi, lo], axis=1), w2, preferred_element_type=F32))
        yield
        for i in idx:
            mask = items[i][4][c]
            log_a = logsigs[i][:, cs] + rs[i][:, :BLOCK]
            if runnings[i] is not None:
                log_a = log_a + runnings[i]
            a = jnp.exp2(log_a)
            if mask is not None:
                a = jnp.where(mask, a, 0.0)
            parts[i][c] = a.astype(BF16)
            runnings[i] = rs[i][:, BLOCK:] if runnings[i] is None else runnings[i] + rs[i][:, BLOCK:]
        yield
    for i in idx:
        amat = parts[i][0] if nblk == 1 else jnp.concatenate(parts[i], axis=1)
        results.append((runnings[i], jnp.dot(amat, items[i][2], preferred_element_type=F32)))


def _attn_kernel(qa_ref, kap_ref, kac_ref, vap_ref, vac_ref, bucket_ref, relb_ref, sink_ref, ga_ref,
                 q_ref, kp_ref, kc_ref, vp_ref, vc_ref, kp2_ref, vp2_ref, w2_ref, gb_ref, proj_hbm,
                 o_ref, bias_ref, yt_ref, qq_ref, kd_ref, vd_ref, acc_ref, oacc_ref, live_ref, sem):
    n = pl.program_id(0)

    @pl.when(n == 0)
    def _():
        bucket = bucket_ref[...]
        krow = lax.broadcasted_iota(jnp.int32, bucket.shape, 0)
        for i, pair in enumerate(SWA_PAIRS):
            for side, h in enumerate(pair):
                t = jnp.full(bucket.shape, NEG_INF, F32)
                for r in range(REL_BUCKETS):
                    t = jnp.where(bucket == r, relb_ref[r, h], t)
                cols = slice(side * BLOCK, (side + 1) * BLOCK)
                bias_ref[0, i, :, cols] = t
                bias_ref[1, i, :, cols] = jnp.where(krow >= BLOCK, t, NEG_INF)

    lane = lax.broadcasted_iota(jnp.int32, (BLOCK, LANES), 1)
    first = lane < HEAD_DIM
    w2 = w2_ref[...]
    qrow = lax.broadcasted_iota(jnp.int32, (2 * BLOCK, BLOCK), 0) % BLOCK
    kcol = lax.broadcasted_iota(jnp.int32, (2 * BLOCK, BLOCK), 1)
    diag = kcol < qrow

    def phase1(with_prev):
        items = []
        for p in range(PAIRS):
            cols = slice(p * LANES, (p + 1) * LANES)
            q = q_ref[:, cols]
            zq = jnp.zeros_like(q)
            qq = jnp.concatenate([jnp.where(first, q, zq), jnp.where(first, zq, q)], axis=0)
            qq_ref[p] = qq
            kd_ref[p] = kp2_ref[:, cols]
            vd_ref[p] = vp2_ref[:, cols]
            if with_prev:
                kb = jnp.concatenate([kp_ref[:, cols], kc_ref[:, cols]], axis=0)
                vb = jnp.concatenate([vp_ref[:, cols], vc_ref[:, cols]], axis=0)
                masks = [None, diag]
            else:
                kb, vb, masks = kc_ref[:, cols], vc_ref[:, cols], [diag]
            items.append((qq, kb, vb, None, masks))
        results = []
        group = PAIRS // 2
        _interleave(
            itertools.chain(*[_sb_stages(items[i:i + group], w2, results) for i in range(0, PAIRS, group)]),
            _swa_stages(qa_ref, kap_ref, kac_ref, vap_ref, vac_ref, bias_ref, 0 if with_prev else 1,
                        sink_ref, ga_ref, yt_ref, o_ref))
        for p, (acc, pv) in enumerate(results):
            acc_ref[p] = acc
            oacc_ref[p] = pv
            live_ref[p] = (jnp.max(acc) > PRUNE_LOG2).astype(jnp.int32)

    pl.when(n > 0)(functools.partial(phase1, True))
    pl.when(n == 0)(functools.partial(phase1, False))

    def pair_body(p, carry):
        def live():
            return (jnp.max(acc_ref[p]) > PRUNE_LOG2).astype(jnp.int32)

        def cond(c):
            j, go = c
            return jnp.logical_and(j >= 0, go > 0)

        def fetch(j, dst, col0, slot):
            src = proj_hbm.at[pl.ds(pl.multiple_of(j * BLOCK, BLOCK), BLOCK),
                              pl.ds(pl.multiple_of(col0 + p * LANES, LANES), LANES)]
            return pltpu.make_async_copy(src, dst.at[p], sem.at[slot])

        def body(c):
            j, _ = c

            @pl.when(j < n - 2)
            def _():
                ck = fetch(j, kd_ref, COL_KS, 0)
                cv = fetch(j, vd_ref, COL_VS, 1)
                ck.start()
                cv.start()
                ck.wait()
                cv.wait()

            results = []
            _interleave(_sb_stages([(qq_ref[p], kd_ref[p], vd_ref[p], acc_ref[p], [None])], w2, results))
            (acc, pv), = results
            acc_ref[p] = acc
            oacc_ref[p] += pv
            return j - 1, live()

        lax.while_loop(cond, body, (n - 2, live_ref[p]))
        return carry

    lax.fori_loop(0, PAIRS, pair_body, 0)

    ys = [jnp.where(first, oacc_ref[p, :BLOCK, :], oacc_ref[p, BLOCK:, :]) for p in range(PAIRS)]
    sq = functools.reduce(lambda a, b: a + b, [y * y for y in ys])
    inv = lax.rsqrt(jnp.sum(sq, axis=-1, keepdims=True) * (1.0 / SB_W) + EPS)
    for p in range(PAIRS):
        cols = slice(p * LANES, (p + 1) * LANES)
        o_ref[:, SWA_Q_W + p * LANES:SWA_Q_W + (p + 1) * LANES] = (ys[p] * inv * gb_ref[:, cols]).astype(BF16)


def _attention(proj, rel_bias, sinks, g_a, g_b):
    S = proj.shape[0]
    N = S // BLOCK
    bucket = jnp.asarray(_rel_bucket_table())
    w2 = jnp.asarray(_cumsum_weights(), dtype=BF16)
    back = lambda d: (lambda n: jnp.maximum(n - d, 0))
    wide = lambda rowf, c: pl.BlockSpec((BLOCK, SB_W), lambda n: (rowf(n), c))
    narrow = lambda rowf, c: pl.BlockSpec((BLOCK, LANES), lambda n: (rowf(n), c))
    smem = pl.BlockSpec(memory_space=pltpu.SMEM)
    return pl.pallas_call(
        _attn_kernel,
        out_shape=jax.ShapeDtypeStruct((S, D_MIX), BF16),
        grid=(N,),
        in_specs=[
            wide(back(0), WIDE_QA),
            narrow(back(1), COL_KA), narrow(back(0), COL_KA),
            narrow(back(1), COL_VA), narrow(back(0), COL_VA),
            _resident((2 * BLOCK, BLOCK)), smem, smem, _resident((1, SWA_Q_W)),
            wide(back(0), WIDE_QS),
            wide(back(1), WIDE_KS), wide(back(0), WIDE_KS),
            wide(back(1), WIDE_VS), wide(back(0), WIDE_VS),
            wide(back(2), WIDE_KS), wide(back(2), WIDE_VS),
            _resident((2 * BLOCK, 2 * BLOCK)), _resident((1, SB_W)),
            pl.BlockSpec(memory_space=pl.ANY),
        ],
        out_specs=pl.BlockSpec((BLOCK, D_MIX), lambda n: (n, 0)),
        scratch_shapes=[
            pltpu.VMEM((2, len(SWA_PAIRS), 2 * BLOCK, 2 * BLOCK), F32),
            pltpu.VMEM((SWA_Q_W, BLOCK), F32),
            pltpu.VMEM((PAIRS, 2 * BLOCK, LANES), BF16),
            pltpu.VMEM((PAIRS, BLOCK, LANES), BF16),
            pltpu.VMEM((PAIRS, BLOCK, LANES), BF16),
            pltpu.VMEM((PAIRS, 2 * BLOCK, BLOCK), F32),
            pltpu.VMEM((PAIRS, 2 * BLOCK, LANES), F32),
            pltpu.SMEM((PAIRS,), jnp.int32),
            pltpu.SemaphoreType.DMA((2,)),
        ],
        compiler_params=_params(("arbitrary",)),
        name="attention",
    )(proj, proj, proj, proj, proj, bucket, rel_bias, sinks, g_a,
      proj, proj, proj, proj, proj, proj, proj, w2, g_b, proj)


def _outproj_kernel(mix_ref, x_ref, gm_ref, w_ref, h_ref, hn_ref):
    h = x_ref[...] + jnp.dot(mix_ref[...], w_ref[...], preferred_element_type=F32)
    h_ref[...] = h
    hn_ref[...] = (_rms(h) * gm_ref[...]).astype(BF16)


def _outproj(mix, x, gm, w_bf16, tm=OUTPROJ_TM):
    S, D = x.shape
    row = lambda i: (i, 0)
    return pl.pallas_call(
        _outproj_kernel,
        out_shape=(jax.ShapeDtypeStruct((S, D), F32), jax.ShapeDtypeStruct((S, D), BF16)),
        grid=(S // tm,),
        in_specs=[
            pl.BlockSpec((tm, D_MIX), row),
            pl.BlockSpec((tm, D), row),
            _resident((1, D)),
            _resident((D_MIX, D)),
        ],
        out_specs=(pl.BlockSpec((tm, D), row), pl.BlockSpec((tm, D), row)),
        compiler_params=_params(("arbitrary",)),
        name="outproj",
    )(mix, x, gm, w_bf16)


def _convglu_kernel(hn_ref, halo_ref, wg_ref, wv_ref, wc_ref, bc_ref, wd_ref, h_ref, gf_ref,
                    o_ref, lhs_ref, gate_ref, *, tm):
    i = pl.program_id(0)
    f = pl.program_id(1)

    @pl.when(f == 0)
    def _():
        halo = halo_ref[...]
        lhs_ref[:HALO, :] = jnp.where(i > 0, halo, jnp.zeros_like(halo))
        lhs_ref[HALO:, :] = hn_ref[...]
        o_ref[...] = h_ref[...]

    gate_ref[...] = jnp.dot(lhs_ref[...], wg_ref[...], preferred_element_type=F32)
    val = jnp.dot(hn_ref[...], wv_ref[...], preferred_element_type=F32)
    gc = bc_ref[...]
    for tap in range(CONV_WIDTH):
        off = HALO - (CONV_WIDTH - 1) + tap
        gc = gc + gate_ref[pl.ds(off, tm), :] * wc_ref[tap:tap + 1, :]
    act = (gc * (1.0 / (1.0 + jnp.exp(-gc))) * val).astype(BF16)
    o_ref[...] += jnp.dot(act, wd_ref[...], preferred_element_type=F32)

    @pl.when(f == pl.num_programs(1) - 1)
    def _():
        o_ref[...] = _rms(o_ref[...]) * gf_ref[...]


def _convglu(hn2, h1, w_up_bf16, w_conv, b_conv, w_down_bf16, g_final, tm=CONVGLU_TM, tf=CONVGLU_TF):
    S, D = h1.shape
    nf = D_FF // tf
    halo_blocks = tm // HALO
    return pl.pallas_call(
        functools.partial(_convglu_kernel, tm=tm),
        out_shape=jax.ShapeDtypeStruct((S, D), F32),
        grid=(S // tm, nf),
        in_specs=[
            pl.BlockSpec((tm, D), lambda i, f: (i, 0)),
            pl.BlockSpec((HALO, D), lambda i, f: (jnp.maximum(i * halo_blocks - 1, 0), 0)),
            pl.BlockSpec((D, tf), lambda i, f: (0, f)),
            pl.BlockSpec((D, tf), lambda i, f: (0, nf + f)),
            pl.BlockSpec((CONV_WIDTH, tf), lambda i, f: (0, f)),
            pl.BlockSpec((1, tf), lambda i, f: (0, f)),
            pl.BlockSpec((tf, D), lambda i, f: (f, 0)),
            pl.BlockSpec((tm, D), lambda i, f: (i, 0)),
            _resident((1, D)),
        ],
        out_specs=pl.BlockSpec((tm, D), lambda i, f: (i, 0)),
        scratch_shapes=[
            pltpu.VMEM((tm + HALO, D), BF16),
            pltpu.VMEM((tm + HALO, tf), F32),
        ],
        compiler_params=_params(("arbitrary", "arbitrary"), CONVGLU_VMEM_LIMIT),
        name="convglu",
    )(hn2, hn2, w_up_bf16, w_up_bf16, w_conv, b_conv, w_down_bf16, h1, g_final)


def kernel(x, w_in, g_attn_norm, rel_bias, swa_sinks, g_swa_out, g_sb_out, w_out,
           g_mlp_norm, w_up, w_conv, b_conv, w_down, g_final):
    B, S, D = x.shape
    assert (B, S, D) == (1, SEQ, D_MODEL)
    x2 = x.reshape(S, D)

    col = np.ones((1, D_IN), np.float32)
    col[:, REF_SPLITS[0]:REF_SPLITS[1]] = SCALE
    col[:, REF_SPLITS[3]:REF_SPLITS[4]] = SCALE
    w_in_s = (w_in * jnp.asarray(col)).astype(BF16)
    w_in_b = jnp.concatenate([w_in_s[:, REF_SPLITS[i]:REF_SPLITS[i + 1]] for i in PERM_GROUPS], axis=1)

    proj, (w_out_b, w_up_b, w_down_b) = _inproj(x2, g_attn_norm.reshape(1, D), w_in_b, (w_out, w_up, w_down))
    mix = _attention(proj, rel_bias, swa_sinks, g_swa_out.reshape(1, -1), g_sb_out.reshape(1, -1))
    h1, hn2 = _outproj(mix, x2, g_mlp_norm.reshape(1, D), w_out_b)
    out = _convglu(hn2, h1, w_up_b, w_conv, b_conv.reshape(1, -1), w_down_b, g_final.reshape(1, D))
    return out.reshape(B, S, D)
```

```python
import functools
import math

import numpy as np
import jax
import jax.numpy as jnp
from jax import lax
from jax.experimental import pallas as pl
from jax.experimental.pallas import tpu as pltpu

D_MODEL = 2048
SEQ = 16384
HEAD_DIM = 64
SWA_Q_HEADS = 16
SWA_KV_HEADS = 2
SWA_GROUP = SWA_Q_HEADS // SWA_KV_HEADS
SB_HEADS = 16
WINDOW = 128
BLOCK = 128
REL_BUCKETS = 32
REL_MAX_DIST = 128
D_FF = 5632
CONV_WIDTH = 3
EPS = 1e-6
NEG_INF = -1e30

SWA_Q_W = SWA_Q_HEADS * HEAD_DIM
SWA_KV_W = SWA_KV_HEADS * HEAD_DIM
SB_W = SB_HEADS * HEAD_DIM
D_MIX = SWA_Q_W + SB_W
D_IN = SWA_Q_W + 2 * SWA_KV_W + 3 * SB_W

LANES = 128
REF_SPLITS = np.cumsum([0, SWA_Q_W, SWA_KV_W, SWA_KV_W, SB_W, SB_W, SB_W])
PERM_GROUPS = (3, 4, 5, 0, 1, 2)
WIDE_QS, WIDE_KS, WIDE_VS, WIDE_QA = 0, 1, 2, 3
COL_KS = SB_W
COL_VS = 2 * SB_W
COL_KA = (3 * SB_W + SWA_Q_W) // LANES
COL_VA = COL_KA + SWA_KV_W // LANES
PAIRS = SB_W // LANES
SWA_PAIRS = tuple((h, h + 2) for g in range(SWA_KV_HEADS) for par in (0, 1)
                  for h in range(g * SWA_GROUP + par, (g + 1) * SWA_GROUP, 4))

SCALE = HEAD_DIM ** -0.5
HALO = 16

V7X_VMEM_BYTES = 64 * 1024 * 1024
VMEM_LIMIT = V7X_VMEM_BYTES - 8 * 1024 * 1024
INPROJ_TM = 512
OUTPROJ_TM = 512
CONVGLU_TM = 1024
CONVGLU_TF = 512
CONVGLU_VMEM_LIMIT = V7X_VMEM_BYTES - 2 * 1024 * 1024

F32 = jnp.float32
BF16 = jnp.bfloat16

PRUNE_LOG = -88.0
LOG2E = math.log2(math.e)
PRUNE_LOG2 = PRUNE_LOG * LOG2E


def _params(sem, vmem=VMEM_LIMIT):
    return pltpu.CompilerParams(dimension_semantics=sem, vmem_limit_bytes=vmem)


def _rms(y):
    return y * lax.rsqrt(jnp.mean(y * y, axis=-1, keepdims=True) + EPS)


def _inproj_kernel(n_later, x_ref, g_ref, w_ref, *refs):
    srcs, o_ref, dsts = refs[:n_later], refs[n_later], refs[n_later + 1:]
    hn = (_rms(x_ref[...]) * g_ref[...]).astype(BF16)
    o_ref[...] = jnp.dot(hn, w_ref[...], preferred_element_type=F32).astype(BF16)
    for src, dst in zip(srcs, dsts):
        if len(dst.shape) == 2:
            dst[...] = src[...].astype(BF16)
        else:
            width = dst.shape[-1]
            for c in range(dst.shape[0]):
                dst[c] = src[:, c * width:(c + 1) * width].astype(BF16)


def _resident(shape):
    return pl.BlockSpec(shape, lambda *_: (0,) * len(shape), pipeline_mode=pl.Buffered(1))


def _inproj(x, g, w_bf16, later_weights, tm=INPROJ_TM):
    S, D = x.shape
    N = w_bf16.shape[1]
    steps = S // tm
    weights = [w for w, _ in later_weights]
    in_slabs = [pl.BlockSpec((w.shape[0] // steps, w.shape[1]), lambda i: (i, 0)) for w in weights]
    out_shapes, out_slabs = [], []
    for w, width in later_weights:
        rows, cols = w.shape
        if width is None:
            out_shapes.append(jax.ShapeDtypeStruct((rows, cols), BF16))
            out_slabs.append(pl.BlockSpec((rows // steps, cols), lambda i: (i, 0)))
        else:
            out_shapes.append(jax.ShapeDtypeStruct((cols // width, rows, width), BF16))
            out_slabs.append(pl.BlockSpec((cols // width, rows // steps, width), lambda i: (0, i, 0)))
    outs = pl.pallas_call(
        functools.partial(_inproj_kernel, len(weights)),
        out_shape=[jax.ShapeDtypeStruct((S, N), BF16)] + out_shapes,
        grid=(steps,),
        in_specs=[
            pl.BlockSpec((tm, D), lambda i: (i, 0)),
            _resident((1, D)),
            _resident((D, N)),
        ] + in_slabs,
        out_specs=[pl.BlockSpec((tm, N), lambda i: (i, 0))] + out_slabs,
        compiler_params=_params(("arbitrary",)),
        name="inproj",
    )(x, g, w_bf16, *weights)
    return outs[0], outs[1:]


def _rel_bucket_table():
    qi = np.arange(BLOCK, dtype=np.int64)[None, :]
    kj = np.arange(2 * BLOCK, dtype=np.int64)[:, None]
    dist = qi + BLOCK - kj
    in_win = (dist >= 0) & (dist < WINDOW)
    dc = np.clip(dist, 0, None)
    max_exact = REL_BUCKETS // 2
    d = np.maximum(dc, 1).astype(np.float32)
    large = max_exact + (np.log(d / np.float32(max_exact)) / np.float32(math.log(REL_MAX_DIST / max_exact))
                         * np.float32(REL_BUCKETS - max_exact)).astype(np.int32)
    large = np.minimum(large, REL_BUCKETS - 1)
    bucket = np.where(dc < max_exact, dc, large).astype(np.int32)
    return np.where(in_win, bucket, -1).astype(np.int32)


def _cumsum_weights():
    kk = np.arange(BLOCK)
    upper = (kk[:, None] > kk[None, :]).astype(np.float32)
    w = np.concatenate([upper, np.ones((BLOCK, BLOCK), np.float32)], axis=1)
    return np.concatenate([w, w], axis=0)


def _interleave(*stage_generators):
    pending = list(stage_generators)
    while pending:
        for g in list(pending):
            try:
                next(g)
            except StopIteration:
                pending.remove(g)


def _swa_stages(q_ref, kp_ref, kc_ref, vp_ref, vc_ref, bias_ref, variant, sink_ref, g_ref, yt_ref, o_ref):
    lane = lax.broadcasted_iota(jnp.int32, (BLOCK, LANES), 1)
    lo = lane < HEAD_DIM
    second = lax.broadcasted_iota(jnp.int32, (1, 2 * BLOCK), 1) >= BLOCK

    k2 = jnp.concatenate([kp_ref[...], kc_ref[...]], axis=0)
    k2s = jnp.concatenate([k2[:, HEAD_DIM:], k2[:, :HEAD_DIM]], axis=1)
    v2t = jnp.concatenate([vp_ref[...], vc_ref[...]], axis=0).T

    def masked_q(h):
        q = q_ref[:, (h // 2) * LANES:(h // 2 + 1) * LANES]
        keep = lo if h % 2 == 0 else jnp.logical_not(lo)
        return jnp.where(keep, q, jnp.zeros_like(q))

    idx = range(len(SWA_PAIRS))
    logits, sinks = [], []
    for i, (ha, hb) in enumerate(SWA_PAIRS):
        group = ha // SWA_GROUP
        keys = k2 if (ha % 2) == group else k2s
        qw = jnp.concatenate([masked_q(ha), masked_q(hb)], axis=0)
        st = lax.dot_general(keys, qw, (((1,), (1,)), ((), ())), preferred_element_type=F32)
        logits.append(st + bias_ref[variant, i])
        sinks.append(jnp.where(second, sink_ref[hb], sink_ref[ha]))
    yield
    ms = [jnp.maximum(jnp.max(logits[i], axis=0, keepdims=True), sinks[i]) for i in idx]
    ps = [jnp.exp(logits[i] - ms[i]) for i in idx]
    yield
    invs = [1.0 / (jnp.sum(ps[i], axis=0, keepdims=True) + jnp.exp(sinks[i] - ms[i])) for i in idx]
    yield
    for i, (ha, hb) in enumerate(SWA_PAIRS):
        w = (ps[i] * invs[i]).astype(BF16)
        out = jnp.dot(v2t, w, preferred_element_type=F32)
        rows = slice((ha // SWA_GROUP) * HEAD_DIM, (ha // SWA_GROUP + 1) * HEAD_DIM)
        yt_ref[ha * HEAD_DIM:(ha + 1) * HEAD_DIM, :] = out[rows, :BLOCK]
        yt_ref[hb * HEAD_DIM:(hb + 1) * HEAD_DIM, :] = out[rows, BLOCK:]
    yield
    yt = yt_ref[...]
    inv = lax.rsqrt(jnp.mean(yt * yt, axis=0, keepdims=True) + EPS)
    o_ref[:, :SWA_Q_W] = ((yt * inv).T * g_ref[...]).astype(BF16)


def _sb_stages(items, w2, results):
    idx = range(len(items))
    nblk = items[0][1].shape[0] // BLOCK
    zs = [lax.dot_general(qq, kb, (((1,), (1,)), ((), ())), preferred_element_type=F32) * LOG2E
          for qq, kb, _, _, _ in items]
    yield
    lgs = [jnp.log2(1.0 + jnp.exp2(-jnp.abs(z))) for z in zs]
    logsigs = [jnp.minimum(zs[i], 0.0) - lgs[i] for i in idx]
    log1ms = [logsigs[i] - zs[i] for i in idx]
    yield
    runnings = [it[3] for it in items]
    parts = [[None] * nblk for _ in idx]
    for c in reversed(range(nblk)):
        cs = slice(c * BLOCK, (c + 1) * BLOCK)
        rs = []
        for i in idx:
            mask = items[i][4][c]
            l1 = log1ms[i][:, cs]
            if mask is not None:
                l1 = jnp.where(mask, l1, 0.0)
            hi = l1.astype(BF16)
            lo = (l1 - hi.astype(F32)).astype(BF16)
            rs.append(jnp.dot(jnp.concatenate([hi, lo], axis=1), w2, preferred_element_type=F32))
        yield
        for i in idx:
            mask = items[i][4][c]
            log_a = logsigs[i][:, cs] + rs[i][:, :BLOCK]
            if runnings[i] is not None:
                log_a = log_a + runnings[i]
            a = jnp.exp2(log_a)
            if mask is not None:
                a = jnp.where(mask, a, 0.0)
            parts[i][c] = a.astype(BF16)
            runnings[i] = rs[i][:, BLOCK:] if runnings[i] is None else runnings[i] + rs[i][:, BLOCK:]
        yield
    for i in idx:
        amat = parts[i][0] if nblk == 1 else jnp.concatenate(parts[i], axis=1)
        results.append((runnings[i], jnp.dot(amat, items[i][2], preferred_element_type=F32)))


def _attn_kernel(qa_ref, kap_ref, kac_ref, vap_ref, vac_ref, bucket_ref, relb_ref, sink_ref, ga_ref,
                 q_ref, kp_ref, kc_ref, vp_ref, vc_ref, kp2_ref, vp2_ref, w2_ref, gb_ref, proj_hbm,
                 o_ref, bias_ref, yt_ref, qq_ref, kd_ref, vd_ref, acc_ref, oacc_ref, live_ref, sem):
    n = pl.program_id(0)

    @pl.when(n == 0)
    def _():
        bucket = bucket_ref[...]
        krow = lax.broadcasted_iota(jnp.int32, bucket.shape, 0)
        for i, pair in enumerate(SWA_PAIRS):
            for side, h in enumerate(pair):
                t = jnp.full(bucket.shape, NEG_INF, F32)
                for r in range(REL_BUCKETS):
                    t = jnp.where(bucket == r, relb_ref[r, h], t)
                cols = slice(side * BLOCK, (side + 1) * BLOCK)
                bias_ref[0, i, :, cols] = t
                bias_ref[1, i, :, cols] = jnp.where(krow >= BLOCK, t, NEG_INF)

    lane = lax.broadcasted_iota(jnp.int32, (BLOCK, LANES), 1)
    first = lane < HEAD_DIM
    w2 = w2_ref[...]
    qrow = lax.broadcasted_iota(jnp.int32, (2 * BLOCK, BLOCK), 0) % BLOCK
    kcol = lax.broadcasted_iota(jnp.int32, (2 * BLOCK, BLOCK), 1)
    diag = kcol < qrow

    def phase1(with_prev):
        items = []
        for p in range(PAIRS):
            cols = slice(p * LANES, (p + 1) * LANES)
            q = q_ref[:, cols]
            zq = jnp.zeros_like(q)
            qq = jnp.concatenate([jnp.where(first, q, zq), jnp.where(first, zq, q)], axis=0)
            qq_ref[p] = qq
            kd_ref[p] = kp2_ref[:, cols]
            vd_ref[p] = vp2_ref[:, cols]
            if with_prev:
                kb = jnp.concatenate([kp_ref[:, cols], kc_ref[:, cols]], axis=0)
                vb = jnp.concatenate([vp_ref[:, cols], vc_ref[:, cols]], axis=0)
                masks = [None, diag]
            else:
                kb, vb, masks = kc_ref[:, cols], vc_ref[:, cols], [diag]
            items.append((qq, kb, vb, None, masks))
        results = []
        _interleave(
            _swa_stages(qa_ref, kap_ref, kac_ref, vap_ref, vac_ref, bias_ref, 0 if with_prev else 1,
                        sink_ref, ga_ref, yt_ref, o_ref),
            _sb_stages(items, w2, results))
        for p, (acc, pv) in enumerate(results):
            acc_ref[p] = acc
            oacc_ref[p] = pv
            live_ref[p] = (jnp.max(acc) > PRUNE_LOG2).astype(jnp.int32)

    pl.when(n > 0)(functools.partial(phase1, True))
    pl.when(n == 0)(functools.partial(phase1, False))

    def pair_body(p, carry):
        def live():
            return (jnp.max(acc_ref[p]) > PRUNE_LOG2).astype(jnp.int32)

        def cond(c):
            j, go = c
            return jnp.logical_and(j >= 0, go > 0)

        def fetch(j, dst, col0, slot):
            src = proj_hbm.at[pl.ds(pl.multiple_of(j * BLOCK, BLOCK), BLOCK),
                              pl.ds(pl.multiple_of(col0 + p * LANES, LANES), LANES)]
            return pltpu.make_async_copy(src, dst.at[p], sem.at[slot])

        def body(c):
            j, _ = c

            @pl.when(j < n - 2)
            def _():
                ck = fetch(j, kd_ref, COL_KS, 0)
                cv = fetch(j, vd_ref, COL_VS, 1)
                ck.start()
                cv.start()
                ck.wait()
                cv.wait()

            results = []
            _interleave(_sb_stages([(qq_ref[p], kd_ref[p], vd_ref[p], acc_ref[p], [None])], w2, results))
            (acc, pv), = results
            acc_ref[p] = acc
            oacc_ref[p] += pv
            return j - 1, live()

        lax.while_loop(cond, body, (n - 2, live_ref[p]))
        return carry

    lax.fori_loop(0, PAIRS, pair_body, 0)

    ys = [jnp.where(first, oacc_ref[p, :BLOCK, :], oacc_ref[p, BLOCK:, :]) for p in range(PAIRS)]
    sq = functools.reduce(lambda a, b: a + b, [y * y for y in ys])
    inv = lax.rsqrt(jnp.sum(sq, axis=-1, keepdims=True) * (1.0 / SB_W) + EPS)
    for p in range(PAIRS):
        cols = slice(p * LANES, (p + 1) * LANES)
        o_ref[:, SWA_Q_W + p * LANES:SWA_Q_W + (p + 1) * LANES] = (ys[p] * inv * gb_ref[:, cols]).astype(BF16)


def _attention(proj, rel_bias, sinks, g_a, g_b):
    S = proj.shape[0]
    N = S // BLOCK
    bucket = jnp.asarray(_rel_bucket_table())
    w2 = jnp.asarray(_cumsum_weights(), dtype=BF16)
    back = lambda d: (lambda n: jnp.maximum(n - d, 0))
    wide = lambda rowf, c: pl.BlockSpec((BLOCK, SB_W), lambda n: (rowf(n), c))
    narrow = lambda rowf, c: pl.BlockSpec((BLOCK, LANES), lambda n: (rowf(n), c))
    smem = pl.BlockSpec(memory_space=pltpu.SMEM)
    return pl.pallas_call(
        _attn_kernel,
        out_shape=jax.ShapeDtypeStruct((S, D_MIX), BF16),
        grid=(N,),
        in_specs=[
            wide(back(0), WIDE_QA),
            narrow(back(1), COL_KA), narrow(back(0), COL_KA),
            narrow(back(1), COL_VA), narrow(back(0), COL_VA),
            _resident((2 * BLOCK, BLOCK)), smem, smem, _resident((1, SWA_Q_W)),
            wide(back(0), WIDE_QS),
            wide(back(1), WIDE_KS), wide(back(0), WIDE_KS),
            wide(back(1), WIDE_VS), wide(back(0), WIDE_VS),
            wide(back(2), WIDE_KS), wide(back(2), WIDE_VS),
            _resident((2 * BLOCK, 2 * BLOCK)), _resident((1, SB_W)),
            pl.BlockSpec(memory_space=pl.ANY),
        ],
        out_specs=pl.BlockSpec((BLOCK, D_MIX), lambda n: (n, 0)),
        scratch_shapes=[
            pltpu.VMEM((2, len(SWA_PAIRS), 2 * BLOCK, 2 * BLOCK), F32),
            pltpu.VMEM((SWA_Q_W, BLOCK), F32),
            pltpu.VMEM((PAIRS, 2 * BLOCK, LANES), BF16),
            pltpu.VMEM((PAIRS, BLOCK, LANES), BF16),
            pltpu.VMEM((PAIRS, BLOCK, LANES), BF16),
            pltpu.VMEM((PAIRS, 2 * BLOCK, BLOCK), F32),
            pltpu.VMEM((PAIRS, 2 * BLOCK, LANES), F32),
            pltpu.SMEM((PAIRS,), jnp.int32),
            pltpu.SemaphoreType.DMA((2,)),
        ],
        compiler_params=_params(("arbitrary",)),
        name="attention",
    )(proj, proj, proj, proj, proj, bucket, rel_bias, sinks, g_a,
      proj, proj, proj, proj, proj, proj, proj, w2, g_b, proj)


def _outproj_kernel(mix_ref, x_ref, gm_ref, w_ref, h_ref, hn_ref):
    h = x_ref[...] + jnp.dot(mix_ref[...], w_ref[...], preferred_element_type=F32)
    h_ref[...] = h
    hn_ref[...] = (_rms(h) * gm_ref[...]).astype(BF16)


def _outproj(mix, x, gm, w_bf16, tm=OUTPROJ_TM):
    S, D = x.shape
    row = lambda i: (i, 0)
    return pl.pallas_call(
        _outproj_kernel,
        out_shape=(jax.ShapeDtypeStruct((S, D), F32), jax.ShapeDtypeStruct((S, D), BF16)),
        grid=(S // tm,),
        in_specs=[
            pl.BlockSpec((tm, D_MIX), row),
            pl.BlockSpec((tm, D), row),
            _resident((1, D)),
            _resident((D_MIX, D)),
        ],
        out_specs=(pl.BlockSpec((tm, D), row), pl.BlockSpec((tm, D), row)),
        compiler_params=_params(("arbitrary",)),
        name="outproj",
    )(mix, x, gm, w_bf16)


def _convglu_kernel(hn_ref, halo_ref, wg_ref, wv_ref, wc_ref, bc_ref, wd_ref, h_ref, gf_ref,
                    o_ref, lhs_ref, gate_ref, *, tm):
    i = pl.program_id(0)
    f = pl.program_id(1)

    @pl.when(f == 0)
    def _():
        halo = halo_ref[...]
        lhs_ref[:HALO, :] = jnp.where(i > 0, halo, jnp.zeros_like(halo))
        lhs_ref[HALO:, :] = hn_ref[...]
        o_ref[...] = h_ref[...]

    gate_ref[...] = jnp.dot(lhs_ref[...], wg_ref[...], preferred_element_type=F32)
    val = jnp.dot(hn_ref[...], wv_ref[...], preferred_element_type=F32)
    gc = bc_ref[...]
    for tap in range(CONV_WIDTH):
        off = HALO - (CONV_WIDTH - 1) + tap
        gc = gc + gate_ref[pl.ds(off, tm), :] * wc_ref[tap:tap + 1, :]
    act = (gc * (1.0 / (1.0 + jnp.exp(-gc))) * val).astype(BF16)
    o_ref[...] += jnp.dot(act, wd_ref[...], preferred_element_type=F32)

    @pl.when(f == pl.num_programs(1) - 1)
    def _():
        o_ref[...] = _rms(o_ref[...]) * gf_ref[...]


def _convglu(hn2, h1, w_up_chunks, w_conv, b_conv, w_down_bf16, g_final, tm=CONVGLU_TM):
    S, D = h1.shape
    tf = w_up_chunks.shape[-1]
    nf = D_FF // tf
    halo_blocks = tm // HALO
    return pl.pallas_call(
        functools.partial(_convglu_kernel, tm=tm),
        out_shape=jax.ShapeDtypeStruct((S, D), F32),
        grid=(S // tm, nf),
        in_specs=[
            pl.BlockSpec((tm, D), lambda i, f: (i, 0)),
            pl.BlockSpec((HALO, D), lambda i, f: (jnp.maximum(i * halo_blocks - 1, 0), 0)),
            pl.BlockSpec((None, D, tf), lambda i, f: (f, 0, 0)),
            pl.BlockSpec((None, D, tf), lambda i, f: (nf + f, 0, 0)),
            pl.BlockSpec((CONV_WIDTH, tf), lambda i, f: (0, f)),
            pl.BlockSpec((1, tf), lambda i, f: (0, f)),
            pl.BlockSpec((tf, D), lambda i, f: (f, 0)),
            pl.BlockSpec((tm, D), lambda i, f: (i, 0)),
            _resident((1, D)),
        ],
        out_specs=pl.BlockSpec((tm, D), lambda i, f: (i, 0)),
        scratch_shapes=[
            pltpu.VMEM((tm + HALO, D), BF16),
            pltpu.VMEM((tm + HALO, tf), F32),
        ],
        compiler_params=_params(("arbitrary", "arbitrary"), CONVGLU_VMEM_LIMIT),
        name="convglu",
    )(hn2, hn2, w_up_chunks, w_up_chunks, w_conv, b_conv, w_down_bf16, h1, g_final)


def kernel(x, w_in, g_attn_norm, rel_bias, swa_sinks, g_swa_out, g_sb_out, w_out,
           g_mlp_norm, w_up, w_conv, b_conv, w_down, g_final):
    B, S, D = x.shape
    assert (B, S, D) == (1, SEQ, D_MODEL)
    x2 = x.reshape(S, D)

    col = np.ones((1, D_IN), np.float32)
    col[:, REF_SPLITS[0]:REF_SPLITS[1]] = SCALE
    col[:, REF_SPLITS[3]:REF_SPLITS[4]] = SCALE
    w_in_s = (w_in * jnp.asarray(col)).astype(BF16)
    w_in_b = jnp.concatenate([w_in_s[:, REF_SPLITS[i]:REF_SPLITS[i + 1]] for i in PERM_GROUPS], axis=1)

    proj, (w_out_b, w_up_b, w_down_b) = _inproj(x2, g_attn_norm.reshape(1, D), w_in_b,
                                                ((w_out, None), (w_up, CONVGLU_TF), (w_down, None)))
    mix = _attention(proj, rel_bias, swa_sinks, g_swa_out.reshape(1, -1), g_sb_out.reshape(1, -1))
    h1, hn2 = _outproj(mix, x2, g_mlp_norm.reshape(1, D), w_out_b)
    out = _convglu(hn2, h1, w_up_b, w_conv, b_conv.reshape(1, -1), w_down_b, g_final.reshape(1, D))
    return out.reshape(B, S, D)
```

```python
import functools
import math

import numpy as np
import jax
import jax.numpy as jnp
from jax import lax
from jax.experimental import pallas as pl
from jax.experimental.pallas import tpu as pltpu

D_MODEL = 2048
SEQ = 16384
HEAD_DIM = 64
SWA_Q_HEADS = 16
SWA_KV_HEADS = 2
SWA_GROUP = SWA_Q_HEADS // SWA_KV_HEADS
SB_HEADS = 16
WINDOW = 128
BLOCK = 128
REL_BUCKETS = 32
REL_MAX_DIST = 128
D_FF = 5632
CONV_WIDTH = 3
EPS = 1e-6
NEG_INF = -1e30

SWA_Q_W = SWA_Q_HEADS * HEAD_DIM
SWA_KV_W = SWA_KV_HEADS * HEAD_DIM
SB_W = SB_HEADS * HEAD_DIM
D_MIX = SWA_Q_W + SB_W
D_IN = SWA_Q_W + 2 * SWA_KV_W + 3 * SB_W

LANES = 128
REF_SPLITS = np.cumsum([0, SWA_Q_W, SWA_KV_W, SWA_KV_W, SB_W, SB_W, SB_W])
PERM_GROUPS = (3, 4, 5, 0, 1, 2)
WIDE_QS, WIDE_KS, WIDE_VS, WIDE_QA = 0, 1, 2, 3
COL_KS = SB_W
COL_VS = 2 * SB_W
COL_KA = (3 * SB_W + SWA_Q_W) // LANES
COL_VA = COL_KA + SWA_KV_W // LANES
PAIRS = SB_W // LANES
SWA_PAIRS = tuple((h, h + 2) for g in range(SWA_KV_HEADS) for par in (0, 1)
                  for h in range(g * SWA_GROUP + par, (g + 1) * SWA_GROUP, 4))

SCALE = HEAD_DIM ** -0.5
TAIL = 8

V7X_VMEM_BYTES = 64 * 1024 * 1024
VMEM_LIMIT = V7X_VMEM_BYTES - 8 * 1024 * 1024
INPROJ_TM = 512
OUTPROJ_TM = 512
CONVGLU_TM = 1024
CONVGLU_TF = 512
CONVGLU_VMEM_LIMIT = V7X_VMEM_BYTES - 2 * 1024 * 1024

F32 = jnp.float32
BF16 = jnp.bfloat16

PRUNE_LOG = -88.0
LOG2E = math.log2(math.e)
PRUNE_LOG2 = PRUNE_LOG * LOG2E


def _params(sem, vmem=VMEM_LIMIT):
    return pltpu.CompilerParams(dimension_semantics=sem, vmem_limit_bytes=vmem)


def _rms(y):
    return y * lax.rsqrt(jnp.mean(y * y, axis=-1, keepdims=True) + EPS)


def _inproj_kernel(n_later, x_ref, g_ref, w_ref, *refs):
    srcs, o_ref, dsts = refs[:n_later], refs[n_later], refs[n_later + 1:]
    hn = (_rms(x_ref[...]) * g_ref[...]).astype(BF16)
    o_ref[...] = jnp.dot(hn, w_ref[...], preferred_element_type=F32).astype(BF16)
    for src, dst in zip(srcs, dsts):
        if len(dst.shape) == 2:
            dst[...] = src[...].astype(BF16)
        else:
            width = dst.shape[-1]
            for c in range(dst.shape[0]):
                dst[c] = src[:, c * width:(c + 1) * width].astype(BF16)


def _resident(shape):
    return pl.BlockSpec(shape, lambda *_: (0,) * len(shape), pipeline_mode=pl.Buffered(1))


def _inproj(x, g, w_bf16, later_weights, tm=INPROJ_TM):
    S, D = x.shape
    N = w_bf16.shape[1]
    steps = S // tm
    weights = [w for w, _ in later_weights]
    in_slabs = [pl.BlockSpec((w.shape[0] // steps, w.shape[1]), lambda i: (i, 0)) for w in weights]
    out_shapes, out_slabs = [], []
    for w, width in later_weights:
        rows, cols = w.shape
        if width is None:
            out_shapes.append(jax.ShapeDtypeStruct((rows, cols), BF16))
            out_slabs.append(pl.BlockSpec((rows // steps, cols), lambda i: (i, 0)))
        else:
            out_shapes.append(jax.ShapeDtypeStruct((cols // width, rows, width), BF16))
            out_slabs.append(pl.BlockSpec((cols // width, rows // steps, width), lambda i: (0, i, 0)))
    outs = pl.pallas_call(
        functools.partial(_inproj_kernel, len(weights)),
        out_shape=[jax.ShapeDtypeStruct((S, N), BF16)] + out_shapes,
        grid=(steps,),
        in_specs=[
            pl.BlockSpec((tm, D), lambda i: (i, 0)),
            _resident((1, D)),
            _resident((D, N)),
        ] + in_slabs,
        out_specs=[pl.BlockSpec((tm, N), lambda i: (i, 0))] + out_slabs,
        compiler_params=_params(("arbitrary",)),
        name="inproj",
    )(x, g, w_bf16, *weights)
    return outs[0], outs[1:]


def _rel_bucket_table():
    qi = np.arange(BLOCK, dtype=np.int64)[None, :]
    kj = np.arange(2 * BLOCK, dtype=np.int64)[:, None]
    dist = qi + BLOCK - kj
    in_win = (dist >= 0) & (dist < WINDOW)
    dc = np.clip(dist, 0, None)
    max_exact = REL_BUCKETS // 2
    d = np.maximum(dc, 1).astype(np.float32)
    large = max_exact + (np.log(d / np.float32(max_exact)) / np.float32(math.log(REL_MAX_DIST / max_exact))
                         * np.float32(REL_BUCKETS - max_exact)).astype(np.int32)
    large = np.minimum(large, REL_BUCKETS - 1)
    bucket = np.where(dc < max_exact, dc, large).astype(np.int32)
    return np.where(in_win, bucket, -1).astype(np.int32)


def _cumsum_weights():
    kk = np.arange(BLOCK)
    upper = (kk[:, None] > kk[None, :]).astype(np.float32)
    w = np.concatenate([upper, np.ones((BLOCK, BLOCK), np.float32)], axis=1)
    return np.concatenate([w, w], axis=0)


def _interleave(*stage_generators):
    pending = list(stage_generators)
    while pending:
        for g in list(pending):
            try:
                next(g)
            except StopIteration:
                pending.remove(g)


def _swa_stages(q_ref, kp_ref, kc_ref, vp_ref, vc_ref, bias_ref, variant, sink_ref, g_ref, yt_ref, o_ref):
    lane = lax.broadcasted_iota(jnp.int32, (BLOCK, LANES), 1)
    lo = lane < HEAD_DIM
    second = lax.broadcasted_iota(jnp.int32, (1, 2 * BLOCK), 1) >= BLOCK

    k2 = jnp.concatenate([kp_ref[...], kc_ref[...]], axis=0)
    k2s = jnp.concatenate([k2[:, HEAD_DIM:], k2[:, :HEAD_DIM]], axis=1)
    v2t = jnp.concatenate([vp_ref[...], vc_ref[...]], axis=0).T

    def masked_q(h):
        q = q_ref[:, (h // 2) * LANES:(h // 2 + 1) * LANES]
        keep = lo if h % 2 == 0 else jnp.logical_not(lo)
        return jnp.where(keep, q, jnp.zeros_like(q))

    idx = range(len(SWA_PAIRS))
    logits, sinks = [], []
    for i, (ha, hb) in enumerate(SWA_PAIRS):
        group = ha // SWA_GROUP
        keys = k2 if (ha % 2) == group else k2s
        qw = jnp.concatenate([masked_q(ha), masked_q(hb)], axis=0)
        st = lax.dot_general(keys, qw, (((1,), (1,)), ((), ())), preferred_element_type=F32)
        logits.append(st + bias_ref[variant, i])
        sinks.append(jnp.where(second, sink_ref[hb], sink_ref[ha]))
    yield
    ms = [jnp.maximum(jnp.max(logits[i], axis=0, keepdims=True), sinks[i]) for i in idx]
    ps = [jnp.exp(logits[i] - ms[i]) for i in idx]
    yield
    invs = [1.0 / (jnp.sum(ps[i], axis=0, keepdims=True) + jnp.exp(sinks[i] - ms[i])) for i in idx]
    yield
    for i, (ha, hb) in enumerate(SWA_PAIRS):
        w = (ps[i] * invs[i]).astype(BF16)
        out = jnp.dot(v2t, w, preferred_element_type=F32)
        rows = slice((ha // SWA_GROUP) * HEAD_DIM, (ha // SWA_GROUP + 1) * HEAD_DIM)
        yt_ref[ha * HEAD_DIM:(ha + 1) * HEAD_DIM, :] = out[rows, :BLOCK]
        yt_ref[hb * HEAD_DIM:(hb + 1) * HEAD_DIM, :] = out[rows, BLOCK:]
    yield
    yt = yt_ref[...]
    inv = lax.rsqrt(jnp.mean(yt * yt, axis=0, keepdims=True) + EPS)
    o_ref[:, :SWA_Q_W] = ((yt * inv).T * g_ref[...]).astype(BF16)


def _sb_stages(items, w2, results):
    idx = range(len(items))
    nblk = items[0][1].shape[0] // BLOCK
    zs = [lax.dot_general(qq, kb, (((1,), (1,)), ((), ())), preferred_element_type=F32) * LOG2E
          for qq, kb, _, _, _ in items]
    yield
    lgs = [jnp.log2(1.0 + jnp.exp2(-jnp.abs(z))) for z in zs]
    logsigs = [jnp.minimum(zs[i], 0.0) - lgs[i] for i in idx]
    log1ms = [logsigs[i] - zs[i] for i in idx]
    yield
    runnings = [it[3] for it in items]
    parts = [[None] * nblk for _ in idx]
    for c in reversed(range(nblk)):
        cs = slice(c * BLOCK, (c + 1) * BLOCK)
        rs = []
        for i in idx:
            mask = items[i][4][c]
            l1 = log1ms[i][:, cs]
            if mask is not None:
                l1 = jnp.where(mask, l1, 0.0)
            hi = l1.astype(BF16)
            lo = (l1 - hi.astype(F32)).astype(BF16)
            rs.append(jnp.dot(jnp.concatenate([hi, lo], axis=1), w2, preferred_element_type=F32))
        yield
        for i in idx:
            mask = items[i][4][c]
            log_a = logsigs[i][:, cs] + rs[i][:, :BLOCK]
            if runnings[i] is not None:
                log_a = log_a + runnings[i]
            a = jnp.exp2(log_a)
            if mask is not None:
                a = jnp.where(mask, a, 0.0)
            parts[i][c] = a.astype(BF16)
            runnings[i] = rs[i][:, BLOCK:] if runnings[i] is None else runnings[i] + rs[i][:, BLOCK:]
        yield
    for i in idx:
        amat = parts[i][0] if nblk == 1 else jnp.concatenate(parts[i], axis=1)
        results.append((runnings[i], jnp.dot(amat, items[i][2], preferred_element_type=F32)))


def _attn_kernel(qa_ref, kap_ref, kac_ref, vap_ref, vac_ref, bucket_ref, relb_ref, sink_ref, ga_ref,
                 q_ref, kp_ref, kc_ref, vp_ref, vc_ref, kp2_ref, vp2_ref, w2_ref, gb_ref, proj_hbm,
                 o_ref, bias_ref, yt_ref, qq_ref, kd_ref, vd_ref, acc_ref, oacc_ref, live_ref, sem):
    n = pl.program_id(0)

    @pl.when(n == 0)
    def _():
        bucket = bucket_ref[...]
        krow = lax.broadcasted_iota(jnp.int32, bucket.shape, 0)
        for i, pair in enumerate(SWA_PAIRS):
            for side, h in enumerate(pair):
                t = jnp.full(bucket.shape, NEG_INF, F32)
                for r in range(REL_BUCKETS):
                    t = jnp.where(bucket == r, relb_ref[r, h], t)
                cols = slice(side * BLOCK, (side + 1) * BLOCK)
                bias_ref[0, i, :, cols] = t
                bias_ref[1, i, :, cols] = jnp.where(krow >= BLOCK, t, NEG_INF)

    lane = lax.broadcasted_iota(jnp.int32, (BLOCK, LANES), 1)
    first = lane < HEAD_DIM
    w2 = w2_ref[...]
    qrow = lax.broadcasted_iota(jnp.int32, (2 * BLOCK, BLOCK), 0) % BLOCK
    kcol = lax.broadcasted_iota(jnp.int32, (2 * BLOCK, BLOCK), 1)
    diag = kcol < qrow

    def phase1(with_prev):
        items = []
        for p in range(PAIRS):
            cols = slice(p * LANES, (p + 1) * LANES)
            q = q_ref[:, cols]
            zq = jnp.zeros_like(q)
            qq = jnp.concatenate([jnp.where(first, q, zq), jnp.where(first, zq, q)], axis=0)
            qq_ref[p] = qq
            kd_ref[p] = kp2_ref[:, cols]
            vd_ref[p] = vp2_ref[:, cols]
            if with_prev:
                kb = jnp.concatenate([kp_ref[:, cols], kc_ref[:, cols]], axis=0)
                vb = jnp.concatenate([vp_ref[:, cols], vc_ref[:, cols]], axis=0)
                masks = [None, diag]
            else:
                kb, vb, masks = kc_ref[:, cols], vc_ref[:, cols], [diag]
            items.append((qq, kb, vb, None, masks))
        results = []
        _interleave(
            _swa_stages(qa_ref, kap_ref, kac_ref, vap_ref, vac_ref, bias_ref, 0 if with_prev else 1,
                        sink_ref, ga_ref, yt_ref, o_ref),
            _sb_stages(items, w2, results))
        for p, (acc, pv) in enumerate(results):
            acc_ref[p] = acc
            oacc_ref[p] = pv
            live_ref[p] = (jnp.max(acc) > PRUNE_LOG2).astype(jnp.int32)

    pl.when(n > 0)(functools.partial(phase1, True))
    pl.when(n == 0)(functools.partial(phase1, False))

    def pair_body(p, carry):
        def live():
            return (jnp.max(acc_ref[p]) > PRUNE_LOG2).astype(jnp.int32)

        def cond(c):
            j, go = c
            return jnp.logical_and(j >= 0, go > 0)

        def fetch(j, dst, col0, slot):
            src = proj_hbm.at[pl.ds(pl.multiple_of(j * BLOCK, BLOCK), BLOCK),
                              pl.ds(pl.multiple_of(col0 + p * LANES, LANES), LANES)]
            return pltpu.make_async_copy(src, dst.at[p], sem.at[slot])

        def body(c):
            j, _ = c

            @pl.when(j < n - 2)
            def _():
                ck = fetch(j, kd_ref, COL_KS, 0)
                cv = fetch(j, vd_ref, COL_VS, 1)
                ck.start()
                cv.start()
                ck.wait()
                cv.wait()

            results = []
            _interleave(_sb_stages([(qq_ref[p], kd_ref[p], vd_ref[p], acc_ref[p], [None])], w2, results))
            (acc, pv), = results
            acc_ref[p] = acc
            oacc_ref[p] += pv
            return j - 1, live()

        lax.while_loop(cond, body, (n - 2, live_ref[p]))
        return carry

    lax.fori_loop(0, PAIRS, pair_body, 0)

    ys = [jnp.where(first, oacc_ref[p, :BLOCK, :], oacc_ref[p, BLOCK:, :]) for p in range(PAIRS)]
    sq = functools.reduce(lambda a, b: a + b, [y * y for y in ys])
    inv = lax.rsqrt(jnp.sum(sq, axis=-1, keepdims=True) * (1.0 / SB_W) + EPS)
    for p in range(PAIRS):
        cols = slice(p * LANES, (p + 1) * LANES)
        o_ref[:, SWA_Q_W + p * LANES:SWA_Q_W + (p + 1) * LANES] = (ys[p] * inv * gb_ref[:, cols]).astype(BF16)


def _attention(proj, rel_bias, sinks, g_a, g_b):
    S = proj.shape[0]
    N = S // BLOCK
    bucket = jnp.asarray(_rel_bucket_table())
    w2 = jnp.asarray(_cumsum_weights(), dtype=BF16)
    back = lambda d: (lambda n: jnp.maximum(n - d, 0))
    wide = lambda rowf, c: pl.BlockSpec((BLOCK, SB_W), lambda n: (rowf(n), c))
    narrow = lambda rowf, c: pl.BlockSpec((BLOCK, LANES), lambda n: (rowf(n), c))
    smem = pl.BlockSpec(memory_space=pltpu.SMEM)
    return pl.pallas_call(
        _attn_kernel,
        out_shape=jax.ShapeDtypeStruct((S, D_MIX), BF16),
        grid=(N,),
        in_specs=[
            wide(back(0), WIDE_QA),
            narrow(back(1), COL_KA), narrow(back(0), COL_KA),
            narrow(back(1), COL_VA), narrow(back(0), COL_VA),
            _resident((2 * BLOCK, BLOCK)), smem, smem, _resident((1, SWA_Q_W)),
            wide(back(0), WIDE_QS),
            wide(back(1), WIDE_KS), wide(back(0), WIDE_KS),
            wide(back(1), WIDE_VS), wide(back(0), WIDE_VS),
            wide(back(2), WIDE_KS), wide(back(2), WIDE_VS),
            _resident((2 * BLOCK, 2 * BLOCK)), _resident((1, SB_W)),
            pl.BlockSpec(memory_space=pl.ANY),
        ],
        out_specs=pl.BlockSpec((BLOCK, D_MIX), lambda n: (n, 0)),
        scratch_shapes=[
            pltpu.VMEM((2, len(SWA_PAIRS), 2 * BLOCK, 2 * BLOCK), F32),
            pltpu.VMEM((SWA_Q_W, BLOCK), F32),
            pltpu.VMEM((PAIRS, 2 * BLOCK, LANES), BF16),
            pltpu.VMEM((PAIRS, BLOCK, LANES), BF16),
            pltpu.VMEM((PAIRS, BLOCK, LANES), BF16),
            pltpu.VMEM((PAIRS, 2 * BLOCK, BLOCK), F32),
            pltpu.VMEM((PAIRS, 2 * BLOCK, LANES), F32),
            pltpu.SMEM((PAIRS,), jnp.int32),
            pltpu.SemaphoreType.DMA((2,)),
        ],
        compiler_params=_params(("arbitrary",)),
        name="attention",
    )(proj, proj, proj, proj, proj, bucket, rel_bias, sinks, g_a,
      proj, proj, proj, proj, proj, proj, proj, w2, g_b, proj)


def _outproj_kernel(mix_ref, x_ref, gm_ref, w_ref, h_ref, hn_ref):
    h = x_ref[...] + jnp.dot(mix_ref[...], w_ref[...], preferred_element_type=F32)
    h_ref[...] = h
    hn_ref[...] = (_rms(h) * gm_ref[...]).astype(BF16)


def _outproj(mix, x, gm, w_bf16, tm=OUTPROJ_TM):
    S, D = x.shape
    row = lambda i: (i, 0)
    return pl.pallas_call(
        _outproj_kernel,
        out_shape=(jax.ShapeDtypeStruct((S, D), F32), jax.ShapeDtypeStruct((S, D), BF16)),
        grid=(S // tm,),
        in_specs=[
            pl.BlockSpec((tm, D_MIX), row),
            pl.BlockSpec((tm, D), row),
            _resident((1, D)),
            _resident((D_MIX, D)),
        ],
        out_specs=(pl.BlockSpec((tm, D), row), pl.BlockSpec((tm, D), row)),
        compiler_params=_params(("arbitrary",)),
        name="outproj",
    )(mix, x, gm, w_bf16)


def _convglu_kernel(hn_ref, wg_ref, wv_ref, wc_ref, bc_ref, wd_ref, h_ref, gf_ref,
                    o_ref, gate_ref, tail_ref, *, tm):
    i = pl.program_id(0)
    f = pl.program_id(1)

    @pl.when(f == 0)
    def _():
        o_ref[...] = h_ref[...]

    @pl.when(i == 0)
    def _():
        tail_ref[f] = jnp.zeros(tail_ref.shape[1:], F32)

    hn = hn_ref[...]
    gate_ref[:TAIL, :] = tail_ref[f]
    gate_ref[TAIL:, :] = jnp.dot(hn, wg_ref[...], preferred_element_type=F32)
    tail_ref[f] = gate_ref[tm:, :]
    val = jnp.dot(hn, wv_ref[...], preferred_element_type=F32)
    gc = bc_ref[...]
    for tap in range(CONV_WIDTH):
        off = TAIL - (CONV_WIDTH - 1) + tap
        gc = gc + gate_ref[pl.ds(off, tm), :] * wc_ref[tap:tap + 1, :]
    act = (gc * (1.0 / (1.0 + jnp.exp(-gc))) * val).astype(BF16)
    o_ref[...] += jnp.dot(act, wd_ref[...], preferred_element_type=F32)

    @pl.when(f == pl.num_programs(1) - 1)
    def _():
        o_ref[...] = _rms(o_ref[...]) * gf_ref[...]


def _convglu(hn2, h1, w_up_chunks, w_conv, b_conv, w_down_bf16, g_final, tm=CONVGLU_TM):
    S, D = h1.shape
    tf = w_up_chunks.shape[-1]
    nf = D_FF // tf
    return pl.pallas_call(
        functools.partial(_convglu_kernel, tm=tm),
        out_shape=jax.ShapeDtypeStruct((S, D), F32),
        grid=(S // tm, nf),
        in_specs=[
            pl.BlockSpec((tm, D), lambda i, f: (i, 0)),
            pl.BlockSpec((None, D, tf), lambda i, f: (f, 0, 0)),
            pl.BlockSpec((None, D, tf), lambda i, f: (nf + f, 0, 0)),
            pl.BlockSpec((CONV_WIDTH, tf), lambda i, f: (0, f)),
            pl.BlockSpec((1, tf), lambda i, f: (0, f)),
            pl.BlockSpec((tf, D), lambda i, f: (f, 0)),
            pl.BlockSpec((tm, D), lambda i, f: (i, 0)),
            _resident((1, D)),
        ],
        out_specs=pl.BlockSpec((tm, D), lambda i, f: (i, 0)),
        scratch_shapes=[
            pltpu.VMEM((TAIL + tm, tf), F32),
            pltpu.VMEM((nf, TAIL, tf), F32),
        ],
        compiler_params=_params(("arbitrary", "arbitrary"), CONVGLU_VMEM_LIMIT),
        name="convglu",
    )(hn2, w_up_chunks, w_up_chunks, w_conv, b_conv, w_down_bf16, h1, g_final)


def kernel(x, w_in, g_attn_norm, rel_bias, swa_sinks, g_swa_out, g_sb_out, w_out,
           g_mlp_norm, w_up, w_conv, b_conv, w_down, g_final):
    B, S, D = x.shape
    assert (B, S, D) == (1, SEQ, D_MODEL)
    x2 = x.reshape(S, D)

    col = np.ones((1, D_IN), np.float32)
    col[:, REF_SPLITS[0]:REF_SPLITS[1]] = SCALE
    col[:, REF_SPLITS[3]:REF_SPLITS[4]] = SCALE
    w_in_s = (w_in * jnp.asarray(col)).astype(BF16)
    w_in_b = jnp.concatenate([w_in_s[:, REF_SPLITS[i]:REF_SPLITS[i + 1]] for i in PERM_GROUPS], axis=1)

    proj, (w_out_b, w_up_b, w_down_b) = _inproj(x2, g_attn_norm.reshape(1, D), w_in_b,
                                                ((w_out, None), (w_up, CONVGLU_TF), (w_down, None)))
    mix = _attention(proj, rel_bias, swa_sinks, g_swa_out.reshape(1, -1), g_sb_out.reshape(1, -1))
    h1, hn2 = _outproj(mix, x2, g_mlp_norm.reshape(1, D), w_out_b)
    out = _convglu(hn2, h1, w_up_b, w_conv, b_conv.reshape(1, -1), w_down_b, g_final.reshape(1, D))
    return out.reshape(B, S, D)
```

```python
import functools
import math

import numpy as np
import jax
import jax.numpy as jnp
from jax import lax
from jax.experimental import pallas as pl
from jax.experimental.pallas import tpu as pltpu

D_MODEL = 2048
SEQ = 16384
HEAD_DIM = 64
SWA_Q_HEADS = 16
SWA_KV_HEADS = 2
SWA_GROUP = SWA_Q_HEADS // SWA_KV_HEADS
SB_HEADS = 16
WINDOW = 128
BLOCK = 128
REL_BUCKETS = 32
REL_MAX_DIST = 128
D_FF = 5632
CONV_WIDTH = 3
EPS = 1e-6
NEG_INF = -1e30

SWA_Q_W = SWA_Q_HEADS * HEAD_DIM
SWA_KV_W = SWA_KV_HEADS * HEAD_DIM
SB_W = SB_HEADS * HEAD_DIM
D_MIX = SWA_Q_W + SB_W
D_IN = SWA_Q_W + 2 * SWA_KV_W + 3 * SB_W

LANES = 128
REF_SPLITS = np.cumsum([0, SWA_Q_W, SWA_KV_W, SWA_KV_W, SB_W, SB_W, SB_W])
PERM_GROUPS = (3, 4, 5, 0, 1, 2)
WIDE_QS, WIDE_KS, WIDE_VS, WIDE_QA = 0, 1, 2, 3
COL_KS = SB_W
COL_VS = 2 * SB_W
COL_KA = (3 * SB_W + SWA_Q_W) // LANES
COL_VA = COL_KA + SWA_KV_W // LANES
PAIRS = SB_W // LANES
SWA_PAIRS = tuple((h, h + 2) for g in range(SWA_KV_HEADS) for par in (0, 1)
                  for h in range(g * SWA_GROUP + par, (g + 1) * SWA_GROUP, 4))

SCALE = HEAD_DIM ** -0.5
TAIL = 8

V7X_VMEM_BYTES = 64 * 1024 * 1024
VMEM_LIMIT = V7X_VMEM_BYTES - 8 * 1024 * 1024
INPROJ_TM = 512
OUTPROJ_TM = 1024
CONVGLU_TM = 1024
CONVGLU_TF = 512
CONVGLU_VMEM_LIMIT = V7X_VMEM_BYTES - 2 * 1024 * 1024

F32 = jnp.float32
BF16 = jnp.bfloat16

PRUNE_LOG = -88.0
LOG2E = math.log2(math.e)
PRUNE_LOG2 = PRUNE_LOG * LOG2E


def _params(sem, vmem=VMEM_LIMIT):
    return pltpu.CompilerParams(dimension_semantics=sem, vmem_limit_bytes=vmem)


def _rms(y):
    return y * lax.rsqrt(jnp.mean(y * y, axis=-1, keepdims=True) + EPS)


def _inproj_kernel(n_later, x_ref, g_ref, w_ref, *refs):
    srcs, o_ref, dsts = refs[:n_later], refs[n_later], refs[n_later + 1:]
    hn = (_rms(x_ref[...]) * g_ref[...]).astype(BF16)
    o_ref[...] = jnp.dot(hn, w_ref[...], preferred_element_type=F32).astype(BF16)
    for src, dst in zip(srcs, dsts):
        if len(dst.shape) == 2:
            dst[...] = src[...].astype(BF16)
        else:
            width = dst.shape[-1]
            for c in range(dst.shape[0]):
                dst[c] = src[:, c * width:(c + 1) * width].astype(BF16)


def _resident(shape):
    return pl.BlockSpec(shape, lambda *_: (0,) * len(shape), pipeline_mode=pl.Buffered(1))


def _inproj(x, g, w_bf16, later_weights, tm=INPROJ_TM):
    S, D = x.shape
    N = w_bf16.shape[1]
    steps = S // tm
    weights = [w for w, _ in later_weights]
    in_slabs = [pl.BlockSpec((w.shape[0] // steps, w.shape[1]), lambda i: (i, 0)) for w in weights]
    out_shapes, out_slabs = [], []
    for w, width in later_weights:
        rows, cols = w.shape
        if width is None:
            out_shapes.append(jax.ShapeDtypeStruct((rows, cols), BF16))
            out_slabs.append(pl.BlockSpec((rows // steps, cols), lambda i: (i, 0)))
        else:
            out_shapes.append(jax.ShapeDtypeStruct((cols // width, rows, width), BF16))
            out_slabs.append(pl.BlockSpec((cols // width, rows // steps, width), lambda i: (0, i, 0)))
    outs = pl.pallas_call(
        functools.partial(_inproj_kernel, len(weights)),
        out_shape=[jax.ShapeDtypeStruct((S, N), BF16)] + out_shapes,
        grid=(steps,),
        in_specs=[
            pl.BlockSpec((tm, D), lambda i: (i, 0)),
            _resident((1, D)),
            _resident((D, N)),
        ] + in_slabs,
        out_specs=[pl.BlockSpec((tm, N), lambda i: (i, 0))] + out_slabs,
        compiler_params=_params(("arbitrary",)),
        name="inproj",
    )(x, g, w_bf16, *weights)
    return outs[0], outs[1:]


def _rel_bucket_table():
    qi = np.arange(BLOCK, dtype=np.int64)[None, :]
    kj = np.arange(2 * BLOCK, dtype=np.int64)[:, None]
    dist = qi + BLOCK - kj
    in_win = (dist >= 0) & (dist < WINDOW)
    dc = np.clip(dist, 0, None)
    max_exact = REL_BUCKETS // 2
    d = np.maximum(dc, 1).astype(np.float32)
    large = max_exact + (np.log(d / np.float32(max_exact)) / np.float32(math.log(REL_MAX_DIST / max_exact))
                         * np.float32(REL_BUCKETS - max_exact)).astype(np.int32)
    large = np.minimum(large, REL_BUCKETS - 1)
    bucket = np.where(dc < max_exact, dc, large).astype(np.int32)
    return np.where(in_win, bucket, -1).astype(np.int32)


def _cumsum_weights():
    kk = np.arange(BLOCK)
    upper = (kk[:, None] > kk[None, :]).astype(np.float32)
    w = np.concatenate([upper, np.ones((BLOCK, BLOCK), np.float32)], axis=1)
    return np.concatenate([w, w], axis=0)


def _interleave(*stage_generators):
    pending = list(stage_generators)
    while pending:
        for g in list(pending):
            try:
                next(g)
            except StopIteration:
                pending.remove(g)


def _swa_stages(q_ref, kp_ref, kc_ref, vp_ref, vc_ref, bias_ref, variant, sink_ref, g_ref, yt_ref, o_ref):
    lane = lax.broadcasted_iota(jnp.int32, (BLOCK, LANES), 1)
    lo = lane < HEAD_DIM
    second = lax.broadcasted_iota(jnp.int32, (1, 2 * BLOCK), 1) >= BLOCK

    k2 = jnp.concatenate([kp_ref[...], kc_ref[...]], axis=0)
    k2s = jnp.concatenate([k2[:, HEAD_DIM:], k2[:, :HEAD_DIM]], axis=1)
    v2t = jnp.concatenate([vp_ref[...], vc_ref[...]], axis=0).T

    def masked_q(h):
        q = q_ref[:, (h // 2) * LANES:(h // 2 + 1) * LANES]
        keep = lo if h % 2 == 0 else jnp.logical_not(lo)
        return jnp.where(keep, q, jnp.zeros_like(q))

    idx = range(len(SWA_PAIRS))
    logits, sinks = [], []
    for i, (ha, hb) in enumerate(SWA_PAIRS):
        group = ha // SWA_GROUP
        keys = k2 if (ha % 2) == group else k2s
        qw = jnp.concatenate([masked_q(ha), masked_q(hb)], axis=0)
        st = lax.dot_general(keys, qw, (((1,), (1,)), ((), ())), preferred_element_type=F32)
        logits.append(st + bias_ref[variant, i])
        sinks.append(jnp.where(second, sink_ref[hb], sink_ref[ha]))
    yield
    ms = [jnp.maximum(jnp.max(logits[i], axis=0, keepdims=True), sinks[i]) for i in idx]
    ps = [jnp.exp(logits[i] - ms[i]) for i in idx]
    yield
    invs = [1.0 / (jnp.sum(ps[i], axis=0, keepdims=True) + jnp.exp(sinks[i] - ms[i])) for i in idx]
    yield
    for i, (ha, hb) in enumerate(SWA_PAIRS):
        w = (ps[i] * invs[i]).astype(BF16)
        out = jnp.dot(v2t, w, preferred_element_type=F32)
        rows = slice((ha // SWA_GROUP) * HEAD_DIM, (ha // SWA_GROUP + 1) * HEAD_DIM)
        yt_ref[ha * HEAD_DIM:(ha + 1) * HEAD_DIM, :] = out[rows, :BLOCK]
        yt_ref[hb * HEAD_DIM:(hb + 1) * HEAD_DIM, :] = out[rows, BLOCK:]
    yield
    yt = yt_ref[...]
    inv = lax.rsqrt(jnp.mean(yt * yt, axis=0, keepdims=True) + EPS)
    o_ref[:, :SWA_Q_W] = ((yt * inv).T * g_ref[...]).astype(BF16)


def _sb_stages(items, w2, results):
    idx = range(len(items))
    nblk = items[0][1].shape[0] // BLOCK
    zs = [lax.dot_general(qq, kb, (((1,), (1,)), ((), ())), preferred_element_type=F32) * LOG2E
          for qq, kb, _, _, _ in items]
    yield
    lgs = [jnp.log2(1.0 + jnp.exp2(-jnp.abs(z))) for z in zs]
    logsigs = [jnp.minimum(zs[i], 0.0) - lgs[i] for i in idx]
    log1ms = [logsigs[i] - zs[i] for i in idx]
    yield
    runnings = [it[3] for it in items]
    parts = [[None] * nblk for _ in idx]
    for c in reversed(range(nblk)):
        cs = slice(c * BLOCK, (c + 1) * BLOCK)
        rs = []
        for i in idx:
            mask = items[i][4][c]
            l1 = log1ms[i][:, cs]
            if mask is not None:
                l1 = jnp.where(mask, l1, 0.0)
            hi = l1.astype(BF16)
            lo = (l1 - hi.astype(F32)).astype(BF16)
            rs.append(jnp.dot(jnp.concatenate([hi, lo], axis=1), w2, preferred_element_type=F32))
        yield
        for i in idx:
            mask = items[i][4][c]
            log_a = logsigs[i][:, cs] + rs[i][:, :BLOCK]
            if runnings[i] is not None:
                log_a = log_a + runnings[i]
            a = jnp.exp2(log_a)
            if mask is not None:
                a = jnp.where(mask, a, 0.0)
            parts[i][c] = a.astype(BF16)
            runnings[i] = rs[i][:, BLOCK:] if runnings[i] is None else runnings[i] + rs[i][:, BLOCK:]
        yield
    for i in idx:
        amat = parts[i][0] if nblk == 1 else jnp.concatenate(parts[i], axis=1)
        results.append((runnings[i], jnp.dot(amat, items[i][2], preferred_element_type=F32)))


def _attn_kernel(qa_ref, kap_ref, kac_ref, vap_ref, vac_ref, bucket_ref, relb_ref, sink_ref, ga_ref,
                 q_ref, kp_ref, kc_ref, vp_ref, vc_ref, kp2_ref, vp2_ref, w2_ref, gb_ref, proj_hbm,
                 o_ref, bias_ref, yt_ref, qq_ref, kd_ref, vd_ref, acc_ref, oacc_ref, live_ref, sem):
    n = pl.program_id(0)

    @pl.when(n == 0)
    def _():
        bucket = bucket_ref[...]
        krow = lax.broadcasted_iota(jnp.int32, bucket.shape, 0)
        for i, pair in enumerate(SWA_PAIRS):
            for side, h in enumerate(pair):
                t = jnp.full(bucket.shape, NEG_INF, F32)
                for r in range(REL_BUCKETS):
                    t = jnp.where(bucket == r, relb_ref[r, h], t)
                cols = slice(side * BLOCK, (side + 1) * BLOCK)
                bias_ref[0, i, :, cols] = t
                bias_ref[1, i, :, cols] = jnp.where(krow >= BLOCK, t, NEG_INF)

    lane = lax.broadcasted_iota(jnp.int32, (BLOCK, LANES), 1)
    first = lane < HEAD_DIM
    w2 = w2_ref[...]
    qrow = lax.broadcasted_iota(jnp.int32, (2 * BLOCK, BLOCK), 0) % BLOCK
    kcol = lax.broadcasted_iota(jnp.int32, (2 * BLOCK, BLOCK), 1)
    diag = kcol < qrow

    def phase1(with_prev):
        items = []
        for p in range(PAIRS):
            cols = slice(p * LANES, (p + 1) * LANES)
            q = q_ref[:, cols]
            zq = jnp.zeros_like(q)
            qq = jnp.concatenate([jnp.where(first, q, zq), jnp.where(first, zq, q)], axis=0)
            qq_ref[p] = qq
            kd_ref[p] = kp2_ref[:, cols]
            vd_ref[p] = vp2_ref[:, cols]
            if with_prev:
                kb = jnp.concatenate([kp_ref[:, cols], kc_ref[:, cols]], axis=0)
                vb = jnp.concatenate([vp_ref[:, cols], vc_ref[:, cols]], axis=0)
                masks = [None, diag]
            else:
                kb, vb, masks = kc_ref[:, cols], vc_ref[:, cols], [diag]
            items.append((qq, kb, vb, None, masks))
        results = []
        _interleave(
            _swa_stages(qa_ref, kap_ref, kac_ref, vap_ref, vac_ref, bias_ref, 0 if with_prev else 1,
                        sink_ref, ga_ref, yt_ref, o_ref),
            _sb_stages(items, w2, results))
        for p, (acc, pv) in enumerate(results):
            acc_ref[p] = acc
            oacc_ref[p] = pv
            live_ref[p] = (jnp.max(acc) > PRUNE_LOG2).astype(jnp.int32)

    pl.when(n > 0)(functools.partial(phase1, True))
    pl.when(n == 0)(functools.partial(phase1, False))

    def pair_body(p, carry):
        def live():
            return (jnp.max(acc_ref[p]) > PRUNE_LOG2).astype(jnp.int32)

        def cond(c):
            j, go = c
            return jnp.logical_and(j >= 0, go > 0)

        def fetch(j, dst, col0, slot):
            src = proj_hbm.at[pl.ds(pl.multiple_of(j * BLOCK, BLOCK), BLOCK),
                              pl.ds(pl.multiple_of(col0 + p * LANES, LANES), LANES)]
            return pltpu.make_async_copy(src, dst.at[p], sem.at[slot])

        def body(c):
            j, _ = c

            @pl.when(j < n - 2)
            def _():
                ck = fetch(j, kd_ref, COL_KS, 0)
                cv = fetch(j, vd_ref, COL_VS, 1)
                ck.start()
                cv.start()
                ck.wait()
                cv.wait()

            results = []
            _interleave(_sb_stages([(qq_ref[p], kd_ref[p], vd_ref[p], acc_ref[p], [None])], w2, results))
            (acc, pv), = results
            acc_ref[p] = acc
            oacc_ref[p] += pv
            return j - 1, live()

        lax.while_loop(cond, body, (n - 2, live_ref[p]))
        return carry

    lax.fori_loop(0, PAIRS, pair_body, 0)

    ys = [jnp.where(first, oacc_ref[p, :BLOCK, :], oacc_ref[p, BLOCK:, :]) for p in range(PAIRS)]
    sq = functools.reduce(lambda a, b: a + b, [y * y for y in ys])
    inv = lax.rsqrt(jnp.sum(sq, axis=-1, keepdims=True) * (1.0 / SB_W) + EPS)
    for p in range(PAIRS):
        cols = slice(p * LANES, (p + 1) * LANES)
        o_ref[:, SWA_Q_W + p * LANES:SWA_Q_W + (p + 1) * LANES] = (ys[p] * inv * gb_ref[:, cols]).astype(BF16)


def _attention(proj, rel_bias, sinks, g_a, g_b):
    S = proj.shape[0]
    N = S // BLOCK
    bucket = jnp.asarray(_rel_bucket_table())
    w2 = jnp.asarray(_cumsum_weights(), dtype=BF16)
    back = lambda d: (lambda n: jnp.maximum(n - d, 0))
    wide = lambda rowf, c: pl.BlockSpec((BLOCK, SB_W), lambda n: (rowf(n), c))
    narrow = lambda rowf, c: pl.BlockSpec((BLOCK, LANES), lambda n: (rowf(n), c))
    smem = pl.BlockSpec(memory_space=pltpu.SMEM)
    return pl.pallas_call(
        _attn_kernel,
        out_shape=jax.ShapeDtypeStruct((S, D_MIX), BF16),
        grid=(N,),
        in_specs=[
            wide(back(0), WIDE_QA),
            narrow(back(1), COL_KA), narrow(back(0), COL_KA),
            narrow(back(1), COL_VA), narrow(back(0), COL_VA),
            _resident((2 * BLOCK, BLOCK)), smem, smem, _resident((1, SWA_Q_W)),
            wide(back(0), WIDE_QS),
            wide(back(1), WIDE_KS), wide(back(0), WIDE_KS),
            wide(back(1), WIDE_VS), wide(back(0), WIDE_VS),
            wide(back(2), WIDE_KS), wide(back(2), WIDE_VS),
            _resident((2 * BLOCK, 2 * BLOCK)), _resident((1, SB_W)),
            pl.BlockSpec(memory_space=pl.ANY),
        ],
        out_specs=pl.BlockSpec((BLOCK, D_MIX), lambda n: (n, 0)),
        scratch_shapes=[
            pltpu.VMEM((2, len(SWA_PAIRS), 2 * BLOCK, 2 * BLOCK), F32),
            pltpu.VMEM((SWA_Q_W, BLOCK), F32),
            pltpu.VMEM((PAIRS, 2 * BLOCK, LANES), BF16),
            pltpu.VMEM((PAIRS, BLOCK, LANES), BF16),
            pltpu.VMEM((PAIRS, BLOCK, LANES), BF16),
            pltpu.VMEM((PAIRS, 2 * BLOCK, BLOCK), F32),
            pltpu.VMEM((PAIRS, 2 * BLOCK, LANES), F32),
            pltpu.SMEM((PAIRS,), jnp.int32),
            pltpu.SemaphoreType.DMA((2,)),
        ],
        compiler_params=_params(("arbitrary",)),
        name="attention",
    )(proj, proj, proj, proj, proj, bucket, rel_bias, sinks, g_a,
      proj, proj, proj, proj, proj, proj, proj, w2, g_b, proj)


def _outproj_kernel(mix_ref, x_ref, gm_ref, w_ref, h_ref, hn_ref):
    h = x_ref[...] + jnp.dot(mix_ref[...], w_ref[...], preferred_element_type=F32)
    h_ref[...] = h
    hn_ref[...] = (_rms(h) * gm_ref[...]).astype(BF16)


def _outproj(mix, x, gm, w_bf16, tm=OUTPROJ_TM):
    S, D = x.shape
    row = lambda i: (i, 0)
    return pl.pallas_call(
        _outproj_kernel,
        out_shape=(jax.ShapeDtypeStruct((S, D), F32), jax.ShapeDtypeStruct((S, D), BF16)),
        grid=(S // tm,),
        in_specs=[
            pl.BlockSpec((tm, D_MIX), row),
            pl.BlockSpec((tm, D), row),
            _resident((1, D)),
            _resident((D_MIX, D)),
        ],
        out_specs=(pl.BlockSpec((tm, D), row), pl.BlockSpec((tm, D), row)),
        compiler_params=_params(("arbitrary",), CONVGLU_VMEM_LIMIT),
        name="outproj",
    )(mix, x, gm, w_bf16)


def _convglu_kernel(hn_ref, wg_ref, wv_ref, wc_ref, bc_ref, wd_ref, h_ref, gf_ref,
                    o_ref, gate_ref, tail_ref, *, tm):
    i = pl.program_id(0)
    f = pl.program_id(1)

    @pl.when(f == 0)
    def _():
        o_ref[...] = h_ref[...]

    @pl.when(i == 0)
    def _():
        tail_ref[f] = jnp.zeros(tail_ref.shape[1:], F32)

    hn = hn_ref[...]
    gate_ref[:TAIL, :] = tail_ref[f]
    gate_ref[TAIL:, :] = jnp.dot(hn, wg_ref[...], preferred_element_type=F32)
    tail_ref[f] = gate_ref[tm:, :]
    val = jnp.dot(hn, wv_ref[...], preferred_element_type=F32)
    gc = bc_ref[...]
    for tap in range(CONV_WIDTH):
        off = TAIL - (CONV_WIDTH - 1) + tap
        gc = gc + gate_ref[pl.ds(off, tm), :] * wc_ref[tap:tap + 1, :]
    act = (gc * (1.0 / (1.0 + jnp.exp(-gc))) * val).astype(BF16)
    o_ref[...] += jnp.dot(act, wd_ref[...], preferred_element_type=F32)

    @pl.when(f == pl.num_programs(1) - 1)
    def _():
        o_ref[...] = _rms(o_ref[...]) * gf_ref[...]


def _convglu(hn2, h1, w_up_chunks, w_conv, b_conv, w_down_bf16, g_final, tm=CONVGLU_TM):
    S, D = h1.shape
    tf = w_up_chunks.shape[-1]
    nf = D_FF // tf
    return pl.pallas_call(
        functools.partial(_convglu_kernel, tm=tm),
        out_shape=jax.ShapeDtypeStruct((S, D), F32),
        grid=(S // tm, nf),
        in_specs=[
            pl.BlockSpec((tm, D), lambda i, f: (i, 0)),
            pl.BlockSpec((None, D, tf), lambda i, f: (f, 0, 0)),
            pl.BlockSpec((None, D, tf), lambda i, f: (nf + f, 0, 0)),
            pl.BlockSpec((CONV_WIDTH, tf), lambda i, f: (0, f)),
            pl.BlockSpec((1, tf), lambda i, f: (0, f)),
            pl.BlockSpec((tf, D), lambda i, f: (f, 0)),
            pl.BlockSpec((tm, D), lambda i, f: (i, 0)),
            _resident((1, D)),
        ],
        out_specs=pl.BlockSpec((tm, D), lambda i, f: (i, 0)),
        scratch_shapes=[
            pltpu.VMEM((TAIL + tm, tf), F32),
            pltpu.VMEM((nf, TAIL, tf), F32),
        ],
        compiler_params=_params(("arbitrary", "arbitrary"), CONVGLU_VMEM_LIMIT),
        name="convglu",
    )(hn2, w_up_chunks, w_up_chunks, w_conv, b_conv, w_down_bf16, h1, g_final)


def kernel(x, w_in, g_attn_norm, rel_bias, swa_sinks, g_swa_out, g_sb_out, w_out,
           g_mlp_norm, w_up, w_conv, b_conv, w_down, g_final):
    B, S, D = x.shape
    assert (B, S, D) == (1, SEQ, D_MODEL)
    x2 = x.reshape(S, D)

    col = np.ones((1, D_IN), np.float32)
    col[:, REF_SPLITS[0]:REF_SPLITS[1]] = SCALE
    col[:, REF_SPLITS[3]:REF_SPLITS[4]] = SCALE
    w_in_s = (w_in * jnp.asarray(col)).astype(BF16)
    w_in_b = jnp.concatenate([w_in_s[:, REF_SPLITS[i]:REF_SPLITS[i + 1]] for i in PERM_GROUPS], axis=1)

    proj, (w_out_b, w_up_b, w_down_b) = _inproj(x2, g_attn_norm.reshape(1, D), w_in_b,
                                                ((w_out, None), (w_up, CONVGLU_TF), (w_down, None)))
    mix = _attention(proj, rel_bias, swa_sinks, g_swa_out.reshape(1, -1), g_sb_out.reshape(1, -1))
    h1, hn2 = _outproj(mix, x2, g_mlp_norm.reshape(1, D), w_out_b)
    out = _convglu(hn2, h1, w_up_b, w_conv, b_conv.reshape(1, -1), w_down_b, g_final.reshape(1, D))
    return out.reshape(B, S, D)
```

```python
import functools
import math

import numpy as np
import jax
import jax.numpy as jnp
from jax import lax
from jax.experimental import pallas as pl
from jax.experimental.pallas import tpu as pltpu

D_MODEL = 2048
SEQ = 16384
HEAD_DIM = 64
SWA_Q_HEADS = 16
SWA_KV_HEADS = 2
SWA_GROUP = SWA_Q_HEADS // SWA_KV_HEADS
SB_HEADS = 16
WINDOW = 128
BLOCK = 128
REL_BUCKETS = 32
REL_MAX_DIST = 128
D_FF = 5632
CONV_WIDTH = 3
EPS = 1e-6
NEG_INF = -1e30

SWA_Q_W = SWA_Q_HEADS * HEAD_DIM
SWA_KV_W = SWA_KV_HEADS * HEAD_DIM
SB_W = SB_HEADS * HEAD_DIM
D_MIX = SWA_Q_W + SB_W
D_IN = SWA_Q_W + 2 * SWA_KV_W + 3 * SB_W

LANES = 128
REF_SPLITS = np.cumsum([0, SWA_Q_W, SWA_KV_W, SWA_KV_W, SB_W, SB_W, SB_W])
COL_QA, COL_KA, COL_VA, COL_QS, COL_KS, COL_VS = (int(c) for c in REF_SPLITS[:-1])
PAIRS = SB_W // LANES
SWA_PAIRS = tuple((h, h + 2) for g in range(SWA_KV_HEADS) for par in (0, 1)
                  for h in range(g * SWA_GROUP + par, (g + 1) * SWA_GROUP, 4))

SCALE = HEAD_DIM ** -0.5
TAIL = 8

V7X_VMEM_BYTES = 64 * 1024 * 1024
VMEM_LIMIT = V7X_VMEM_BYTES - 8 * 1024 * 1024
INPROJ_TM = 512
OUTPROJ_TM = 512
CONVGLU_TM = 1024
CONVGLU_TF = 512
CONVGLU_VMEM_LIMIT = V7X_VMEM_BYTES - 2 * 1024 * 1024

F32 = jnp.float32
BF16 = jnp.bfloat16

PRUNE_LOG = -88.0
LOG2E = math.log2(math.e)
PRUNE_LOG2 = PRUNE_LOG * LOG2E


def _params(sem, vmem=VMEM_LIMIT):
    return pltpu.CompilerParams(dimension_semantics=sem, vmem_limit_bytes=vmem)


def _rms(y):
    return y * lax.rsqrt(jnp.mean(y * y, axis=-1, keepdims=True) + EPS)


def _inproj_kernel(n_later, x_ref, g_ref, w_ref, *refs):
    srcs, o_ref, dsts = refs[:n_later], refs[n_later], refs[n_later + 1:]
    hn = (_rms(x_ref[...]) * g_ref[...]).astype(BF16)
    o_ref[...] = jnp.dot(hn, w_ref[...], preferred_element_type=F32).astype(BF16)
    for src, dst in zip(srcs, dsts):
        if len(dst.shape) == 2:
            dst[...] = src[...].astype(BF16)
        else:
            width = dst.shape[-1]
            for c in range(dst.shape[0]):
                dst[c] = src[:, c * width:(c + 1) * width].astype(BF16)


def _resident(shape):
    return pl.BlockSpec(shape, lambda *_: (0,) * len(shape), pipeline_mode=pl.Buffered(1))


def _inproj(x, g, w_bf16, later_weights, tm=INPROJ_TM):
    S, D = x.shape
    N = w_bf16.shape[1]
    steps = S // tm
    weights = [w for w, _ in later_weights]
    in_slabs = [pl.BlockSpec((w.shape[0] // steps, w.shape[1]), lambda i: (i, 0)) for w in weights]
    out_shapes, out_slabs = [], []
    for w, width in later_weights:
        rows, cols = w.shape
        if width is None:
            out_shapes.append(jax.ShapeDtypeStruct((rows, cols), BF16))
            out_slabs.append(pl.BlockSpec((rows // steps, cols), lambda i: (i, 0)))
        else:
            out_shapes.append(jax.ShapeDtypeStruct((cols // width, rows, width), BF16))
            out_slabs.append(pl.BlockSpec((cols // width, rows // steps, width), lambda i: (0, i, 0)))
    outs = pl.pallas_call(
        functools.partial(_inproj_kernel, len(weights)),
        out_shape=[jax.ShapeDtypeStruct((S, N), BF16)] + out_shapes,
        grid=(steps,),
        in_specs=[
            pl.BlockSpec((tm, D), lambda i: (i, 0)),
            _resident((1, D)),
            _resident((D, N)),
        ] + in_slabs,
        out_specs=[pl.BlockSpec((tm, N), lambda i: (i, 0))] + out_slabs,
        compiler_params=_params(("arbitrary",)),
        name="inproj",
    )(x, g, w_bf16, *weights)
    return outs[0], outs[1:]


def _rel_bucket_table():
    qi = np.arange(BLOCK, dtype=np.int64)[None, :]
    kj = np.arange(2 * BLOCK, dtype=np.int64)[:, None]
    dist = qi + BLOCK - kj
    in_win = (dist >= 0) & (dist < WINDOW)
    dc = np.clip(dist, 0, None)
    max_exact = REL_BUCKETS // 2
    d = np.maximum(dc, 1).astype(np.float32)
    large = max_exact + (np.log(d / np.float32(max_exact)) / np.float32(math.log(REL_MAX_DIST / max_exact))
                         * np.float32(REL_BUCKETS - max_exact)).astype(np.int32)
    large = np.minimum(large, REL_BUCKETS - 1)
    bucket = np.where(dc < max_exact, dc, large).astype(np.int32)
    return np.where(in_win, bucket, -1).astype(np.int32)


def _cumsum_weights():
    kk = np.arange(BLOCK)
    upper = (kk[:, None] > kk[None, :]).astype(np.float32)
    w = np.concatenate([upper, np.ones((BLOCK, BLOCK), np.float32)], axis=1)
    return np.concatenate([w, w], axis=0)


def _interleave(*stage_generators):
    pending = list(stage_generators)
    while pending:
        for g in list(pending):
            try:
                next(g)
            except StopIteration:
                pending.remove(g)


def _swa_stages(q_ref, kp_ref, kc_ref, vp_ref, vc_ref, bias_ref, variant, sink_ref, g_ref, yt_ref, o_ref):
    lane = lax.broadcasted_iota(jnp.int32, (BLOCK, LANES), 1)
    lo = lane < HEAD_DIM
    second = lax.broadcasted_iota(jnp.int32, (1, 2 * BLOCK), 1) >= BLOCK

    k2 = jnp.concatenate([kp_ref[...], kc_ref[...]], axis=0)
    k2s = jnp.concatenate([k2[:, HEAD_DIM:], k2[:, :HEAD_DIM]], axis=1)
    v2t = jnp.concatenate([vp_ref[...], vc_ref[...]], axis=0).T

    def masked_q(h):
        q = q_ref[:, (h // 2) * LANES:(h // 2 + 1) * LANES]
        keep = lo if h % 2 == 0 else jnp.logical_not(lo)
        return jnp.where(keep, q, jnp.zeros_like(q))

    idx = range(len(SWA_PAIRS))
    logits, sinks = [], []
    for i, (ha, hb) in enumerate(SWA_PAIRS):
        group = ha // SWA_GROUP
        keys = k2 if (ha % 2) == group else k2s
        qw = jnp.concatenate([masked_q(ha), masked_q(hb)], axis=0)
        st = lax.dot_general(keys, qw, (((1,), (1,)), ((), ())), preferred_element_type=F32)
        logits.append(st + bias_ref[variant, i])
        sinks.append(jnp.where(second, sink_ref[hb], sink_ref[ha]))
    yield
    ms = [jnp.maximum(jnp.max(logits[i], axis=0, keepdims=True), sinks[i]) for i in idx]
    ps = [jnp.exp(logits[i] - ms[i]) for i in idx]
    yield
    invs = [1.0 / (jnp.sum(ps[i], axis=0, keepdims=True) + jnp.exp(sinks[i] - ms[i])) for i in idx]
    yield
    for i, (ha, hb) in enumerate(SWA_PAIRS):
        w = (ps[i] * invs[i]).astype(BF16)
        out = jnp.dot(v2t, w, preferred_element_type=F32)
        rows = slice((ha // SWA_GROUP) * HEAD_DIM, (ha // SWA_GROUP + 1) * HEAD_DIM)
        yt_ref[ha * HEAD_DIM:(ha + 1) * HEAD_DIM, :] = out[rows, :BLOCK]
        yt_ref[hb * HEAD_DIM:(hb + 1) * HEAD_DIM, :] = out[rows, BLOCK:]
    yield
    yt = yt_ref[...]
    inv = lax.rsqrt(jnp.mean(yt * yt, axis=0, keepdims=True) + EPS)
    o_ref[:, :SWA_Q_W] = ((yt * inv).T * g_ref[...]).astype(BF16)


def _sb_stages(items, w2, results):
    idx = range(len(items))
    nblk = items[0][1].shape[0] // BLOCK
    zs = [lax.dot_general(qq, kb, (((1,), (1,)), ((), ())), preferred_element_type=F32) * LOG2E
          for qq, kb, _, _, _ in items]
    yield
    lgs = [jnp.log2(1.0 + jnp.exp2(-jnp.abs(z))) for z in zs]
    logsigs = [jnp.minimum(zs[i], 0.0) - lgs[i] for i in idx]
    log1ms = [logsigs[i] - zs[i] for i in idx]
    yield
    runnings = [it[3] for it in items]
    parts = [[None] * nblk for _ in idx]
    for c in reversed(range(nblk)):
        cs = slice(c * BLOCK, (c + 1) * BLOCK)
        rs = []
        for i in idx:
            mask = items[i][4][c]
            l1 = log1ms[i][:, cs]
            if mask is not None:
                l1 = jnp.where(mask, l1, 0.0)
            hi = l1.astype(BF16)
            lo = (l1 - hi.astype(F32)).astype(BF16)
            rs.append(jnp.dot(jnp.concatenate([hi, lo], axis=1), w2, preferred_element_type=F32))
        yield
        for i in idx:
            mask = items[i][4][c]
            log_a = logsigs[i][:, cs] + rs[i][:, :BLOCK]
            if runnings[i] is not None:
                log_a = log_a + runnings[i]
            a = jnp.exp2(log_a)
            if mask is not None:
                a = jnp.where(mask, a, 0.0)
            parts[i][c] = a.astype(BF16)
            runnings[i] = rs[i][:, BLOCK:] if runnings[i] is None else runnings[i] + rs[i][:, BLOCK:]
        yield
    for i in idx:
        amat = parts[i][0] if nblk == 1 else jnp.concatenate(parts[i], axis=1)
        results.append((runnings[i], jnp.dot(amat, items[i][2], preferred_element_type=F32)))


def _attn_kernel(qa_ref, kap_ref, kac_ref, vap_ref, vac_ref, bucket_ref, relb_ref, sink_ref, ga_ref,
                 q_ref, kp_ref, kc_ref, vp_ref, vc_ref, kp2_ref, vp2_ref, w2_ref, gb_ref, proj_hbm,
                 o_ref, bias_ref, yt_ref, qq_ref, kd_ref, vd_ref, acc_ref, oacc_ref, live_ref, sem):
    n = pl.program_id(0)

    @pl.when(n == 0)
    def _():
        bucket = bucket_ref[...]
        krow = lax.broadcasted_iota(jnp.int32, bucket.shape, 0)
        for i, pair in enumerate(SWA_PAIRS):
            for side, h in enumerate(pair):
                t = jnp.full(bucket.shape, NEG_INF, F32)
                for r in range(REL_BUCKETS):
                    t = jnp.where(bucket == r, relb_ref[r, h], t)
                cols = slice(side * BLOCK, (side + 1) * BLOCK)
                bias_ref[0, i, :, cols] = t
                bias_ref[1, i, :, cols] = jnp.where(krow >= BLOCK, t, NEG_INF)

    lane = lax.broadcasted_iota(jnp.int32, (BLOCK, LANES), 1)
    first = lane < HEAD_DIM
    w2 = w2_ref[...]
    qrow = lax.broadcasted_iota(jnp.int32, (2 * BLOCK, BLOCK), 0) % BLOCK
    kcol = lax.broadcasted_iota(jnp.int32, (2 * BLOCK, BLOCK), 1)
    diag = kcol < qrow

    def phase1(with_prev):
        items = []
        for p in range(PAIRS):
            cols = slice(p * LANES, (p + 1) * LANES)
            q = q_ref[:, cols]
            zq = jnp.zeros_like(q)
            qq = jnp.concatenate([jnp.where(first, q, zq), jnp.where(first, zq, q)], axis=0)
            qq_ref[p] = qq
            kd_ref[p] = kp2_ref[:, cols]
            vd_ref[p] = vp2_ref[:, cols]
            if with_prev:
                kb = jnp.concatenate([kp_ref[:, cols], kc_ref[:, cols]], axis=0)
                vb = jnp.concatenate([vp_ref[:, cols], vc_ref[:, cols]], axis=0)
                masks = [None, diag]
            else:
                kb, vb, masks = kc_ref[:, cols], vc_ref[:, cols], [diag]
            items.append((qq, kb, vb, None, masks))
        results = []
        _interleave(
            _swa_stages(qa_ref, kap_ref, kac_ref, vap_ref, vac_ref, bias_ref, 0 if with_prev else 1,
                        sink_ref, ga_ref, yt_ref, o_ref),
            _sb_stages(items, w2, results))
        for p, (acc, pv) in enumerate(results):
            acc_ref[p] = acc
            oacc_ref[p] = pv
            live_ref[p] = (jnp.max(acc) > PRUNE_LOG2).astype(jnp.int32)

    pl.when(n > 0)(functools.partial(phase1, True))
    pl.when(n == 0)(functools.partial(phase1, False))

    def pair_body(p, carry):
        def live():
            return (jnp.max(acc_ref[p]) > PRUNE_LOG2).astype(jnp.int32)

        def cond(c):
            j, go = c
            return jnp.logical_and(j >= 0, go > 0)

        def fetch(j, dst, col0, slot):
            src = proj_hbm.at[pl.ds(pl.multiple_of(j * BLOCK, BLOCK), BLOCK),
                              pl.ds(pl.multiple_of(col0 + p * LANES, LANES), LANES)]
            return pltpu.make_async_copy(src, dst.at[p], sem.at[slot])

        def body(c):
            j, _ = c

            @pl.when(j < n - 2)
            def _():
                ck = fetch(j, kd_ref, COL_KS, 0)
                cv = fetch(j, vd_ref, COL_VS, 1)
                ck.start()
                cv.start()
                ck.wait()
                cv.wait()

            results = []
            _interleave(_sb_stages([(qq_ref[p], kd_ref[p], vd_ref[p], acc_ref[p], [None])], w2, results))
            (acc, pv), = results
            acc_ref[p] = acc
            oacc_ref[p] += pv
            return j - 1, live()

        lax.while_loop(cond, body, (n - 2, live_ref[p]))
        return carry

    lax.fori_loop(0, PAIRS, pair_body, 0)

    ys = [jnp.where(first, oacc_ref[p, :BLOCK, :], oacc_ref[p, BLOCK:, :]) for p in range(PAIRS)]
    sq = functools.reduce(lambda a, b: a + b, [y * y for y in ys])
    inv = lax.rsqrt(jnp.sum(sq, axis=-1, keepdims=True) * (1.0 / SB_W) + EPS)
    for p in range(PAIRS):
        cols = slice(p * LANES, (p + 1) * LANES)
        o_ref[:, SWA_Q_W + p * LANES:SWA_Q_W + (p + 1) * LANES] = (ys[p] * inv * gb_ref[:, cols]).astype(BF16)


def _attention(proj, rel_bias, sinks, g_a, g_b):
    S = proj.shape[0]
    N = S // BLOCK
    bucket = jnp.asarray(_rel_bucket_table())
    w2 = jnp.asarray(_cumsum_weights(), dtype=BF16)
    back = lambda d: (lambda n: jnp.maximum(n - d, 0))
    wide = lambda rowf, c: pl.BlockSpec((pl.Element(BLOCK), pl.Element(SB_W)), lambda n: (rowf(n) * BLOCK, c))
    narrow = lambda rowf, c: pl.BlockSpec((BLOCK, LANES), lambda n: (rowf(n), c // LANES))
    smem = pl.BlockSpec(memory_space=pltpu.SMEM)
    return pl.pallas_call(
        _attn_kernel,
        out_shape=jax.ShapeDtypeStruct((S, D_MIX), BF16),
        grid=(N,),
        in_specs=[
            wide(back(0), COL_QA),
            narrow(back(1), COL_KA), narrow(back(0), COL_KA),
            narrow(back(1), COL_VA), narrow(back(0), COL_VA),
            _resident((2 * BLOCK, BLOCK)), smem, smem, _resident((1, SWA_Q_W)),
            wide(back(0), COL_QS),
            wide(back(1), COL_KS), wide(back(0), COL_KS),
            wide(back(1), COL_VS), wide(back(0), COL_VS),
            wide(back(2), COL_KS), wide(back(2), COL_VS),
            _resident((2 * BLOCK, 2 * BLOCK)), _resident((1, SB_W)),
            pl.BlockSpec(memory_space=pl.ANY),
        ],
        out_specs=pl.BlockSpec((BLOCK, D_MIX), lambda n: (n, 0)),
        scratch_shapes=[
            pltpu.VMEM((2, len(SWA_PAIRS), 2 * BLOCK, 2 * BLOCK), F32),
            pltpu.VMEM((SWA_Q_W, BLOCK), F32),
            pltpu.VMEM((PAIRS, 2 * BLOCK, LANES), BF16),
            pltpu.VMEM((PAIRS, BLOCK, LANES), BF16),
            pltpu.VMEM((PAIRS, BLOCK, LANES), BF16),
            pltpu.VMEM((PAIRS, 2 * BLOCK, BLOCK), F32),
            pltpu.VMEM((PAIRS, 2 * BLOCK, LANES), F32),
            pltpu.SMEM((PAIRS,), jnp.int32),
            pltpu.SemaphoreType.DMA((2,)),
        ],
        compiler_params=_params(("arbitrary",)),
        name="attention",
    )(proj, proj, proj, proj, proj, bucket, rel_bias, sinks, g_a,
      proj, proj, proj, proj, proj, proj, proj, w2, g_b, proj)


def _outproj_kernel(mix_ref, x_ref, gm_ref, w_ref, h_ref, hn_ref):
    h = x_ref[...] + jnp.dot(mix_ref[...], w_ref[...], preferred_element_type=F32)
    h_ref[...] = h
    hn_ref[...] = (_rms(h) * gm_ref[...]).astype(BF16)


def _outproj(mix, x, gm, w_bf16, tm=OUTPROJ_TM):
    S, D = x.shape
    row = lambda i: (i, 0)
    return pl.pallas_call(
        _outproj_kernel,
        out_shape=(jax.ShapeDtypeStruct((S, D), F32), jax.ShapeDtypeStruct((S, D), BF16)),
        grid=(S // tm,),
        in_specs=[
            pl.BlockSpec((tm, D_MIX), row),
            pl.BlockSpec((tm, D), row),
            _resident((1, D)),
            _resident((D_MIX, D)),
        ],
        out_specs=(pl.BlockSpec((tm, D), row), pl.BlockSpec((tm, D), row)),
        compiler_params=_params(("arbitrary",)),
        name="outproj",
    )(mix, x, gm, w_bf16)


def _convglu_kernel(hn_ref, wg_ref, wv_ref, wc_ref, bc_ref, wd_ref, h_ref, gf_ref,
                    o_ref, gate_ref, tail_ref, *, tm):
    i = pl.program_id(0)
    f = pl.program_id(1)

    @pl.when(f == 0)
    def _():
        o_ref[...] = h_ref[...]

    @pl.when(i == 0)
    def _():
        tail_ref[f] = jnp.zeros(tail_ref.shape[1:], F32)

    hn = hn_ref[...]
    gate_ref[:TAIL, :] = tail_ref[f]
    gate_ref[TAIL:, :] = jnp.dot(hn, wg_ref[...], preferred_element_type=F32)
    tail_ref[f] = gate_ref[tm:, :]
    val = jnp.dot(hn, wv_ref[...], preferred_element_type=F32)
    gc = bc_ref[...]
    for tap in range(CONV_WIDTH):
        off = TAIL - (CONV_WIDTH - 1) + tap
        gc = gc + gate_ref[pl.ds(off, tm), :] * wc_ref[tap:tap + 1, :]
    act = (gc * (1.0 / (1.0 + jnp.exp(-gc))) * val).astype(BF16)
    o_ref[...] += jnp.dot(act, wd_ref[...], preferred_element_type=F32)

    @pl.when(f == pl.num_programs(1) - 1)
    def _():
        o_ref[...] = _rms(o_ref[...]) * gf_ref[...]


def _convglu(hn2, h1, w_up_chunks, w_conv, b_conv, w_down_bf16, g_final, tm=CONVGLU_TM):
    S, D = h1.shape
    tf = w_up_chunks.shape[-1]
    nf = D_FF // tf
    return pl.pallas_call(
        functools.partial(_convglu_kernel, tm=tm),
        out_shape=jax.ShapeDtypeStruct((S, D), F32),
        grid=(S // tm, nf),
        in_specs=[
            pl.BlockSpec((tm, D), lambda i, f: (i, 0)),
            pl.BlockSpec((None, D, tf), lambda i, f: (f, 0, 0)),
            pl.BlockSpec((None, D, tf), lambda i, f: (nf + f, 0, 0)),
            pl.BlockSpec((CONV_WIDTH, tf), lambda i, f: (0, f)),
            pl.BlockSpec((1, tf), lambda i, f: (0, f)),
            pl.BlockSpec((tf, D), lambda i, f: (f, 0)),
            pl.BlockSpec((tm, D), lambda i, f: (i, 0)),
            _resident((1, D)),
        ],
        out_specs=pl.BlockSpec((tm, D), lambda i, f: (i, 0)),
        scratch_shapes=[
            pltpu.VMEM((TAIL + tm, tf), F32),
            pltpu.VMEM((nf, TAIL, tf), F32),
        ],
        compiler_params=_params(("arbitrary", "arbitrary"), CONVGLU_VMEM_LIMIT),
        name="convglu",
    )(hn2, w_up_chunks, w_up_chunks, w_conv, b_conv, w_down_bf16, h1, g_final)


def kernel(x, w_in, g_attn_norm, rel_bias, swa_sinks, g_swa_out, g_sb_out, w_out,
           g_mlp_norm, w_up, w_conv, b_conv, w_down, g_final):
    B, S, D = x.shape
    assert (B, S, D) == (1, SEQ, D_MODEL)
    x2 = x.reshape(S, D)

    col = np.ones((1, D_IN), np.float32)
    col[:, COL_QA:COL_KA] = SCALE
    col[:, COL_QS:COL_KS] = SCALE
    w_in_b = (w_in * jnp.asarray(col)).astype(BF16)

    proj, (w_out_b, w_up_b, w_down_b) = _inproj(x2, g_attn_norm.reshape(1, D), w_in_b,
                                                ((w_out, None), (w_up, CONVGLU_TF), (w_down, None)))
    mix = _attention(proj, rel_bias, swa_sinks, g_swa_out.reshape(1, -1), g_sb_out.reshape(1, -1))
    h1, hn2 = _outproj(mix, x2, g_mlp_norm.reshape(1, D), w_out_b)
    out = _convglu(hn2, h1, w_up_b, w_conv, b_conv.reshape(1, -1), w_down_b, g_final.reshape(1, D))
    return out.reshape(B, S, D)
```

```python
import functools
import math

import numpy as np
import jax
import jax.numpy as jnp
from jax import lax
from jax.experimental import pallas as pl
from jax.experimental.pallas import tpu as pltpu

D_MODEL = 2048
SEQ = 16384
HEAD_DIM = 64
SWA_Q_HEADS = 16
SWA_KV_HEADS = 2
SWA_GROUP = SWA_Q_HEADS // SWA_KV_HEADS
SB_HEADS = 16
WINDOW = 128
BLOCK = 128
REL_BUCKETS = 32
REL_MAX_DIST = 128
D_FF = 5632
CONV_WIDTH = 3
EPS = 1e-6
NEG_INF = -1e30

SWA_Q_W = SWA_Q_HEADS * HEAD_DIM
SWA_KV_W = SWA_KV_HEADS * HEAD_DIM
SB_W = SB_HEADS * HEAD_DIM
D_MIX = SWA_Q_W + SB_W
D_IN = SWA_Q_W + 2 * SWA_KV_W + 3 * SB_W

LANES = 128
REF_SPLITS = np.cumsum([0, SWA_Q_W, SWA_KV_W, SWA_KV_W, SB_W, SB_W, SB_W])
COL_QA, COL_KA, COL_VA, COL_QS, COL_KS, COL_VS = (int(c) for c in REF_SPLITS[:-1])
PAIRS = SB_W // LANES
SWA_PAIRS = tuple((h, h + 2) for g in range(SWA_KV_HEADS) for par in (0, 1)
                  for h in range(g * SWA_GROUP + par, (g + 1) * SWA_GROUP, 4))

SCALE = HEAD_DIM ** -0.5
TAIL = 8

V7X_VMEM_BYTES = 64 * 1024 * 1024
VMEM_LIMIT = V7X_VMEM_BYTES - 8 * 1024 * 1024
INPROJ_TM = 512
OUTPROJ_TM = 512
CONVGLU_TM = 1024
CONVGLU_TF = 512
CONVGLU_VMEM_LIMIT = V7X_VMEM_BYTES - 2 * 1024 * 1024

F32 = jnp.float32
BF16 = jnp.bfloat16

PRUNE_LOG = -88.0
LOG2E = math.log2(math.e)
PRUNE_LOG2 = PRUNE_LOG * LOG2E


def _params(sem, vmem=VMEM_LIMIT):
    return pltpu.CompilerParams(dimension_semantics=sem, vmem_limit_bytes=vmem)


def _rms(y):
    return y * lax.rsqrt(jnp.mean(y * y, axis=-1, keepdims=True) + EPS)


def _inproj_kernel(n_later, x_ref, g_ref, w_ref, *refs):
    srcs, o_ref, dsts = refs[:n_later], refs[n_later], refs[n_later + 1:]
    hn = (_rms(x_ref[...]) * g_ref[...]).astype(BF16)
    o_ref[...] = jnp.dot(hn, w_ref[...], preferred_element_type=F32).astype(BF16)
    for src, dst in zip(srcs, dsts):
        dst[...] = src[...].astype(BF16)


def _resident(shape):
    return pl.BlockSpec(shape, lambda *_: (0,) * len(shape), pipeline_mode=pl.Buffered(1))


def _inproj(x, g, w_bf16, later_weights, tm=INPROJ_TM):
    S, D = x.shape
    N = w_bf16.shape[1]
    steps = S // tm
    slabs = [pl.BlockSpec((w.shape[0] // steps, w.shape[1]), lambda i: (i, 0)) for w in later_weights]
    outs = pl.pallas_call(
        functools.partial(_inproj_kernel, len(later_weights)),
        out_shape=[jax.ShapeDtypeStruct((S, N), BF16)]
        + [jax.ShapeDtypeStruct(w.shape, BF16) for w in later_weights],
        grid=(steps,),
        in_specs=[
            pl.BlockSpec((tm, D), lambda i: (i, 0)),
            _resident((1, D)),
            _resident((D, N)),
        ] + slabs,
        out_specs=[pl.BlockSpec((tm, N), lambda i: (i, 0))] + slabs,
        compiler_params=_params(("arbitrary",)),
        name="inproj",
    )(x, g, w_bf16, *later_weights)
    return outs[0], outs[1:]


def _rel_bucket_table():
    qi = np.arange(BLOCK, dtype=np.int64)[None, :]
    kj = np.arange(2 * BLOCK, dtype=np.int64)[:, None]
    dist = qi + BLOCK - kj
    in_win = (dist >= 0) & (dist < WINDOW)
    dc = np.clip(dist, 0, None)
    max_exact = REL_BUCKETS // 2
    d = np.maximum(dc, 1).astype(np.float32)
    large = max_exact + (np.log(d / np.float32(max_exact)) / np.float32(math.log(REL_MAX_DIST / max_exact))
                         * np.float32(REL_BUCKETS - max_exact)).astype(np.int32)
    large = np.minimum(large, REL_BUCKETS - 1)
    bucket = np.where(dc < max_exact, dc, large).astype(np.int32)
    return np.where(in_win, bucket, -1).astype(np.int32)


def _cumsum_weights():
    kk = np.arange(BLOCK)
    upper = (kk[:, None] > kk[None, :]).astype(np.float32)
    w = np.concatenate([upper, np.ones((BLOCK, BLOCK), np.float32)], axis=1)
    return np.concatenate([w, w], axis=0)


def _interleave(*stage_generators):
    pending = list(stage_generators)
    while pending:
        for g in list(pending):
            try:
                next(g)
            except StopIteration:
                pending.remove(g)


def _swa_stages(q_ref, kp_ref, kc_ref, vp_ref, vc_ref, bias_ref, variant, sink_ref, g_ref, yt_ref, o_ref):
    lane = lax.broadcasted_iota(jnp.int32, (BLOCK, LANES), 1)
    lo = lane < HEAD_DIM
    second = lax.broadcasted_iota(jnp.int32, (1, 2 * BLOCK), 1) >= BLOCK

    k2 = jnp.concatenate([kp_ref[...], kc_ref[...]], axis=0)
    k2s = jnp.concatenate([k2[:, HEAD_DIM:], k2[:, :HEAD_DIM]], axis=1)
    v2t = jnp.concatenate([vp_ref[...], vc_ref[...]], axis=0).T

    def masked_q(h):
        q = q_ref[:, (h // 2) * LANES:(h // 2 + 1) * LANES]
        keep = lo if h % 2 == 0 else jnp.logical_not(lo)
        return jnp.where(keep, q, jnp.zeros_like(q))

    idx = range(len(SWA_PAIRS))
    logits, sinks = [], []
    for i, (ha, hb) in enumerate(SWA_PAIRS):
        group = ha // SWA_GROUP
        keys = k2 if (ha % 2) == group else k2s
        qw = jnp.concatenate([masked_q(ha), masked_q(hb)], axis=0)
        st = lax.dot_general(keys, qw, (((1,), (1,)), ((), ())), preferred_element_type=F32)
        logits.append(st + bias_ref[variant, i])
        sinks.append(jnp.where(second, sink_ref[hb], sink_ref[ha]))
    yield
    ms = [jnp.maximum(jnp.max(logits[i], axis=0, keepdims=True), sinks[i]) for i in idx]
    ps = [jnp.exp(logits[i] - ms[i]) for i in idx]
    yield
    invs = [1.0 / (jnp.sum(ps[i], axis=0, keepdims=True) + jnp.exp(sinks[i] - ms[i])) for i in idx]
    yield
    for i, (ha, hb) in enumerate(SWA_PAIRS):
        w = (ps[i] * invs[i]).astype(BF16)
        out = jnp.dot(v2t, w, preferred_element_type=F32)
        rows = slice((ha // SWA_GROUP) * HEAD_DIM, (ha // SWA_GROUP + 1) * HEAD_DIM)
        yt_ref[ha * HEAD_DIM:(ha + 1) * HEAD_DIM, :] = out[rows, :BLOCK]
        yt_ref[hb * HEAD_DIM:(hb + 1) * HEAD_DIM, :] = out[rows, BLOCK:]
    yield
    yt = yt_ref[...]
    inv = lax.rsqrt(jnp.mean(yt * yt, axis=0, keepdims=True) + EPS)
    o_ref[:, :SWA_Q_W] = ((yt * inv).T * g_ref[...]).astype(BF16)


def _sb_stages(items, w2, results):
    idx = range(len(items))
    nblk = items[0][1].shape[0] // BLOCK
    zs = [lax.dot_general(qq, kb, (((1,), (1,)), ((), ())), preferred_element_type=F32) * LOG2E
          for qq, kb, _, _, _ in items]
    yield
    lgs = [jnp.log2(1.0 + jnp.exp2(-jnp.abs(z))) for z in zs]
    logsigs = [jnp.minimum(zs[i], 0.0) - lgs[i] for i in idx]
    log1ms = [logsigs[i] - zs[i] for i in idx]
    yield
    runnings = [it[3] for it in items]
    parts = [[None] * nblk for _ in idx]
    for c in reversed(range(nblk)):
        cs = slice(c * BLOCK, (c + 1) * BLOCK)
        rs = []
        for i in idx:
            mask = items[i][4][c]
            l1 = log1ms[i][:, cs]
            if mask is not None:
                l1 = jnp.where(mask, l1, 0.0)
            hi = l1.astype(BF16)
            lo = (l1 - hi.astype(F32)).astype(BF16)
            rs.append(jnp.dot(jnp.concatenate([hi, lo], axis=1), w2, preferred_element_type=F32))
        yield
        for i in idx:
            mask = items[i][4][c]
            log_a = logsigs[i][:, cs] + rs[i][:, :BLOCK]
            if runnings[i] is not None:
                log_a = log_a + runnings[i]
            a = jnp.exp2(log_a)
            if mask is not None:
                a = jnp.where(mask, a, 0.0)
            parts[i][c] = a.astype(BF16)
            runnings[i] = rs[i][:, BLOCK:] if runnings[i] is None else runnings[i] + rs[i][:, BLOCK:]
        yield
    for i in idx:
        amat = parts[i][0] if nblk == 1 else jnp.concatenate(parts[i], axis=1)
        results.append((runnings[i], jnp.dot(amat, items[i][2], preferred_element_type=F32)))


def _attn_kernel(qa_ref, kap_ref, kac_ref, vap_ref, vac_ref, bucket_ref, relb_ref, sink_ref, ga_ref,
                 q_ref, kp_ref, kc_ref, vp_ref, vc_ref, kp2_ref, vp2_ref, w2_ref, gb_ref, proj_hbm,
                 o_ref, bias_ref, yt_ref, qq_ref, kd_ref, vd_ref, acc_ref, oacc_ref, live_ref, sem):
    n = pl.program_id(0)

    @pl.when(n == 0)
    def _():
        bucket = bucket_ref[...]
        krow = lax.broadcasted_iota(jnp.int32, bucket.shape, 0)
        for i, pair in enumerate(SWA_PAIRS):
            for side, h in enumerate(pair):
                t = jnp.full(bucket.shape, NEG_INF, F32)
                for r in range(REL_BUCKETS):
                    t = jnp.where(bucket == r, relb_ref[r, h], t)
                cols = slice(side * BLOCK, (side + 1) * BLOCK)
                bias_ref[0, i, :, cols] = t
                bias_ref[1, i, :, cols] = jnp.where(krow >= BLOCK, t, NEG_INF)

    lane = lax.broadcasted_iota(jnp.int32, (BLOCK, LANES), 1)
    first = lane < HEAD_DIM
    w2 = w2_ref[...]
    qrow = lax.broadcasted_iota(jnp.int32, (2 * BLOCK, BLOCK), 0) % BLOCK
    kcol = lax.broadcasted_iota(jnp.int32, (2 * BLOCK, BLOCK), 1)
    diag = kcol < qrow

    def phase1(with_prev):
        items = []
        for p in range(PAIRS):
            cols = slice(p * LANES, (p + 1) * LANES)
            q = q_ref[:, cols]
            zq = jnp.zeros_like(q)
            qq = jnp.concatenate([jnp.where(first, q, zq), jnp.where(first, zq, q)], axis=0)
            qq_ref[p] = qq
            kd_ref[p] = kp2_ref[:, cols]
            vd_ref[p] = vp2_ref[:, cols]
            if with_prev:
                kb = jnp.concatenate([kp_ref[:, cols], kc_ref[:, cols]], axis=0)
                vb = jnp.concatenate([vp_ref[:, cols], vc_ref[:, cols]], axis=0)
                masks = [None, diag]
            else:
                kb, vb, masks = kc_ref[:, cols], vc_ref[:, cols], [diag]
            items.append((qq, kb, vb, None, masks))
        results = []
        _interleave(
            _swa_stages(qa_ref, kap_ref, kac_ref, vap_ref, vac_ref, bias_ref, 0 if with_prev else 1,
                        sink_ref, ga_ref, yt_ref, o_ref),
            _sb_stages(items, w2, results))
        for p, (acc, pv) in enumerate(results):
            acc_ref[p] = acc
            oacc_ref[p] = pv
            live_ref[p] = (jnp.max(acc) > PRUNE_LOG2).astype(jnp.int32)

    pl.when(n > 0)(functools.partial(phase1, True))
    pl.when(n == 0)(functools.partial(phase1, False))

    def pair_body(p, carry):
        def live():
            return (jnp.max(acc_ref[p]) > PRUNE_LOG2).astype(jnp.int32)

        def cond(c):
            j, go = c
            return jnp.logical_and(j >= 0, go > 0)

        def fetch(j, dst, col0, slot):
            src = proj_hbm.at[pl.ds(pl.multiple_of(j * BLOCK, BLOCK), BLOCK),
                              pl.ds(pl.multiple_of(col0 + p * LANES, LANES), LANES)]
            return pltpu.make_async_copy(src, dst.at[p], sem.at[slot])

        def body(c):
            j, _ = c

            @pl.when(j < n - 2)
            def _():
                ck = fetch(j, kd_ref, COL_KS, 0)
                cv = fetch(j, vd_ref, COL_VS, 1)
                ck.start()
                cv.start()
                ck.wait()
                cv.wait()

            results = []
            _interleave(_sb_stages([(qq_ref[p], kd_ref[p], vd_ref[p], acc_ref[p], [None])], w2, results))
            (acc, pv), = results
            acc_ref[p] = acc
            oacc_ref[p] += pv
            return j - 1, live()

        lax.while_loop(cond, body, (n - 2, live_ref[p]))
        return carry

    lax.fori_loop(0, PAIRS, pair_body, 0)

    ys = [jnp.where(first, oacc_ref[p, :BLOCK, :], oacc_ref[p, BLOCK:, :]) for p in range(PAIRS)]
    sq = functools.reduce(lambda a, b: a + b, [y * y for y in ys])
    inv = lax.rsqrt(jnp.sum(sq, axis=-1, keepdims=True) * (1.0 / SB_W) + EPS)
    for p in range(PAIRS):
        cols = slice(p * LANES, (p + 1) * LANES)
        o_ref[:, SWA_Q_W + p * LANES:SWA_Q_W + (p + 1) * LANES] = (ys[p] * inv * gb_ref[:, cols]).astype(BF16)


def _attention(proj, rel_bias, sinks, g_a, g_b):
    S = proj.shape[0]
    N = S // BLOCK
    bucket = jnp.asarray(_rel_bucket_table())
    w2 = jnp.asarray(_cumsum_weights(), dtype=BF16)
    back = lambda d: (lambda n: jnp.maximum(n - d, 0))
    wide = lambda rowf, c: pl.BlockSpec((pl.Element(BLOCK), pl.Element(SB_W)), lambda n: (rowf(n) * BLOCK, c))
    narrow = lambda rowf, c: pl.BlockSpec((BLOCK, LANES), lambda n: (rowf(n), c // LANES))
    smem = pl.BlockSpec(memory_space=pltpu.SMEM)
    return pl.pallas_call(
        _attn_kernel,
        out_shape=jax.ShapeDtypeStruct((S, D_MIX), BF16),
        grid=(N,),
        in_specs=[
            wide(back(0), COL_QA),
            narrow(back(1), COL_KA), narrow(back(0), COL_KA),
            narrow(back(1), COL_VA), narrow(back(0), COL_VA),
            _resident((2 * BLOCK, BLOCK)), smem, smem, _resident((1, SWA_Q_W)),
            wide(back(0), COL_QS),
            wide(back(1), COL_KS), wide(back(0), COL_KS),
            wide(back(1), COL_VS), wide(back(0), COL_VS),
            wide(back(2), COL_KS), wide(back(2), COL_VS),
            _resident((2 * BLOCK, 2 * BLOCK)), _resident((1, SB_W)),
            pl.BlockSpec(memory_space=pl.ANY),
        ],
        out_specs=pl.BlockSpec((BLOCK, D_MIX), lambda n: (n, 0)),
        scratch_shapes=[
            pltpu.VMEM((2, len(SWA_PAIRS), 2 * BLOCK, 2 * BLOCK), F32),
            pltpu.VMEM((SWA_Q_W, BLOCK), F32),
            pltpu.VMEM((PAIRS, 2 * BLOCK, LANES), BF16),
            pltpu.VMEM((PAIRS, BLOCK, LANES), BF16),
            pltpu.VMEM((PAIRS, BLOCK, LANES), BF16),
            pltpu.VMEM((PAIRS, 2 * BLOCK, BLOCK), F32),
            pltpu.VMEM((PAIRS, 2 * BLOCK, LANES), F32),
            pltpu.SMEM((PAIRS,), jnp.int32),
            pltpu.SemaphoreType.DMA((2,)),
        ],
        compiler_params=_params(("arbitrary",)),
        name="attention",
    )(proj, proj, proj, proj, proj, bucket, rel_bias, sinks, g_a,
      proj, proj, proj, proj, proj, proj, proj, w2, g_b, proj)


def _outproj_kernel(mix_ref, x_ref, gm_ref, w_ref, h_ref, hn_ref):
    h = x_ref[...] + jnp.dot(mix_ref[...], w_ref[...], preferred_element_type=F32)
    h_ref[...] = h
    hn_ref[...] = (_rms(h) * gm_ref[...]).astype(BF16)


def _outproj(mix, x, gm, w_bf16, tm=OUTPROJ_TM):
    S, D = x.shape
    row = lambda i: (i, 0)
    return pl.pallas_call(
        _outproj_kernel,
        out_shape=(jax.ShapeDtypeStruct((S, D), F32), jax.ShapeDtypeStruct((S, D), BF16)),
        grid=(S // tm,),
        in_specs=[
            pl.BlockSpec((tm, D_MIX), row),
            pl.BlockSpec((tm, D), row),
            _resident((1, D)),
            _resident((D_MIX, D)),
        ],
        out_specs=(pl.BlockSpec((tm, D), row), pl.BlockSpec((tm, D), row)),
        compiler_params=_params(("arbitrary",)),
        name="outproj",
    )(mix, x, gm, w_bf16)


def _convglu_kernel(hn_ref, wg_ref, wv_ref, wc_ref, bc_ref, wd_ref, h_ref, gf_ref,
                    o_ref, gate_ref, tail_ref, *, tm):
    i = pl.program_id(0)
    f = pl.program_id(1)

    @pl.when(f == 0)
    def _():
        o_ref[...] = h_ref[...]

    @pl.when(i == 0)
    def _():
        tail_ref[f] = jnp.zeros(tail_ref.shape[1:], F32)

    hn = hn_ref[...]
    gate_ref[:TAIL, :] = tail_ref[f]
    gate_ref[TAIL:, :] = jnp.dot(hn, wg_ref[...], preferred_element_type=F32)
    tail_ref[f] = gate_ref[tm:, :]
    val = jnp.dot(hn, wv_ref[...], preferred_element_type=F32)
    gc = bc_ref[...]
    for tap in range(CONV_WIDTH):
        off = TAIL - (CONV_WIDTH - 1) + tap
        gc = gc + gate_ref[pl.ds(off, tm), :] * wc_ref[tap:tap + 1, :]
    act = (gc * (1.0 / (1.0 + jnp.exp(-gc))) * val).astype(BF16)
    o_ref[...] += jnp.dot(act, wd_ref[...], preferred_element_type=F32)

    @pl.when(f == pl.num_programs(1) - 1)
    def _():
        o_ref[...] = _rms(o_ref[...]) * gf_ref[...]


def _convglu(hn2, h1, w_up_bf16, w_conv, b_conv, w_down_bf16, g_final, tm=CONVGLU_TM, tf=CONVGLU_TF):
    S, D = h1.shape
    nf = D_FF // tf
    return pl.pallas_call(
        functools.partial(_convglu_kernel, tm=tm),
        out_shape=jax.ShapeDtypeStruct((S, D), F32),
        grid=(S // tm, nf),
        in_specs=[
            pl.BlockSpec((tm, D), lambda i, f: (i, 0)),
            pl.BlockSpec((D, tf), lambda i, f: (0, f)),
            pl.BlockSpec((D, tf), lambda i, f: (0, nf + f)),
            pl.BlockSpec((CONV_WIDTH, tf), lambda i, f: (0, f)),
            pl.BlockSpec((1, tf), lambda i, f: (0, f)),
            pl.BlockSpec((tf, D), lambda i, f: (f, 0)),
            pl.BlockSpec((tm, D), lambda i, f: (i, 0)),
            _resident((1, D)),
        ],
        out_specs=pl.BlockSpec((tm, D), lambda i, f: (i, 0)),
        scratch_shapes=[
            pltpu.VMEM((TAIL + tm, tf), F32),
            pltpu.VMEM((nf, TAIL, tf), F32),
        ],
        compiler_params=_params(("arbitrary", "arbitrary"), CONVGLU_VMEM_LIMIT),
        name="convglu",
    )(hn2, w_up_bf16, w_up_bf16, w_conv, b_conv, w_down_bf16, h1, g_final)


def kernel(x, w_in, g_attn_norm, rel_bias, swa_sinks, g_swa_out, g_sb_out, w_out,
           g_mlp_norm, w_up, w_conv, b_conv, w_down, g_final):
    B, S, D = x.shape
    assert (B, S, D) == (1, SEQ, D_MODEL)
    x2 = x.reshape(S, D)

    col = np.ones((1, D_IN), np.float32)
    col[:, COL_QA:COL_KA] = SCALE
    col[:, COL_QS:COL_KS] = SCALE
    w_in_b = (w_in * jnp.asarray(col)).astype(BF16)

    proj, (w_out_b, w_up_b, w_down_b) = _inproj(x2, g_attn_norm.reshape(1, D), w_in_b, (w_out, w_up, w_down))
    mix = _attention(proj, rel_bias, swa_sinks, g_swa_out.reshape(1, -1), g_sb_out.reshape(1, -1))
    h1, hn2 = _outproj(mix, x2, g_mlp_norm.reshape(1, D), w_out_b)
    out = _convglu(hn2, h1, w_up_b, w_conv, b_conv.reshape(1, -1), w_down_b, g_final.reshape(1, D))
    return out.reshape(B, S, D)
```

```python
import functools
import math

import numpy as np
import jax
import jax.numpy as jnp
from jax import lax
from jax.experimental import pallas as pl
from jax.experimental.pallas import tpu as pltpu

D_MODEL = 2048
SEQ = 16384
HEAD_DIM = 64
SWA_Q_HEADS = 16
SWA_KV_HEADS = 2
SWA_GROUP = SWA_Q_HEADS // SWA_KV_HEADS
SB_HEADS = 16
WINDOW = 128
BLOCK = 128
REL_BUCKETS = 32
REL_MAX_DIST = 128
D_FF = 5632
CONV_WIDTH = 3
EPS = 1e-6
NEG_INF = -1e30

SWA_Q_W = SWA_Q_HEADS * HEAD_DIM
SWA_KV_W = SWA_KV_HEADS * HEAD_DIM
SB_W = SB_HEADS * HEAD_DIM
D_MIX = SWA_Q_W + SB_W
D_IN = SWA_Q_W + 2 * SWA_KV_W + 3 * SB_W

LANES = 128
REF_SPLITS = np.cumsum([0, SWA_Q_W, SWA_KV_W, SWA_KV_W, SB_W, SB_W, SB_W])
COL_QA, COL_KA, COL_VA, COL_QS, COL_KS, COL_VS = (int(c) for c in REF_SPLITS[:-1])
PAIRS = SB_W // LANES
SWA_PAIRS = tuple((h, h + 2) for g in range(SWA_KV_HEADS) for par in (0, 1)
                  for h in range(g * SWA_GROUP + par, (g + 1) * SWA_GROUP, 4))

SCALE = HEAD_DIM ** -0.5
TAIL = 8
ONES_ROWS = 16

V7X_VMEM_BYTES = 64 * 1024 * 1024
VMEM_LIMIT = V7X_VMEM_BYTES - 8 * 1024 * 1024
INPROJ_TM = 512
OUTPROJ_TM = 512
CONVGLU_TM = 1024
CONVGLU_TF = 512
CONVGLU_VMEM_LIMIT = V7X_VMEM_BYTES - 2 * 1024 * 1024

F32 = jnp.float32
BF16 = jnp.bfloat16

PRUNE_LOG = -88.0
LOG2E = math.log2(math.e)
PRUNE_LOG2 = PRUNE_LOG * LOG2E


def _params(sem, vmem=VMEM_LIMIT):
    return pltpu.CompilerParams(dimension_semantics=sem, vmem_limit_bytes=vmem)


def _rms(y):
    return y * lax.rsqrt(jnp.mean(y * y, axis=-1, keepdims=True) + EPS)


def _inproj_kernel(n_later, x_ref, g_ref, w_ref, *refs):
    srcs, o_ref, dsts = refs[:n_later], refs[n_later], refs[n_later + 1:]
    hn = (_rms(x_ref[...]) * g_ref[...]).astype(BF16)
    o_ref[...] = jnp.dot(hn, w_ref[...], preferred_element_type=F32).astype(BF16)
    for src, dst in zip(srcs, dsts):
        if len(dst.shape) == 2:
            dst[...] = src[...].astype(BF16)
        else:
            width = dst.shape[-1]
            for c in range(dst.shape[0]):
                dst[c] = src[:, c * width:(c + 1) * width].astype(BF16)


def _resident(shape):
    return pl.BlockSpec(shape, lambda *_: (0,) * len(shape), pipeline_mode=pl.Buffered(1))


def _inproj(x, g, w_bf16, later_weights, tm=INPROJ_TM):
    S, D = x.shape
    N = w_bf16.shape[1]
    steps = S // tm
    weights = [w for w, _ in later_weights]
    in_slabs = [pl.BlockSpec((w.shape[0] // steps, w.shape[1]), lambda i: (i, 0)) for w in weights]
    out_shapes, out_slabs = [], []
    for w, width in later_weights:
        rows, cols = w.shape
        if width is None:
            out_shapes.append(jax.ShapeDtypeStruct((rows, cols), BF16))
            out_slabs.append(pl.BlockSpec((rows // steps, cols), lambda i: (i, 0)))
        else:
            out_shapes.append(jax.ShapeDtypeStruct((cols // width, rows, width), BF16))
            out_slabs.append(pl.BlockSpec((cols // width, rows // steps, width), lambda i: (0, i, 0)))
    outs = pl.pallas_call(
        functools.partial(_inproj_kernel, len(weights)),
        out_shape=[jax.ShapeDtypeStruct((S, N), BF16)] + out_shapes,
        grid=(steps,),
        in_specs=[
            pl.BlockSpec((tm, D), lambda i: (i, 0)),
            _resident((1, D)),
            _resident((D, N)),
        ] + in_slabs,
        out_specs=[pl.BlockSpec((tm, N), lambda i: (i, 0))] + out_slabs,
        compiler_params=_params(("arbitrary",)),
        name="inproj",
    )(x, g, w_bf16, *weights)
    return outs[0], outs[1:]


def _rel_bucket_table():
    qi = np.arange(BLOCK, dtype=np.int64)[None, :]
    kj = np.arange(2 * BLOCK, dtype=np.int64)[:, None]
    dist = qi + BLOCK - kj
    in_win = (dist >= 0) & (dist < WINDOW)
    dc = np.clip(dist, 0, None)
    max_exact = REL_BUCKETS // 2
    d = np.maximum(dc, 1).astype(np.float32)
    large = max_exact + (np.log(d / np.float32(max_exact)) / np.float32(math.log(REL_MAX_DIST / max_exact))
                         * np.float32(REL_BUCKETS - max_exact)).astype(np.int32)
    large = np.minimum(large, REL_BUCKETS - 1)
    bucket = np.where(dc < max_exact, dc, large).astype(np.int32)
    return np.where(in_win, bucket, -1).astype(np.int32)


def _cumsum_weights():
    kk = np.arange(BLOCK)
    upper = (kk[:, None] > kk[None, :]).astype(np.float32)
    w = np.concatenate([upper, np.ones((BLOCK, BLOCK), np.float32)], axis=1)
    return np.concatenate([w, w], axis=0)


def _interleave(*stage_generators):
    pending = list(stage_generators)
    while pending:
        for g in list(pending):
            try:
                next(g)
            except StopIteration:
                pending.remove(g)


def _swa_stages(q_ref, kp_ref, kc_ref, vp_ref, vc_ref, bias_ref, variant, sink_ref, g_ref, yt_ref, o_ref):
    lane = lax.broadcasted_iota(jnp.int32, (BLOCK, LANES), 1)
    lo = lane < HEAD_DIM
    second = lax.broadcasted_iota(jnp.int32, (1, 2 * BLOCK), 1) >= BLOCK

    k2 = jnp.concatenate([kp_ref[...], kc_ref[...]], axis=0)
    k2s = jnp.concatenate([k2[:, HEAD_DIM:], k2[:, :HEAD_DIM]], axis=1)
    v2t = jnp.concatenate([vp_ref[...], vc_ref[...]], axis=0).T
    v2t = jnp.concatenate([v2t, jnp.ones((ONES_ROWS, 2 * BLOCK), BF16)], axis=0)

    def masked_q(h):
        q = q_ref[:, (h // 2) * LANES:(h // 2 + 1) * LANES]
        keep = lo if h % 2 == 0 else jnp.logical_not(lo)
        return jnp.where(keep, q, jnp.zeros_like(q))

    idx = range(len(SWA_PAIRS))
    logits, sinks = [], []
    for i, (ha, hb) in enumerate(SWA_PAIRS):
        group = ha // SWA_GROUP
        keys = k2 if (ha % 2) == group else k2s
        qw = jnp.concatenate([masked_q(ha), masked_q(hb)], axis=0)
        st = lax.dot_general(keys, qw, (((1,), (1,)), ((), ())), preferred_element_type=F32)
        logits.append(st + bias_ref[variant, i])
        sinks.append(jnp.where(second, sink_ref[hb], sink_ref[ha]))
    yield
    ms = [jnp.maximum(jnp.max(logits[i], axis=0, keepdims=True), sinks[i]) for i in idx]
    ps = [jnp.exp(logits[i] - ms[i]) for i in idx]
    yield
    outs = [jnp.dot(v2t, ps[i].astype(BF16), preferred_element_type=F32) for i in idx]
    yield
    for i, (ha, hb) in enumerate(SWA_PAIRS):
        denom = outs[i][2 * HEAD_DIM:2 * HEAD_DIM + 1, :] + jnp.exp(sinks[i] - ms[i])
        rows = slice((ha // SWA_GROUP) * HEAD_DIM, (ha // SWA_GROUP + 1) * HEAD_DIM)
        y = outs[i][rows, :] * (1.0 / denom)
        yt_ref[ha * HEAD_DIM:(ha + 1) * HEAD_DIM, :] = y[:, :BLOCK]
        yt_ref[hb * HEAD_DIM:(hb + 1) * HEAD_DIM, :] = y[:, BLOCK:]
    yield
    yt = yt_ref[...]
    inv = lax.rsqrt(jnp.mean(yt * yt, axis=0, keepdims=True) + EPS)
    o_ref[:, :SWA_Q_W] = ((yt * inv).T * g_ref[...]).astype(BF16)


def _sb_stages(items, w2, results):
    idx = range(len(items))
    nblk = items[0][1].shape[0] // BLOCK
    zs = []
    for qq, kb, _, _, masks in items:
        z = lax.dot_general(qq, kb, (((1,), (1,)), ((), ())), preferred_element_type=F32) * LOG2E
        if any(m is not None for m in masks):
            z = jnp.concatenate([z[:, c * BLOCK:(c + 1) * BLOCK] if m is None
                                 else jnp.where(m, z[:, c * BLOCK:(c + 1) * BLOCK], NEG_INF)
                                 for c, m in enumerate(masks)], axis=1)
        zs.append(z)
    yield
    lgs = [jnp.log2(1.0 + jnp.exp2(-jnp.abs(z))) for z in zs]
    logsigs = [jnp.minimum(zs[i], 0.0) - lgs[i] for i in idx]
    log1ms = [logsigs[i] - zs[i] for i in idx]
    yield
    runnings = [it[3] for it in items]
    parts = [[None] * nblk for _ in idx]
    for c in reversed(range(nblk)):
        cs = slice(c * BLOCK, (c + 1) * BLOCK)
        rs = []
        for i in idx:
            l1 = log1ms[i][:, cs]
            hi = l1.astype(BF16)
            lo = (l1 - hi.astype(F32)).astype(BF16)
            rs.append(jnp.dot(jnp.concatenate([hi, lo], axis=1), w2, preferred_element_type=F32))
        yield
        for i in idx:
            log_a = logsigs[i][:, cs] + rs[i][:, :BLOCK]
            if runnings[i] is not None:
                log_a = log_a + runnings[i]
            parts[i][c] = jnp.exp2(log_a).astype(BF16)
            runnings[i] = rs[i][:, BLOCK:] if runnings[i] is None else runnings[i] + rs[i][:, BLOCK:]
        yield
    for i in idx:
        amat = parts[i][0] if nblk == 1 else jnp.concatenate(parts[i], axis=1)
        results.append((runnings[i], jnp.dot(amat, items[i][2], preferred_element_type=F32)))


def _attn_kernel(qa_ref, kap_ref, kac_ref, vap_ref, vac_ref, bucket_ref, relb_ref, sink_ref, ga_ref,
                 q_ref, kp_ref, kc_ref, vp_ref, vc_ref, kp2_ref, vp2_ref, w2_ref, gb_ref, proj_hbm,
                 o_ref, bias_ref, yt_ref, qq_ref, kd_ref, vd_ref, acc_ref, oacc_ref, live_ref, sem):
    n = pl.program_id(0)

    @pl.when(n == 0)
    def _():
        bucket = bucket_ref[...]
        krow = lax.broadcasted_iota(jnp.int32, bucket.shape, 0)
        for i, pair in enumerate(SWA_PAIRS):
            for side, h in enumerate(pair):
                t = jnp.full(bucket.shape, NEG_INF, F32)
                for r in range(REL_BUCKETS):
                    t = jnp.where(bucket == r, relb_ref[r, h], t)
                cols = slice(side * BLOCK, (side + 1) * BLOCK)
                bias_ref[0, i, :, cols] = t
                bias_ref[1, i, :, cols] = jnp.where(krow >= BLOCK, t, NEG_INF)

    lane = lax.broadcasted_iota(jnp.int32, (BLOCK, LANES), 1)
    first = lane < HEAD_DIM
    w2 = w2_ref[...]
    qrow = lax.broadcasted_iota(jnp.int32, (2 * BLOCK, BLOCK), 0) % BLOCK
    kcol = lax.broadcasted_iota(jnp.int32, (2 * BLOCK, BLOCK), 1)
    diag = kcol < qrow

    def phase1(with_prev):
        items = []
        for p in range(PAIRS):
            cols = slice(p * LANES, (p + 1) * LANES)
            q = q_ref[:, cols]
            zq = jnp.zeros_like(q)
            qq = jnp.concatenate([jnp.where(first, q, zq), jnp.where(first, zq, q)], axis=0)
            qq_ref[p] = qq
            kd_ref[p] = kp2_ref[:, cols]
            vd_ref[p] = vp2_ref[:, cols]
            if with_prev:
                kb = jnp.concatenate([kp_ref[:, cols], kc_ref[:, cols]], axis=0)
                vb = jnp.concatenate([vp_ref[:, cols], vc_ref[:, cols]], axis=0)
                masks = [None, diag]
            else:
                kb, vb, masks = kc_ref[:, cols], vc_ref[:, cols], [diag]
            items.append((qq, kb, vb, None, masks))
        results = []
        _interleave(
            _swa_stages(qa_ref, kap_ref, kac_ref, vap_ref, vac_ref, bias_ref, 0 if with_prev else 1,
                        sink_ref, ga_ref, yt_ref, o_ref),
            _sb_stages(items, w2, results))
        for p, (acc, pv) in enumerate(results):
            acc_ref[p] = acc
            oacc_ref[p] = pv
            live_ref[p] = (jnp.max(acc) > PRUNE_LOG2).astype(jnp.int32)

    pl.when(n > 0)(functools.partial(phase1, True))
    pl.when(n == 0)(functools.partial(phase1, False))

    def pair_body(p, carry):
        def live():
            return (jnp.max(acc_ref[p]) > PRUNE_LOG2).astype(jnp.int32)

        def cond(c):
            j, go = c
            return jnp.logical_and(j >= 0, go > 0)

        def fetch(j, dst, col0, slot):
            src = proj_hbm.at[pl.ds(pl.multiple_of(j * BLOCK, BLOCK), BLOCK),
                              pl.ds(pl.multiple_of(col0 + p * LANES, LANES), LANES)]
            return pltpu.make_async_copy(src, dst.at[p], sem.at[slot])

        def body(c):
            j, _ = c

            @pl.when(j < n - 2)
            def _():
                ck = fetch(j, kd_ref, COL_KS, 0)
                cv = fetch(j, vd_ref, COL_VS, 1)
                ck.start()
                cv.start()
                ck.wait()
                cv.wait()

            results = []
            _interleave(_sb_stages([(qq_ref[p], kd_ref[p], vd_ref[p], acc_ref[p], [None])], w2, results))
            (acc, pv), = results
            acc_ref[p] = acc
            oacc_ref[p] += pv
            return j - 1, live()

        lax.while_loop(cond, body, (n - 2, live_ref[p]))
        return carry

    lax.fori_loop(0, PAIRS, pair_body, 0)

    ys = [jnp.where(first, oacc_ref[p, :BLOCK, :], oacc_ref[p, BLOCK:, :]) for p in range(PAIRS)]
    sq = functools.reduce(lambda a, b: a + b, [y * y for y in ys])
    inv = lax.rsqrt(jnp.sum(sq, axis=-1, keepdims=True) * (1.0 / SB_W) + EPS)
    for p in range(PAIRS):
        cols = slice(p * LANES, (p + 1) * LANES)
        o_ref[:, SWA_Q_W + p * LANES:SWA_Q_W + (p + 1) * LANES] = (ys[p] * inv * gb_ref[:, cols]).astype(BF16)


def _attention(proj, rel_bias, sinks, g_a, g_b):
    S = proj.shape[0]
    N = S // BLOCK
    bucket = jnp.asarray(_rel_bucket_table())
    w2 = jnp.asarray(_cumsum_weights(), dtype=BF16)
    back = lambda d: (lambda n: jnp.maximum(n - d, 0))
    wide = lambda rowf, c: pl.BlockSpec((pl.Element(BLOCK), pl.Element(SB_W)), lambda n: (rowf(n) * BLOCK, c))
    narrow = lambda rowf, c: pl.BlockSpec((BLOCK, LANES), lambda n: (rowf(n), c // LANES))
    smem = pl.BlockSpec(memory_space=pltpu.SMEM)
    return pl.pallas_call(
        _attn_kernel,
        out_shape=jax.ShapeDtypeStruct((S, D_MIX), BF16),
        grid=(N,),
        in_specs=[
            wide(back(0), COL_QA),
            narrow(back(1), COL_KA), narrow(back(0), COL_KA),
            narrow(back(1), COL_VA), narrow(back(0), COL_VA),
            _resident((2 * BLOCK, BLOCK)), smem, smem, _resident((1, SWA_Q_W)),
            wide(back(0), COL_QS),
            wide(back(1), COL_KS), wide(back(0), COL_KS),
            wide(back(1), COL_VS), wide(back(0), COL_VS),
            wide(back(2), COL_KS), wide(back(2), COL_VS),
            _resident((2 * BLOCK, 2 * BLOCK)), _resident((1, SB_W)),
            pl.BlockSpec(memory_space=pl.ANY),
        ],
        out_specs=pl.BlockSpec((BLOCK, D_MIX), lambda n: (n, 0)),
        scratch_shapes=[
            pltpu.VMEM((2, len(SWA_PAIRS), 2 * BLOCK, 2 * BLOCK), F32),
            pltpu.VMEM((SWA_Q_W, BLOCK), F32),
            pltpu.VMEM((PAIRS, 2 * BLOCK, LANES), BF16),
            pltpu.VMEM((PAIRS, BLOCK, LANES), BF16),
            pltpu.VMEM((PAIRS, BLOCK, LANES), BF16),
            pltpu.VMEM((PAIRS, 2 * BLOCK, BLOCK), F32),
            pltpu.VMEM((PAIRS, 2 * BLOCK, LANES), F32),
            pltpu.SMEM((PAIRS,), jnp.int32),
            pltpu.SemaphoreType.DMA((2,)),
        ],
        compiler_params=_params(("arbitrary",)),
        name="attention",
    )(proj, proj, proj, proj, proj, bucket, rel_bias, sinks, g_a,
      proj, proj, proj, proj, proj, proj, proj, w2, g_b, proj)


def _outproj_kernel(mix_ref, x_ref, gm_ref, w_ref, h_ref, hn_ref):
    h = x_ref[...] + jnp.dot(mix_ref[...], w_ref[...], preferred_element_type=F32)
    h_ref[...] = h
    hn_ref[...] = (_rms(h) * gm_ref[...]).astype(BF16)


def _outproj(mix, x, gm, w_bf16, tm=OUTPROJ_TM):
    S, D = x.shape
    row = lambda i: (i, 0)
    return pl.pallas_call(
        _outproj_kernel,
        out_shape=(jax.ShapeDtypeStruct((S, D), F32), jax.ShapeDtypeStruct((S, D), BF16)),
        grid=(S // tm,),
        in_specs=[
            pl.BlockSpec((tm, D_MIX), row),
            pl.BlockSpec((tm, D), row),
            _resident((1, D)),
            _resident((D_MIX, D)),
        ],
        out_specs=(pl.BlockSpec((tm, D), row), pl.BlockSpec((tm, D), row)),
        compiler_params=_params(("arbitrary",)),
        name="outproj",
    )(mix, x, gm, w_bf16)


def _convglu_kernel(hn_ref, wg_ref, wv_ref, wc_ref, bc_ref, wd_ref, h_ref, gf_ref,
                    o_ref, gate_ref, tail_ref, *, tm):
    i = pl.program_id(0)
    f = pl.program_id(1)

    @pl.when(f == 0)
    def _():
        o_ref[...] = h_ref[...]

    @pl.when(i == 0)
    def _():
        tail_ref[f] = jnp.zeros(tail_ref.shape[1:], F32)

    hn = hn_ref[...]
    gate_ref[:TAIL, :] = tail_ref[f]
    gate_ref[TAIL:, :] = jnp.dot(hn, wg_ref[...], preferred_element_type=F32)
    tail_ref[f] = gate_ref[tm:, :]
    val = jnp.dot(hn, wv_ref[...], preferred_element_type=F32)
    gc = bc_ref[...]
    for tap in range(CONV_WIDTH):
        off = TAIL - (CONV_WIDTH - 1) + tap
        gc = gc + gate_ref[pl.ds(off, tm), :] * wc_ref[tap:tap + 1, :]
    act = (gc * (1.0 / (1.0 + jnp.exp(-gc))) * val).astype(BF16)
    o_ref[...] += jnp.dot(act, wd_ref[...], preferred_element_type=F32)

    @pl.when(f == pl.num_programs(1) - 1)
    def _():
        o_ref[...] = _rms(o_ref[...]) * gf_ref[...]


def _convglu(hn2, h1, w_up_chunks, w_conv, b_conv, w_down_bf16, g_final, tm=CONVGLU_TM):
    S, D = h1.shape
    tf = w_up_chunks.shape[-1]
    nf = D_FF // tf
    return pl.pallas_call(
        functools.partial(_convglu_kernel, tm=tm),
        out_shape=jax.ShapeDtypeStruct((S, D), F32),
        grid=(S // tm, nf),
        in_specs=[
            pl.BlockSpec((tm, D), lambda i, f: (i, 0)),
            pl.BlockSpec((None, D, tf), lambda i, f: (f, 0, 0)),
            pl.BlockSpec((None, D, tf), lambda i, f: (nf + f, 0, 0)),
            pl.BlockSpec((CONV_WIDTH, tf), lambda i, f: (0, f)),
            pl.BlockSpec((1, tf), lambda i, f: (0, f)),
            pl.BlockSpec((tf, D), lambda i, f: (f, 0)),
            pl.BlockSpec((tm, D), lambda i, f: (i, 0)),
            _resident((1, D)),
        ],
        out_specs=pl.BlockSpec((tm, D), lambda i, f: (i, 0)),
        scratch_shapes=[
            pltpu.VMEM((TAIL + tm, tf), F32),
            pltpu.VMEM((nf, TAIL, tf), F32),
        ],
        compiler_params=_params(("arbitrary", "arbitrary"), CONVGLU_VMEM_LIMIT),
        name="convglu",
    )(hn2, w_up_chunks, w_up_chunks, w_conv, b_conv, w_down_bf16, h1, g_final)


def kernel(x, w_in, g_attn_norm, rel_bias, swa_sinks, g_swa_out, g_sb_out, w_out,
           g_mlp_norm, w_up, w_conv, b_conv, w_down, g_final):
    B, S, D = x.shape
    assert (B, S, D) == (1, SEQ, D_MODEL)
    x2 = x.reshape(S, D)

    col = np.ones((1, D_IN), np.float32)
    col[:, COL_QA:COL_KA] = SCALE
    col[:, COL_QS:COL_KS] = SCALE
    w_in_b = (w_in * jnp.asarray(col)).astype(BF16)

    proj, (w_out_b, w_up_b, w_down_b) = _inproj(x2, g_attn_norm.reshape(1, D), w_in_b,
                                                ((w_out, None), (w_up, CONVGLU_TF), (w_down, None)))
    mix = _attention(proj, rel_bias, swa_sinks, g_swa_out.reshape(1, -1), g_sb_out.reshape(1, -1))
    h1, hn2 = _outproj(mix, x2, g_mlp_norm.reshape(1, D), w_out_b)
    out = _convglu(hn2, h1, w_up_b, w_conv, b_conv.reshape(1, -1), w_down_b, g_final.reshape(1, D))
    return out.reshape(B, S, D)
```

```python
import functools
import math

import numpy as np
import jax
import jax.numpy as jnp
from jax import lax
from jax.experimental import pallas as pl
from jax.experimental.pallas import tpu as pltpu

D_MODEL = 2048
SEQ = 16384
HEAD_DIM = 64
SWA_Q_HEADS = 16
SWA_KV_HEADS = 2
SWA_GROUP = SWA_Q_HEADS // SWA_KV_HEADS
SB_HEADS = 16
WINDOW = 128
BLOCK = 128
REL_BUCKETS = 32
REL_MAX_DIST = 128
D_FF = 5632
CONV_WIDTH = 3
EPS = 1e-6
NEG_INF = -1e30

SWA_Q_W = SWA_Q_HEADS * HEAD_DIM
SWA_KV_W = SWA_KV_HEADS * HEAD_DIM
SB_W = SB_HEADS * HEAD_DIM
D_MIX = SWA_Q_W + SB_W
D_IN = SWA_Q_W + 2 * SWA_KV_W + 3 * SB_W

LANES = 128
REF_SPLITS = np.cumsum([0, SWA_Q_W, SWA_KV_W, SWA_KV_W, SB_W, SB_W, SB_W])
COL_QA, COL_KA, COL_VA, COL_QS, COL_KS, COL_VS = (int(c) for c in REF_SPLITS[:-1])
PAIRS = SB_W // LANES
SWA_PAIRS = tuple((h, h + 2) for g in range(SWA_KV_HEADS) for par in (0, 1)
                  for h in range(g * SWA_GROUP + par, (g + 1) * SWA_GROUP, 4))

SCALE = HEAD_DIM ** -0.5
TAIL = 8

V7X_VMEM_BYTES = 64 * 1024 * 1024
VMEM_LIMIT = V7X_VMEM_BYTES - 8 * 1024 * 1024
INPROJ_TM = 512
OUTPROJ_TM = 512
CONVGLU_TM = 1024
CONVGLU_TF = 512
CONVGLU_VMEM_LIMIT = V7X_VMEM_BYTES - 2 * 1024 * 1024

F32 = jnp.float32
BF16 = jnp.bfloat16

PRUNE_LOG = -88.0
LOG2E = math.log2(math.e)
PRUNE_LOG2 = PRUNE_LOG * LOG2E


def _params(sem, vmem=VMEM_LIMIT):
    return pltpu.CompilerParams(dimension_semantics=sem, vmem_limit_bytes=vmem)


def _rms(y):
    return y * lax.rsqrt(jnp.mean(y * y, axis=-1, keepdims=True) + EPS)


def _inproj_kernel(n_later, x_ref, g_ref, w_ref, *refs):
    srcs, o_ref, dsts = refs[:n_later], refs[n_later], refs[n_later + 1:]
    hn = (_rms(x_ref[...]) * g_ref[...]).astype(BF16)
    o_ref[...] = jnp.dot(hn, w_ref[...], preferred_element_type=F32).astype(BF16)
    for src, dst in zip(srcs, dsts):
        if len(dst.shape) == 2:
            dst[...] = src[...].astype(BF16)
        else:
            width = dst.shape[-1]
            for c in range(dst.shape[0]):
                dst[c] = src[:, c * width:(c + 1) * width].astype(BF16)


def _resident(shape):
    return pl.BlockSpec(shape, lambda *_: (0,) * len(shape), pipeline_mode=pl.Buffered(1))


def _inproj(x, g, w_bf16, later_weights, tm=INPROJ_TM):
    S, D = x.shape
    N = w_bf16.shape[1]
    steps = S // tm
    weights = [w for w, _ in later_weights]
    in_slabs = [pl.BlockSpec((w.shape[0] // steps, w.shape[1]), lambda i: (i, 0)) for w in weights]
    out_shapes, out_slabs = [], []
    for w, width in later_weights:
        rows, cols = w.shape
        if width is None:
            out_shapes.append(jax.ShapeDtypeStruct((rows, cols), BF16))
            out_slabs.append(pl.BlockSpec((rows // steps, cols), lambda i: (i, 0)))
        else:
            out_shapes.append(jax.ShapeDtypeStruct((cols // width, rows, width), BF16))
            out_slabs.append(pl.BlockSpec((cols // width, rows // steps, width), lambda i: (0, i, 0)))
    outs = pl.pallas_call(
        functools.partial(_inproj_kernel, len(weights)),
        out_shape=[jax.ShapeDtypeStruct((S, N), BF16)] + out_shapes,
        grid=(steps,),
        in_specs=[
            pl.BlockSpec((tm, D), lambda i: (i, 0)),
            _resident((1, D)),
            _resident((D, N)),
        ] + in_slabs,
        out_specs=[pl.BlockSpec((tm, N), lambda i: (i, 0))] + out_slabs,
        compiler_params=_params(("arbitrary",)),
        name="inproj",
    )(x, g, w_bf16, *weights)
    return outs[0], outs[1:]


def _rel_bucket_table():
    qi = np.arange(BLOCK, dtype=np.int64)[None, :]
    kj = np.arange(2 * BLOCK, dtype=np.int64)[:, None]
    dist = qi + BLOCK - kj
    in_win = (dist >= 0) & (dist < WINDOW)
    dc = np.clip(dist, 0, None)
    max_exact = REL_BUCKETS // 2
    d = np.maximum(dc, 1).astype(np.float32)
    large = max_exact + (np.log(d / np.float32(max_exact)) / np.float32(math.log(REL_MAX_DIST / max_exact))
                         * np.float32(REL_BUCKETS - max_exact)).astype(np.int32)
    large = np.minimum(large, REL_BUCKETS - 1)
    bucket = np.where(dc < max_exact, dc, large).astype(np.int32)
    return np.where(in_win, bucket, -1).astype(np.int32)


def _cumsum_weights():
    kk = np.arange(BLOCK)
    upper = (kk[:, None] > kk[None, :]).astype(np.float32)
    w = np.concatenate([upper, np.ones((BLOCK, BLOCK), np.float32)], axis=1)
    return np.concatenate([w, w], axis=0)


def _interleave(*stage_generators):
    pending = list(stage_generators)
    while pending:
        for g in list(pending):
            try:
                next(g)
            except StopIteration:
                pending.remove(g)


def _swa_stages(q_ref, kp_ref, kc_ref, vp_ref, vc_ref, bias_ref, variant, sink_ref, g_ref, yt_ref, o_ref):
    lane = lax.broadcasted_iota(jnp.int32, (BLOCK, LANES), 1)
    lo = lane < HEAD_DIM
    second = lax.broadcasted_iota(jnp.int32, (1, 2 * BLOCK), 1) >= BLOCK

    k2 = jnp.concatenate([kp_ref[...], kc_ref[...]], axis=0)
    k2s = jnp.concatenate([k2[:, HEAD_DIM:], k2[:, :HEAD_DIM]], axis=1)
    v2t = jnp.concatenate([vp_ref[...], vc_ref[...]], axis=0).T

    def masked_q(h):
        q = q_ref[:, (h // 2) * LANES:(h // 2 + 1) * LANES]
        keep = lo if h % 2 == 0 else jnp.logical_not(lo)
        return jnp.where(keep, q, jnp.zeros_like(q))

    idx = range(len(SWA_PAIRS))
    logits, sinks = [], []
    for i, (ha, hb) in enumerate(SWA_PAIRS):
        group = ha // SWA_GROUP
        keys = k2 if (ha % 2) == group else k2s
        qw = jnp.concatenate([masked_q(ha), masked_q(hb)], axis=0)
        st = lax.dot_general(keys, qw, (((1,), (1,)), ((), ())), preferred_element_type=F32)
        logits.append(st + bias_ref[variant, i])
        sinks.append(jnp.where(second, sink_ref[hb], sink_ref[ha]))
    yield
    ms = [jnp.maximum(jnp.max(logits[i], axis=0, keepdims=True), sinks[i]) for i in idx]
    ps = [jnp.exp(logits[i] - ms[i]) for i in idx]
    yield
    invs = [1.0 / (jnp.sum(ps[i], axis=0, keepdims=True) + jnp.exp(sinks[i] - ms[i])) for i in idx]
    yield
    for i, (ha, hb) in enumerate(SWA_PAIRS):
        w = (ps[i] * invs[i]).astype(BF16)
        out = jnp.dot(v2t, w, preferred_element_type=F32)
        rows = slice((ha // SWA_GROUP) * HEAD_DIM, (ha // SWA_GROUP + 1) * HEAD_DIM)
        yt_ref[ha * HEAD_DIM:(ha + 1) * HEAD_DIM, :] = out[rows, :BLOCK]
        yt_ref[hb * HEAD_DIM:(hb + 1) * HEAD_DIM, :] = out[rows, BLOCK:]
    yield
    yt = yt_ref[...]
    inv = lax.rsqrt(jnp.mean(yt * yt, axis=0, keepdims=True) + EPS)
    o_ref[:, :SWA_Q_W] = ((yt * inv).T * g_ref[...]).astype(BF16)


def _sb_stages(items, w2, results):
    idx = range(len(items))
    nblk = items[0][1].shape[0] // BLOCK
    zs = [lax.dot_general(qq, kb, (((1,), (1,)), ((), ())), preferred_element_type=F32) * LOG2E
          for qq, kb, _, _, _ in items]
    yield
    lgs = [jnp.log2(1.0 + jnp.exp2(-jnp.abs(z))) for z in zs]
    logsigs = [jnp.minimum(zs[i], 0.0) - lgs[i] for i in idx]
    log1ms = [logsigs[i] - zs[i] for i in idx]
    yield
    runnings = [it[3] for it in items]
    parts = [[None] * nblk for _ in idx]
    for c in reversed(range(nblk)):
        cs = slice(c * BLOCK, (c + 1) * BLOCK)
        rs = []
        for i in idx:
            mask = items[i][4][c]
            l1 = log1ms[i][:, cs]
            if mask is not None:
                l1 = jnp.where(mask, l1, 0.0)
            hi = l1.astype(BF16)
            lo = (l1 - hi.astype(F32)).astype(BF16)
            rs.append(jnp.dot(jnp.concatenate([hi, lo], axis=1), w2, preferred_element_type=F32))
        yield
        for i in idx:
            mask = items[i][4][c]
            log_a = logsigs[i][:, cs] + rs[i][:, :BLOCK]
            if runnings[i] is not None:
                log_a = log_a + runnings[i]
            a = jnp.exp2(log_a)
            if mask is not None:
                a = jnp.where(mask, a, 0.0)
            parts[i][c] = a.astype(BF16)
            runnings[i] = rs[i][:, BLOCK:] if runnings[i] is None else runnings[i] + rs[i][:, BLOCK:]
        yield
    for i in idx:
        amat = parts[i][0] if nblk == 1 else jnp.concatenate(parts[i], axis=1)
        results.append((runnings[i], jnp.dot(amat, items[i][2], preferred_element_type=F32)))


def _attn_kernel(qa_ref, kap_ref, kac_ref, vap_ref, vac_ref, bucket_ref, relb_ref, sink_ref, ga_ref,
                 q_ref, kp_ref, kc_ref, vp_ref, vc_ref, kp2_ref, vp2_ref, w2_ref, gb_ref, proj_hbm,
                 o_ref, bias_ref, yt_ref, qq_ref, kd_ref, vd_ref, acc_ref, oacc_ref, live_ref, sem):
    n = pl.program_id(0)

    @pl.when(n == 0)
    def _():
        bucket = bucket_ref[...]
        krow = lax.broadcasted_iota(jnp.int32, bucket.shape, 0)
        for i, pair in enumerate(SWA_PAIRS):
            for side, h in enumerate(pair):
                t = jnp.full(bucket.shape, NEG_INF, F32)
                for r in range(REL_BUCKETS):
                    t = jnp.where(bucket == r, relb_ref[r, h], t)
                cols = slice(side * BLOCK, (side + 1) * BLOCK)
                bias_ref[0, i, :, cols] = t
                bias_ref[1, i, :, cols] = jnp.where(krow >= BLOCK, t, NEG_INF)

    lane = lax.broadcasted_iota(jnp.int32, (BLOCK, LANES), 1)
    first = lane < HEAD_DIM
    w2 = w2_ref[...]
    qrow = lax.broadcasted_iota(jnp.int32, (2 * BLOCK, BLOCK), 0) % BLOCK
    kcol = lax.broadcasted_iota(jnp.int32, (2 * BLOCK, BLOCK), 1)
    diag = kcol < qrow

    def phase1(with_prev):
        items = []
        for p in range(PAIRS):
            cols = slice(p * LANES, (p + 1) * LANES)
            q = q_ref[:, cols]
            zq = jnp.zeros_like(q)
            qq = jnp.concatenate([jnp.where(first, q, zq), jnp.where(first, zq, q)], axis=0)
            qq_ref[p] = qq
            kd_ref[p] = kp2_ref[:, cols]
            vd_ref[p] = vp2_ref[:, cols]
            if with_prev:
                kb = jnp.concatenate([kp_ref[:, cols], kc_ref[:, cols]], axis=0)
                vb = jnp.concatenate([vp_ref[:, cols], vc_ref[:, cols]], axis=0)
                masks = [None, diag]
            else:
                kb, vb, masks = kc_ref[:, cols], vc_ref[:, cols], [diag]
            items.append((qq, kb, vb, None, masks))
        results = []
        _interleave(
            _sb_stages(items, w2, results),
            _swa_stages(qa_ref, kap_ref, kac_ref, vap_ref, vac_ref, bias_ref, 0 if with_prev else 1,
                        sink_ref, ga_ref, yt_ref, o_ref))
        for p, (acc, pv) in enumerate(results):
            acc_ref[p] = acc
            oacc_ref[p] = pv
            live_ref[p] = (jnp.max(acc) > PRUNE_LOG2).astype(jnp.int32)

    pl.when(n > 0)(functools.partial(phase1, True))
    pl.when(n == 0)(functools.partial(phase1, False))

    def pair_body(p, carry):
        def live():
            return (jnp.max(acc_ref[p]) > PRUNE_LOG2).astype(jnp.int32)

        def cond(c):
            j, go = c
            return jnp.logical_and(j >= 0, go > 0)

        def fetch(j, dst, col0, slot):
            src = proj_hbm.at[pl.ds(pl.multiple_of(j * BLOCK, BLOCK), BLOCK),
                              pl.ds(pl.multiple_of(col0 + p * LANES, LANES), LANES)]
            return pltpu.make_async_copy(src, dst.at[p], sem.at[slot])

        def body(c):
            j, _ = c

            @pl.when(j < n - 2)
            def _():
                ck = fetch(j, kd_ref, COL_KS, 0)
                cv = fetch(j, vd_ref, COL_VS, 1)
                ck.start()
                cv.start()
                ck.wait()
                cv.wait()

            results = []
            _interleave(_sb_stages([(qq_ref[p], kd_ref[p], vd_ref[p], acc_ref[p], [None])], w2, results))
            (acc, pv), = results
            acc_ref[p] = acc
            oacc_ref[p] += pv
            return j - 1, live()

        lax.while_loop(cond, body, (n - 2, live_ref[p]))
        return carry

    lax.fori_loop(0, PAIRS, pair_body, 0)

    ys = [jnp.where(first, oacc_ref[p, :BLOCK, :], oacc_ref[p, BLOCK:, :]) for p in range(PAIRS)]
    sq = functools.reduce(lambda a, b: a + b, [y * y for y in ys])
    inv = lax.rsqrt(jnp.sum(sq, axis=-1, keepdims=True) * (1.0 / SB_W) + EPS)
    for p in range(PAIRS):
        cols = slice(p * LANES, (p + 1) * LANES)
        o_ref[:, SWA_Q_W + p * LANES:SWA_Q_W + (p + 1) * LANES] = (ys[p] * inv * gb_ref[:, cols]).astype(BF16)


def _attention(proj, rel_bias, sinks, g_a, g_b):
    S = proj.shape[0]
    N = S // BLOCK
    bucket = jnp.asarray(_rel_bucket_table())
    w2 = jnp.asarray(_cumsum_weights(), dtype=BF16)
    back = lambda d: (lambda n: jnp.maximum(n - d, 0))
    wide = lambda rowf, c: pl.BlockSpec((pl.Element(BLOCK), pl.Element(SB_W)), lambda n: (rowf(n) * BLOCK, c))
    narrow = lambda rowf, c: pl.BlockSpec((BLOCK, LANES), lambda n: (rowf(n), c // LANES))
    smem = pl.BlockSpec(memory_space=pltpu.SMEM)
    return pl.pallas_call(
        _attn_kernel,
        out_shape=jax.ShapeDtypeStruct((S, D_MIX), BF16),
        grid=(N,),
        in_specs=[
            wide(back(0), COL_QA),
            narrow(back(1), COL_KA), narrow(back(0), COL_KA),
            narrow(back(1), COL_VA), narrow(back(0), COL_VA),
            _resident((2 * BLOCK, BLOCK)), smem, smem, _resident((1, SWA_Q_W)),
            wide(back(0), COL_QS),
            wide(back(1), COL_KS), wide(back(0), COL_KS),
            wide(back(1), COL_VS), wide(back(0), COL_VS),
            wide(back(2), COL_KS), wide(back(2), COL_VS),
            _resident((2 * BLOCK, 2 * BLOCK)), _resident((1, SB_W)),
            pl.BlockSpec(memory_space=pl.ANY),
        ],
        out_specs=pl.BlockSpec((BLOCK, D_MIX), lambda n: (n, 0)),
        scratch_shapes=[
            pltpu.VMEM((2, len(SWA_PAIRS), 2 * BLOCK, 2 * BLOCK), F32),
            pltpu.VMEM((SWA_Q_W, BLOCK), F32),
            pltpu.VMEM((PAIRS, 2 * BLOCK, LANES), BF16),
            pltpu.VMEM((PAIRS, BLOCK, LANES), BF16),
            pltpu.VMEM((PAIRS, BLOCK, LANES), BF16),
            pltpu.VMEM((PAIRS, 2 * BLOCK, BLOCK), F32),
            pltpu.VMEM((PAIRS, 2 * BLOCK, LANES), F32),
            pltpu.SMEM((PAIRS,), jnp.int32),
            pltpu.SemaphoreType.DMA((2,)),
        ],
        compiler_params=_params(("arbitrary",)),
        name="attention",
    )(proj, proj, proj, proj, proj, bucket, rel_bias, sinks, g_a,
      proj, proj, proj, proj, proj, proj, proj, w2, g_b, proj)


def _outproj_kernel(mix_ref, x_ref, gm_ref, w_ref, h_ref, hn_ref):
    h = x_ref[...] + jnp.dot(mix_ref[...], w_ref[...], preferred_element_type=F32)
    h_ref[...] = h
    hn_ref[...] = (_rms(h) * gm_ref[...]).astype(BF16)


def _outproj(mix, x, gm, w_bf16, tm=OUTPROJ_TM):
    S, D = x.shape
    row = lambda i: (i, 0)
    return pl.pallas_call(
        _outproj_kernel,
        out_shape=(jax.ShapeDtypeStruct((S, D), F32), jax.ShapeDtypeStruct((S, D), BF16)),
        grid=(S // tm,),
        in_specs=[
            pl.BlockSpec((tm, D_MIX), row),
            pl.BlockSpec((tm, D), row),
            _resident((1, D)),
            _resident((D_MIX, D)),
        ],
        out_specs=(pl.BlockSpec((tm, D), row), pl.BlockSpec((tm, D), row)),
        compiler_params=_params(("arbitrary",)),
        name="outproj",
    )(mix, x, gm, w_bf16)


def _convglu_kernel(hn_ref, wg_ref, wv_ref, wc_ref, bc_ref, wd_ref, h_ref, gf_ref,
                    o_ref, gate_ref, tail_ref, *, tm):
    i = pl.program_id(0)
    f = pl.program_id(1)

    @pl.when(f == 0)
    def _():
        o_ref[...] = h_ref[...]

    @pl.when(i == 0)
    def _():
        tail_ref[f] = jnp.zeros(tail_ref.shape[1:], F32)

    hn = hn_ref[...]
    gate_ref[:TAIL, :] = tail_ref[f]
    gate_ref[TAIL:, :] = jnp.dot(hn, wg_ref[...], preferred_element_type=F32)
    tail_ref[f] = gate_ref[tm:, :]
    val = jnp.dot(hn, wv_ref[...], preferred_element_type=F32)
    gc = bc_ref[...]
    for tap in range(CONV_WIDTH):
        off = TAIL - (CONV_WIDTH - 1) + tap
        gc = gc + gate_ref[pl.ds(off, tm), :] * wc_ref[tap:tap + 1, :]
    act = (gc * (1.0 / (1.0 + jnp.exp(-gc))) * val).astype(BF16)
    o_ref[...] += jnp.dot(act, wd_ref[...], preferred_element_type=F32)

    @pl.when(f == pl.num_programs(1) - 1)
    def _():
        o_ref[...] = _rms(o_ref[...]) * gf_ref[...]


def _convglu(hn2, h1, w_up_chunks, w_conv, b_conv, w_down_bf16, g_final, tm=CONVGLU_TM):
    S, D = h1.shape
    tf = w_up_chunks.shape[-1]
    nf = D_FF // tf
    return pl.pallas_call(
        functools.partial(_convglu_kernel, tm=tm),
        out_shape=jax.ShapeDtypeStruct((S, D), F32),
        grid=(S // tm, nf),
        in_specs=[
            pl.BlockSpec((tm, D), lambda i, f: (i, 0)),
            pl.BlockSpec((None, D, tf), lambda i, f: (f, 0, 0)),
            pl.BlockSpec((None, D, tf), lambda i, f: (nf + f, 0, 0)),
            pl.BlockSpec((CONV_WIDTH, tf), lambda i, f: (0, f)),
            pl.BlockSpec((1, tf), lambda i, f: (0, f)),
            pl.BlockSpec((tf, D), lambda i, f: (f, 0)),
            pl.BlockSpec((tm, D), lambda i, f: (i, 0)),
            _resident((1, D)),
        ],
        out_specs=pl.BlockSpec((tm, D), lambda i, f: (i, 0)),
        scratch_shapes=[
            pltpu.VMEM((TAIL + tm, tf), F32),
            pltpu.VMEM((nf, TAIL, tf), F32),
        ],
        compiler_params=_params(("arbitrary", "arbitrary"), CONVGLU_VMEM_LIMIT),
        name="convglu",
    )(hn2, w_up_chunks, w_up_chunks, w_conv, b_conv, w_down_bf16, h1, g_final)


def kernel(x, w_in, g_attn_norm, rel_bias, swa_sinks, g_swa_out, g_sb_out, w_out,
           g_mlp_norm, w_up, w_conv, b_conv, w_down, g_final):
    B, S, D = x.shape
    assert (B, S, D) == (1, SEQ, D_MODEL)
    x2 = x.reshape(S, D)

    col = np.ones((1, D_IN), np.float32)
    col[:, COL_QA:COL_KA] = SCALE
    col[:, COL_QS:COL_KS] = SCALE
    w_in_b = (w_in * jnp.asarray(col)).astype(BF16)

    proj, (w_out_b, w_up_b, w_down_b) = _inproj(x2, g_attn_norm.reshape(1, D), w_in_b,
                                                ((w_out, None), (w_up, CONVGLU_TF), (w_down, None)))
    mix = _attention(proj, rel_bias, swa_sinks, g_swa_out.reshape(1, -1), g_sb_out.reshape(1, -1))
    h1, hn2 = _outproj(mix, x2, g_mlp_norm.reshape(1, D), w_out_b)
    out = _convglu(hn2, h1, w_up_b, w_conv, b_conv.reshape(1, -1), w_down_b, g_final.reshape(1, D))
    return out.reshape(B, S, D)
```

```python
import functools
import math

import numpy as np
import jax
import jax.numpy as jnp
from jax import lax
from jax.experimental import pallas as pl
from jax.experimental.pallas import tpu as pltpu

D_MODEL = 2048
SEQ = 16384
HEAD_DIM = 64
SWA_Q_HEADS = 16
SWA_KV_HEADS = 2
SWA_GROUP = SWA_Q_HEADS // SWA_KV_HEADS
SB_HEADS = 16
WINDOW = 128
BLOCK = 128
REL_BUCKETS = 32
REL_MAX_DIST = 128
D_FF = 5632
CONV_WIDTH = 3
EPS = 1e-6
NEG_INF = -1e30

SWA_Q_W = SWA_Q_HEADS * HEAD_DIM
SWA_KV_W = SWA_KV_HEADS * HEAD_DIM
SB_W = SB_HEADS * HEAD_DIM
D_MIX = SWA_Q_W + SB_W
D_IN = SWA_Q_W + 2 * SWA_KV_W + 3 * SB_W

LANES = 128
REF_SPLITS = np.cumsum([0, SWA_Q_W, SWA_KV_W, SWA_KV_W, SB_W, SB_W, SB_W])
COL_QA, COL_KA, COL_VA, COL_QS, COL_KS, COL_VS = (int(c) for c in REF_SPLITS[:-1])
PAIRS = SB_W // LANES
SWA_PAIRS = tuple((h, h + 2) for g in range(SWA_KV_HEADS) for par in (0, 1)
                  for h in range(g * SWA_GROUP + par, (g + 1) * SWA_GROUP, 4))

SCALE = HEAD_DIM ** -0.5
TAIL = 8

V7X_VMEM_BYTES = 64 * 1024 * 1024
VMEM_LIMIT = V7X_VMEM_BYTES - 8 * 1024 * 1024
INPROJ_TM = 512
OUTPROJ_TM = 512
CONVGLU_TM = 1024
CONVGLU_TF = 512
CONVGLU_VMEM_LIMIT = V7X_VMEM_BYTES - 2 * 1024 * 1024

F32 = jnp.float32
BF16 = jnp.bfloat16

PRUNE_LOG = -88.0
LOG2E = math.log2(math.e)
PRUNE_LOG2 = PRUNE_LOG * LOG2E


def _params(sem, vmem=VMEM_LIMIT):
    return pltpu.CompilerParams(dimension_semantics=sem, vmem_limit_bytes=vmem)


def _rms(y):
    return y * lax.rsqrt(jnp.mean(y * y, axis=-1, keepdims=True) + EPS)


def _inproj_kernel(n_later, x_ref, g_ref, w_ref, *refs):
    srcs, o_ref, dsts = refs[:n_later], refs[n_later], refs[n_later + 1:]
    hn = (_rms(x_ref[...]) * g_ref[...]).astype(BF16)
    o_ref[...] = jnp.dot(hn, w_ref[...], preferred_element_type=F32).astype(BF16)
    for src, dst in zip(srcs, dsts):
        if len(dst.shape) == 2:
            dst[...] = src[...].astype(BF16)
        else:
            width = dst.shape[-1]
            for c in range(dst.shape[0]):
                dst[c] = src[:, c * width:(c + 1) * width].astype(BF16)


def _resident(shape):
    return pl.BlockSpec(shape, lambda *_: (0,) * len(shape), pipeline_mode=pl.Buffered(1))


def _inproj(x, g, w_bf16, later_weights, tm=INPROJ_TM):
    S, D = x.shape
    N = w_bf16.shape[1]
    steps = S // tm
    weights = [w for w, _ in later_weights]
    in_slabs = [pl.BlockSpec((w.shape[0] // steps, w.shape[1]), lambda i: (i, 0)) for w in weights]
    out_shapes, out_slabs = [], []
    for w, width in later_weights:
        rows, cols = w.shape
        if width is None:
            out_shapes.append(jax.ShapeDtypeStruct((rows, cols), BF16))
            out_slabs.append(pl.BlockSpec((rows // steps, cols), lambda i: (i, 0)))
        else:
            out_shapes.append(jax.ShapeDtypeStruct((cols // width, rows, width), BF16))
            out_slabs.append(pl.BlockSpec((cols // width, rows // steps, width), lambda i: (0, i, 0)))
    outs = pl.pallas_call(
        functools.partial(_inproj_kernel, len(weights)),
        out_shape=[jax.ShapeDtypeStruct((S, N), BF16)] + out_shapes,
        grid=(steps,),
        in_specs=[
            pl.BlockSpec((tm, D), lambda i: (i, 0)),
            _resident((1, D)),
            _resident((D, N)),
        ] + in_slabs,
        out_specs=[pl.BlockSpec((tm, N), lambda i: (i, 0))] + out_slabs,
        compiler_params=_params(("arbitrary",)),
        name="inproj",
    )(x, g, w_bf16, *weights)
    return outs[0], outs[1:]


def _rel_bucket_table():
    qi = np.arange(BLOCK, dtype=np.int64)[None, :]
    kj = np.arange(2 * BLOCK, dtype=np.int64)[:, None]
    dist = qi + BLOCK - kj
    in_win = (dist >= 0) & (dist < WINDOW)
    dc = np.clip(dist, 0, None)
    max_exact = REL_BUCKETS // 2
    d = np.maximum(dc, 1).astype(np.float32)
    large = max_exact + (np.log(d / np.float32(max_exact)) / np.float32(math.log(REL_MAX_DIST / max_exact))
                         * np.float32(REL_BUCKETS - max_exact)).astype(np.int32)
    large = np.minimum(large, REL_BUCKETS - 1)
    bucket = np.where(dc < max_exact, dc, large).astype(np.int32)
    return np.where(in_win, bucket, -1).astype(np.int32)


def _cumsum_weights():
    kk = np.arange(BLOCK)
    upper = (kk[:, None] > kk[None, :]).astype(np.float32)
    w = np.concatenate([upper, np.ones((BLOCK, BLOCK), np.float32)], axis=1)
    return np.concatenate([w, w], axis=0)


def _interleave(*stage_generators):
    pending = list(stage_generators)
    while pending:
        for g in list(pending):
            try:
                next(g)
            except StopIteration:
                pending.remove(g)


def _swa_stages(q_ref, kp_ref, kc_ref, vp_ref, vc_ref, bias_ref, variant, sink_ref, g_ref, yt_ref, o_ref):
    lane = lax.broadcasted_iota(jnp.int32, (BLOCK, LANES), 1)
    lo = lane < HEAD_DIM
    second = lax.broadcasted_iota(jnp.int32, (1, 2 * BLOCK), 1) >= BLOCK

    k2 = jnp.concatenate([kp_ref[...], kc_ref[...]], axis=0)
    k2s = jnp.concatenate([k2[:, HEAD_DIM:], k2[:, :HEAD_DIM]], axis=1)
    v2t = jnp.concatenate([vp_ref[...], vc_ref[...]], axis=0).T

    def masked_q(h):
        q = q_ref[:, (h // 2) * LANES:(h // 2 + 1) * LANES]
        keep = lo if h % 2 == 0 else jnp.logical_not(lo)
        return jnp.where(keep, q, jnp.zeros_like(q))

    idx = range(len(SWA_PAIRS))
    logits, sinks = [], []
    for i, (ha, hb) in enumerate(SWA_PAIRS):
        group = ha // SWA_GROUP
        keys = k2 if (ha % 2) == group else k2s
        qw = jnp.concatenate([masked_q(ha), masked_q(hb)], axis=0)
        st = lax.dot_general(keys, qw, (((1,), (1,)), ((), ())), preferred_element_type=F32)
        logits.append(st + bias_ref[variant, i])
        sinks.append(jnp.where(second, sink_ref[hb], sink_ref[ha]))
    yield
    ms = [jnp.maximum(jnp.max(logits[i], axis=0, keepdims=True), sinks[i]) for i in idx]
    ps = [jnp.exp(logits[i] - ms[i]) for i in idx]
    yield
    invs = [1.0 / (jnp.sum(ps[i], axis=0, keepdims=True) + jnp.exp(sinks[i] - ms[i])) for i in idx]
    yield
    for i, (ha, hb) in enumerate(SWA_PAIRS):
        w = (ps[i] * invs[i]).astype(BF16)
        out = jnp.dot(v2t, w, preferred_element_type=F32)
        rows = slice((ha // SWA_GROUP) * HEAD_DIM, (ha // SWA_GROUP + 1) * HEAD_DIM)
        yt_ref[ha * HEAD_DIM:(ha + 1) * HEAD_DIM, :] = out[rows, :BLOCK]
        yt_ref[hb * HEAD_DIM:(hb + 1) * HEAD_DIM, :] = out[rows, BLOCK:]
    yield
    yt = yt_ref[...]
    inv = lax.rsqrt(jnp.mean(yt * yt, axis=0, keepdims=True) + EPS)
    o_ref[:, :SWA_Q_W] = ((yt * inv).T * g_ref[...]).astype(BF16)


def _sb_stages(items, w2, results):
    idx = range(len(items))
    nblk = items[0][1].shape[0] // BLOCK
    zs = [lax.dot_general(qq, kb, (((1,), (1,)), ((), ())), preferred_element_type=F32) * LOG2E
          for qq, kb, _, _, _ in items]
    yield
    lgs = [jnp.log2(1.0 + jnp.exp2(-jnp.abs(z))) for z in zs]
    logsigs = [jnp.minimum(zs[i], 0.0) - lgs[i] for i in idx]
    log1ms = [logsigs[i] - zs[i] for i in idx]
    yield
    runnings = [it[3] for it in items]
    parts = [[None] * nblk for _ in idx]
    for c in reversed(range(nblk)):
        cs = slice(c * BLOCK, (c + 1) * BLOCK)
        rs = []
        for i in idx:
            mask = items[i][4][c]
            l1 = log1ms[i][:, cs]
            if mask is not None:
                l1 = jnp.where(mask, l1, 0.0)
            hi = l1.astype(BF16)
            lo = (l1 - hi.astype(F32)).astype(BF16)
            rs.append(jnp.dot(jnp.concatenate([hi, lo], axis=1), w2, preferred_element_type=F32))
        yield
        for i in idx:
            mask = items[i][4][c]
            log_a = logsigs[i][:, cs] + rs[i][:, :BLOCK]
            if runnings[i] is not None:
                log_a = log_a + runnings[i]
            a = jnp.exp2(log_a)
            if mask is not None:
                a = jnp.where(mask, a, 0.0)
            parts[i][c] = a.astype(BF16)
            runnings[i] = rs[i][:, BLOCK:] if runnings[i] is None else runnings[i] + rs[i][:, BLOCK:]
        yield
    for i in idx:
        amat = parts[i][0] if nblk == 1 else jnp.concatenate(parts[i], axis=1)
        results.append((runnings[i], jnp.dot(amat, items[i][2], preferred_element_type=F32)))


def _attn_kernel(qa_ref, kap_ref, kac_ref, vap_ref, vac_ref, bucket_ref, relb_ref, sink_ref, ga_ref,
                 q_ref, kp_ref, kc_ref, vp_ref, vc_ref, kp2_ref, vp2_ref, w2_ref, gb_ref, proj_hbm,
                 o_ref, bias_ref, yt_ref, qq_ref, kd_ref, vd_ref, acc_ref, oacc_ref, live_ref, sem):
    n = pl.program_id(0)

    @pl.when(n == 0)
    def _():
        bucket = bucket_ref[...]
        krow = lax.broadcasted_iota(jnp.int32, bucket.shape, 0)
        for i, pair in enumerate(SWA_PAIRS):
            for side, h in enumerate(pair):
                t = jnp.full(bucket.shape, NEG_INF, F32)
                for r in range(REL_BUCKETS):
                    t = jnp.where(bucket == r, relb_ref[r, h], t)
                cols = slice(side * BLOCK, (side + 1) * BLOCK)
                bias_ref[0, i, :, cols] = t
                bias_ref[1, i, :, cols] = jnp.where(krow >= BLOCK, t, NEG_INF)

    lane = lax.broadcasted_iota(jnp.int32, (BLOCK, LANES), 1)
    first = lane < HEAD_DIM
    w2 = w2_ref[...]
    qrow = lax.broadcasted_iota(jnp.int32, (2 * BLOCK, BLOCK), 0) % BLOCK
    kcol = lax.broadcasted_iota(jnp.int32, (2 * BLOCK, BLOCK), 1)
    diag = kcol < qrow

    def phase1(with_prev):
        items = []
        for p in range(PAIRS):
            cols = slice(p * LANES, (p + 1) * LANES)
            q = q_ref[:, cols]
            zq = jnp.zeros_like(q)
            qq = jnp.concatenate([jnp.where(first, q, zq), jnp.where(first, zq, q)], axis=0)
            qq_ref[p] = qq
            kd_ref[p] = kp2_ref[:, cols]
            vd_ref[p] = vp2_ref[:, cols]
            if with_prev:
                kb = jnp.concatenate([kp_ref[:, cols], kc_ref[:, cols]], axis=0)
                vb = jnp.concatenate([vp_ref[:, cols], vc_ref[:, cols]], axis=0)
                masks = [None, diag]
            else:
                kb, vb, masks = kc_ref[:, cols], vc_ref[:, cols], [diag]
            items.append((qq, kb, vb, None, masks))
        results = []
        _interleave(
            _swa_stages(qa_ref, kap_ref, kac_ref, vap_ref, vac_ref, bias_ref, 0 if with_prev else 1,
                        sink_ref, ga_ref, yt_ref, o_ref),
            _sb_stages(items, w2, results))
        for p, (acc, pv) in enumerate(results):
            acc_ref[p] = acc
            oacc_ref[p] = pv
            live_ref[p] = (jnp.max(acc[:, :1]) > PRUNE_LOG2).astype(jnp.int32)

    pl.when(n > 0)(functools.partial(phase1, True))
    pl.when(n == 0)(functools.partial(phase1, False))

    def pair_body(p, carry):
        def live():
            return (jnp.max(acc_ref[p, :, :1]) > PRUNE_LOG2).astype(jnp.int32)

        def cond(c):
            j, go = c
            return jnp.logical_and(j >= 0, go > 0)

        def fetch(j, dst, col0, slot):
            src = proj_hbm.at[pl.ds(pl.multiple_of(j * BLOCK, BLOCK), BLOCK),
                              pl.ds(pl.multiple_of(col0 + p * LANES, LANES), LANES)]
            return pltpu.make_async_copy(src, dst.at[p], sem.at[slot])

        def body(c):
            j, _ = c

            @pl.when(j < n - 2)
            def _():
                ck = fetch(j, kd_ref, COL_KS, 0)
                cv = fetch(j, vd_ref, COL_VS, 1)
                ck.start()
                cv.start()
                ck.wait()
                cv.wait()

            results = []
            _interleave(_sb_stages([(qq_ref[p], kd_ref[p], vd_ref[p], acc_ref[p], [None])], w2, results))
            (acc, pv), = results
            acc_ref[p] = acc
            oacc_ref[p] += pv
            return j - 1, live()

        lax.while_loop(cond, body, (n - 2, live_ref[p]))
        return carry

    lax.fori_loop(0, PAIRS, pair_body, 0)

    ys = [jnp.where(first, oacc_ref[p, :BLOCK, :], oacc_ref[p, BLOCK:, :]) for p in range(PAIRS)]
    sq = functools.reduce(lambda a, b: a + b, [y * y for y in ys])
    inv = lax.rsqrt(jnp.sum(sq, axis=-1, keepdims=True) * (1.0 / SB_W) + EPS)
    for p in range(PAIRS):
        cols = slice(p * LANES, (p + 1) * LANES)
        o_ref[:, SWA_Q_W + p * LANES:SWA_Q_W + (p + 1) * LANES] = (ys[p] * inv * gb_ref[:, cols]).astype(BF16)


def _attention(proj, rel_bias, sinks, g_a, g_b):
    S = proj.shape[0]
    N = S // BLOCK
    bucket = jnp.asarray(_rel_bucket_table())
    w2 = jnp.asarray(_cumsum_weights(), dtype=BF16)
    back = lambda d: (lambda n: jnp.maximum(n - d, 0))
    wide = lambda rowf, c: pl.BlockSpec((pl.Element(BLOCK), pl.Element(SB_W)), lambda n: (rowf(n) * BLOCK, c))
    narrow = lambda rowf, c: pl.BlockSpec((BLOCK, LANES), lambda n: (rowf(n), c // LANES))
    smem = pl.BlockSpec(memory_space=pltpu.SMEM)
    return pl.pallas_call(
        _attn_kernel,
        out_shape=jax.ShapeDtypeStruct((S, D_MIX), BF16),
        grid=(N,),
        in_specs=[
            wide(back(0), COL_QA),
            narrow(back(1), COL_KA), narrow(back(0), COL_KA),
            narrow(back(1), COL_VA), narrow(back(0), COL_VA),
            _resident((2 * BLOCK, BLOCK)), smem, smem, _resident((1, SWA_Q_W)),
            wide(back(0), COL_QS),
            wide(back(1), COL_KS), wide(back(0), COL_KS),
            wide(back(1), COL_VS), wide(back(0), COL_VS),
            wide(back(2), COL_KS), wide(back(2), COL_VS),
            _resident((2 * BLOCK, 2 * BLOCK)), _resident((1, SB_W)),
            pl.BlockSpec(memory_space=pl.ANY),
        ],
        out_specs=pl.BlockSpec((BLOCK, D_MIX), lambda n: (n, 0)),
        scratch_shapes=[
            pltpu.VMEM((2, len(SWA_PAIRS), 2 * BLOCK, 2 * BLOCK), F32),
            pltpu.VMEM((SWA_Q_W, BLOCK), F32),
            pltpu.VMEM((PAIRS, 2 * BLOCK, LANES), BF16),
            pltpu.VMEM((PAIRS, BLOCK, LANES), BF16),
            pltpu.VMEM((PAIRS, BLOCK, LANES), BF16),
            pltpu.VMEM((PAIRS, 2 * BLOCK, BLOCK), F32),
            pltpu.VMEM((PAIRS, 2 * BLOCK, LANES), F32),
            pltpu.SMEM((PAIRS,), jnp.int32),
            pltpu.SemaphoreType.DMA((2,)),
        ],
        compiler_params=_params(("arbitrary",)),
        name="attention",
    )(proj, proj, proj, proj, proj, bucket, rel_bias, sinks, g_a,
      proj, proj, proj, proj, proj, proj, proj, w2, g_b, proj)


def _outproj_kernel(mix_ref, x_ref, gm_ref, w_ref, h_ref, hn_ref):
    h = x_ref[...] + jnp.dot(mix_ref[...], w_ref[...], preferred_element_type=F32)
    h_ref[...] = h
    hn_ref[...] = (_rms(h) * gm_ref[...]).astype(BF16)


def _outproj(mix, x, gm, w_bf16, tm=OUTPROJ_TM):
    S, D = x.shape
    row = lambda i: (i, 0)
    return pl.pallas_call(
        _outproj_kernel,
        out_shape=(jax.ShapeDtypeStruct((S, D), F32), jax.ShapeDtypeStruct((S, D), BF16)),
        grid=(S // tm,),
        in_specs=[
            pl.BlockSpec((tm, D_MIX), row),
            pl.BlockSpec((tm, D), row),
            _resident((1, D)),
            _resident((D_MIX, D)),
        ],
        out_specs=(pl.BlockSpec((tm, D), row), pl.BlockSpec((tm, D), row)),
        compiler_params=_params(("arbitrary",)),
        name="outproj",
    )(mix, x, gm, w_bf16)


def _convglu_kernel(hn_ref, wg_ref, wv_ref, wc_ref, bc_ref, wd_ref, h_ref, gf_ref,
                    o_ref, gate_ref, tail_ref, *, tm):
    i = pl.program_id(0)
    f = pl.program_id(1)

    @pl.when(f == 0)
    def _():
        o_ref[...] = h_ref[...]

    @pl.when(i == 0)
    def _():
        tail_ref[f] = jnp.zeros(tail_ref.shape[1:], F32)

    hn = hn_ref[...]
    gate_ref[:TAIL, :] = tail_ref[f]
    gate_ref[TAIL:, :] = jnp.dot(hn, wg_ref[...], preferred_element_type=F32)
    tail_ref[f] = gate_ref[tm:, :]
    val = jnp.dot(hn, wv_ref[...], preferred_element_type=F32)
    gc = bc_ref[...]
    for tap in range(CONV_WIDTH):
        off = TAIL - (CONV_WIDTH - 1) + tap
        gc = gc + gate_ref[pl.ds(off, tm), :] * wc_ref[tap:tap + 1, :]
    act = (gc * (1.0 / (1.0 + jnp.exp(-gc))) * val).astype(BF16)
    o_ref[...] += jnp.dot(act, wd_ref[...], preferred_element_type=F32)

    @pl.when(f == pl.num_programs(1) - 1)
    def _():
        o_ref[...] = _rms(o_ref[...]) * gf_ref[...]


def _convglu(hn2, h1, w_up_chunks, w_conv, b_conv, w_down_bf16, g_final, tm=CONVGLU_TM):
    S, D = h1.shape
    tf = w_up_chunks.shape[-1]
    nf = D_FF // tf
    return pl.pallas_call(
        functools.partial(_convglu_kernel, tm=tm),
        out_shape=jax.ShapeDtypeStruct((S, D), F32),
        grid=(S // tm, nf),
        in_specs=[
            pl.BlockSpec((tm, D), lambda i, f: (i, 0)),
            pl.BlockSpec((None, D, tf), lambda i, f: (f, 0, 0)),
            pl.BlockSpec((None, D, tf), lambda i, f: (nf + f, 0, 0)),
            pl.BlockSpec((CONV_WIDTH, tf), lambda i, f: (0, f)),
            pl.BlockSpec((1, tf), lambda i, f: (0, f)),
            pl.BlockSpec((tf, D), lambda i, f: (f, 0)),
            pl.BlockSpec((tm, D), lambda i, f: (i, 0)),
            _resident((1, D)),
        ],
        out_specs=pl.BlockSpec((tm, D), lambda i, f: (i, 0)),
        scratch_shapes=[
            pltpu.VMEM((TAIL + tm, tf), F32),
            pltpu.VMEM((nf, TAIL, tf), F32),
        ],
        compiler_params=_params(("arbitrary", "arbitrary"), CONVGLU_VMEM_LIMIT),
        name="convglu",
    )(hn2, w_up_chunks, w_up_chunks, w_conv, b_conv, w_down_bf16, h1, g_final)


def kernel(x, w_in, g_attn_norm, rel_bias, swa_sinks, g_swa_out, g_sb_out, w_out,
           g_mlp_norm, w_up, w_conv, b_conv, w_down, g_final):
    B, S, D = x.shape
    assert (B, S, D) == (1, SEQ, D_MODEL)
    x2 = x.reshape(S, D)

    col = np.ones((1, D_IN), np.float32)
    col[:, COL_QA:COL_KA] = SCALE
    col[:, COL_QS:COL_KS] = SCALE
    w_in_b = (w_in * jnp.asarray(col)).astype(BF16)

    proj, (w_out_b, w_up_b, w_down_b) = _inproj(x2, g_attn_norm.reshape(1, D), w_in_b,
                                                ((w_out, None), (w_up, CONVGLU_TF), (w_down, None)))
    mix = _attention(proj, rel_bias, swa_sinks, g_swa_out.reshape(1, -1), g_sb_out.reshape(1, -1))
    h1, hn2 = _outproj(mix, x2, g_mlp_norm.reshape(1, D), w_out_b)
    out = _convglu(hn2, h1, w_up_b, w_conv, b_conv.reshape(1, -1), w_down_b, g_final.reshape(1, D))
    return out.reshape(B, S, D)
```
